```python
import math
import jax
import jax.numpy as jnp
from jax import lax
import numpy as np

D_MODEL = 1024
BATCH = 2
SEQ = 8192
DEPTH = 4
DEC_BATCH = 32
DEC_SEQ = 1
PAST_LEN = 8192
PAGE_SIZE = 128

NSA_HEADS = 8
NSA_KV = 2
NSA_GROUP = NSA_HEADS // NSA_KV
NSA_DH = 64
BRANCH_WIDTH = 512
CMP_STRIDE = 16
CMP_BLOCK = 2 * CMP_STRIDE
CMP_HID = 128
SEL_BLOCK = 64
N_SELECT = 16
WINDOW = 512
Q_BLOCK = 128
N_KV_STREAMS = 6
N_CACHE_STREAMS = 4
N_BUCKETS = 32
MAX_DISTANCE = 128
DN_HEADS = 4
DN_DK = 128
DN_DV = 128
DN_CONV = 4
DN_CHUNK = 64
DN_CONV_DIM = DN_HEADS * (2 * DN_DK + DN_DV)
HG_HEADS = 4
HG_DK = 128
HG_DV = 128
HG_CHUNK = 64
N_EXPERTS = 16
N_EXPERT_GROUPS = 4
EXPERTS_PER_GROUP = N_EXPERTS // N_EXPERT_GROUPS
TOP_K = 2
D_EXPERT = 512
MOE_ROW_BLOCK = 128
D_PLE = 256
DEEPNORM_ALPHA = (2 * DEPTH) ** 0.25
DEEPNORM_BETA = (8 * DEPTH) ** -0.25
EPS = 1e-6
NEG = -1e30
IN_SPLITS = (NSA_HEADS * NSA_DH, N_KV_STREAMS * NSA_KV * NSA_DH, 3 * NSA_HEADS,
             DN_CONV_DIM, DN_HEADS, DN_HEADS, DN_HEADS * DN_DV,
             HG_HEADS * HG_DK, HG_HEADS * HG_DK, HG_HEADS * HG_DV, HG_HEADS * HG_DV,
             3 * D_MODEL)
D_IN = sum(IN_SPLITS)

kernel_name = 'nsa_deltanet_hgrn2_moe_decoder_step'


def _split(z, sizes):
    return jnp.split(z, np.cumsum(sizes)[:-1].tolist(), axis=-1)


def _layer_norm(x, g, b):
    x32 = x.astype(jnp.float32)
    mu = x32.mean(-1, keepdims=True)
    var = jnp.square(x32 - mu).mean(-1, keepdims=True)
    return ((x32 - mu) * lax.rsqrt(var + EPS) * g + b).astype(x.dtype)


def _masked_softmax(logits, mask):
    z = jnp.where(mask, logits.astype(jnp.float32), NEG)
    e = jnp.exp(z - z.max(-1, keepdims=True)) * mask
    s = e.sum(-1, keepdims=True)
    return e / jnp.where(s > 0, s, 1.0)


def _l2norm(a):
    a32 = a.astype(jnp.float32)
    return (a32 * lax.rsqrt(jnp.sum(a32 * a32, -1, keepdims=True) + EPS)).astype(a.dtype)


def _gated_rmsnorm(o, gate, w):
    B, L, H, dv = o.shape
    o32 = o.astype(jnp.float32)
    o32 = o32 * lax.rsqrt(jnp.mean(o32 * o32, -1, keepdims=True) + EPS) * w
    return (o32.reshape(B, L, H * dv) * jax.nn.silu(gate.astype(jnp.float32))).astype(gate.dtype)


def _t5_bucket(dist):
    n = jnp.maximum(dist, 0)
    exact = N_BUCKETS // 2
    rel = jnp.log(jnp.maximum(n, 1).astype(jnp.float32) / exact) / math.log(MAX_DISTANCE / exact)
    large = jnp.minimum(exact + (rel * (N_BUCKETS - exact)).astype(jnp.int32), N_BUCKETS - 1)
    return jnp.where(n < exact, n, large)


def _head_bias(dist, rel_bias):
    Q, K = dist.shape
    bias = rel_bias[_t5_bucket(dist)]
    return bias.reshape(Q, K, NSA_KV, NSA_GROUP).transpose(2, 3, 0, 1).astype(jnp.float32)


def _compress(rows, w1, w2, pe):
    B, T, G, dh = rows.shape
    n_chunk = T // CMP_STRIDE
    r = rows[:, :n_chunk * CMP_STRIDE].reshape(B, n_chunk, CMP_STRIDE, G, dh)
    w1h = w1.reshape(2, CMP_STRIDE, dh, CMP_HID)
    lo = jnp.einsum('bcsgd,sdh->bcgh', r, w1h[0])
    hi = jnp.einsum('bcsgd,sdh->bcgh', r, w1h[1])
    h = lo[:, :-1] + hi[:, 1:] + pe.reshape(-1) @ w1
    return jax.nn.silu(h) @ w2


def _importance_map(n_cmp, n_sb):
    c0 = jnp.arange(n_cmp) * CMP_STRIDE
    c1 = c0 + CMP_BLOCK - 1
    b0 = jnp.arange(n_sb) * SEL_BLOCK
    b1 = b0 + SEL_BLOCK - 1
    return ((c0[:, None] <= b1[None, :]) & (c1[:, None] >= b0[None, :])).astype(jnp.float32)


def _nsa_core(q, gates, t, kc, vc, cmp_end, imp_map, ks_blk, vs_blk, kw, vw, w_pos, rel_bias):
    f32 = jnp.float32
    B, Q = q.shape[:2]
    lg = jnp.einsum('bqgjd,bcgd->bgjqc', q, kc).astype(f32) + _head_bias(t[:, None] - cmp_end[None, :], rel_bias)
    p_cmp = _masked_softmax(lg, cmp_end[None, :] <= t[:, None])
    o_cmp = jnp.einsum('bgjqc,bcgd->bqgjd', p_cmp.astype(vc.dtype), vc)
    n_sb = ks_blk.shape[2]
    n_top = min(N_SELECT, n_sb)
    imp = jnp.einsum('bgjqc,cn->bgqn', p_cmp, imp_map)
    blk = jnp.arange(n_sb)[None, :]
    cur = (t // SEL_BLOCK)[:, None]
    forced = (blk == 0) | (blk == cur) | (blk == cur - 1)
    score = jnp.where(blk * SEL_BLOCK <= t[:, None], jnp.where(forced, jnp.inf, imp), -jnp.inf)
    top_s, top_i = lax.top_k(score, n_top)
    b_ix = jnp.arange(B)[:, None, None, None]
    g_ix = jnp.arange(NSA_KV)[None, :, None, None]
    n_keys = n_top * SEL_BLOCK
    ks = ks_blk[b_ix, g_ix, top_i].reshape(B, NSA_KV, Q, n_keys, NSA_DH)
    vs = vs_blk[b_ix, g_ix, top_i].reshape(B, NSA_KV, Q, n_keys, NSA_DH)
    s_pos = (top_i[..., None] * SEL_BLOCK + jnp.arange(SEL_BLOCK)).reshape(B, NSA_KV, Q, n_keys)
    s_ok = jnp.repeat(top_s > -jnp.inf, SEL_BLOCK, axis=-1) & (s_pos <= t[:, None])
    table = rel_bias.reshape(N_BUCKETS, NSA_KV, NSA_GROUP).transpose(1, 2, 0)
    bucket = _t5_bucket(t[:, None] - s_pos)
    s_bias = table[jnp.arange(NSA_KV)[None, :, None, None, None],
                   jnp.arange(NSA_GROUP)[None, None, :, None, None],
                   bucket[:, :, None]].astype(f32)
    lg = jnp.einsum('bqgjd,bgqmd->bgjqm', q, ks).astype(f32) + s_bias
    p_sel = _masked_softmax(lg, s_ok[:, :, None])
    o_sel = jnp.einsum('bgjqm,bgqmd->bqgjd', p_sel.astype(vs.dtype), vs)
    w_ok = (w_pos[None, :] <= t[:, None]) & (w_pos[None, :] > t[:, None] - WINDOW) & (w_pos[None, :] >= 0)
    lg = jnp.einsum('bqgjd,bkgd->bgjqk', q, kw).astype(f32) + _head_bias(t[:, None] - w_pos[None, :], rel_bias)
    p_win = _masked_softmax(lg, w_ok)
    o_win = jnp.einsum('bgjqk,bkgd->bqgjd', p_win.astype(vw.dtype), vw)
    o = gates[..., 0:1] * o_cmp + gates[..., 1:2] * o_sel + gates[..., 2:3] * o_win
    return o.reshape(B, Q, NSA_HEADS * NSA_DH)


def _nsa_mixer(q_raw, kv_raw, gate_raw, past_rows, win_buf, w_cmp1, w_cmp2, pe_cmp, rel_bias):
    B, L, _ = q_raw.shape
    q = (q_raw * NSA_DH ** -0.5).reshape(B, L, NSA_KV, NSA_GROUP, NSA_DH)
    gates = jax.nn.sigmoid(gate_raw).reshape(B, L, NSA_KV, NSA_GROUP, 3)
    kv = kv_raw.reshape(B, L, N_KV_STREAMS, NSA_KV, NSA_DH)
    new_rows, new_win = kv[:, :, :N_CACHE_STREAMS], kv[:, :, N_CACHE_STREAMS:]
    pos0 = 0 if past_rows is None else past_rows.shape[1]
    rows = new_rows if past_rows is None else jnp.concatenate([past_rows, new_rows], 1)
    T = rows.shape[1]
    kc = _compress(rows[:, :, 0], w_cmp1[0], w_cmp2[0], pe_cmp[0])
    vc = _compress(rows[:, :, 1], w_cmp1[1], w_cmp2[1], pe_cmp[1])
    n_cmp = kc.shape[1]
    cmp_end = jnp.arange(n_cmp) * CMP_STRIDE + CMP_BLOCK - 1
    n_sb = -(-T // SEL_BLOCK)
    sel = jnp.pad(rows[:, :, 2:4], ((0, 0), (0, n_sb * SEL_BLOCK - T), (0, 0), (0, 0), (0, 0)))
    sel = sel.reshape(B, n_sb, SEL_BLOCK, 2, NSA_KV, NSA_DH).transpose(3, 0, 4, 1, 2, 5)
    imp_map = _importance_map(n_cmp, n_sb)
    wrows = new_win if win_buf is None else jnp.concatenate([win_buf, new_win], 1)
    n_buf = wrows.shape[1] - L

    def attend(q_b, g_b, t, kw, w_pos):
        return _nsa_core(q_b, g_b, t, kc, vc, cmp_end, imp_map, sel[0], sel[1],
                         kw[:, :, 0], kw[:, :, 1], w_pos, rel_bias)

    if past_rows is None:
        qb = min(Q_BLOCK, L)
        wpad = jnp.pad(wrows, ((0, 0), (WINDOW, 0), (0, 0), (0, 0), (0, 0)))

        def block(i):
            s0 = i * qb
            return attend(lax.dynamic_slice_in_dim(q, s0, qb, 1),
                          lax.dynamic_slice_in_dim(gates, s0, qb, 1),
                          s0 + jnp.arange(qb),
                          lax.dynamic_slice_in_dim(wpad, s0, WINDOW + qb, 1),
                          s0 - WINDOW + jnp.arange(WINDOW + qb))

        o = lax.map(block, jnp.arange(L // qb))
        o = jnp.moveaxis(o, 0, 1).reshape(B, L, NSA_HEADS * NSA_DH)
    else:
        o = attend(q, gates, pos0 + jnp.arange(L), wrows, pos0 - n_buf + jnp.arange(n_buf + L))
    keep = min(WINDOW, pos0 + L)
    return o, new_rows, wrows[:, -keep:]


def _chunk(a, c):
    B, L = a.shape[:2]
    n = -(-L // c)
    a = jnp.pad(a, [(0, 0), (0, n * c - L)] + [(0, 0)] * (a.ndim - 2))
    a = a.reshape((B, n, c) + a.shape[2:])
    return jnp.moveaxis(a, (1, 2), (0, 3))


def _unchunk(o, L):
    n, B, H, c, d = o.shape
    return jnp.moveaxis(o, (0, 3), (1, 2)).reshape(B, n * c, H, d)[:, :L]


def _gated_delta(q, k, v, beta, g, s0):
    f32 = jnp.float32
    L = k.shape[1]
    c = min(DN_CHUNK, L)
    qs, ks, vs = (_chunk(a.astype(f32), c) for a in (q, k, v))
    bs, gs = _chunk(beta.astype(f32), c), _chunk(g.astype(f32), c)
    incl = jnp.tril(jnp.ones((c, c), bool))
    strict = jnp.tril(jnp.ones((c, c), bool), -1)
    eye = jnp.eye(c, dtype=f32)

    def step(S, xs):
        qc, kc, vc, bc, gc = xs
        G = jnp.cumsum(gc, -1)
        dec = jnp.where(incl, jnp.exp(jnp.where(incl, G[..., :, None] - G[..., None, :], 0.0)), 0.0)
        kb = kc * bc[..., None]
        A = jnp.where(strict, jnp.einsum('bhtk,bhsk->bhts', kb, kc) * dec, 0.0)
        Tm = lax.linalg.triangular_solve(eye + A, jnp.broadcast_to(eye, A.shape), left_side=True, lower=True)
        u = Tm @ (vc * bc[..., None])
        w = Tm @ (kb * jnp.exp(G)[..., None])
        v_new = u - w @ S
        attn = jnp.einsum('bhtk,bhsk->bhts', qc, kc) * dec
        o = (qc * jnp.exp(G)[..., None]) @ S + attn @ v_new
        S = S * jnp.exp(G[..., -1])[..., None, None] + jnp.einsum(
            'bhtk,bhtv->bhkv', kc * jnp.exp(G[..., -1:] - G)[..., None], v_new)
        return S, o

    S, o = lax.scan(step, s0.astype(f32), (qs, ks, vs, bs, gs))
    return _unchunk(o, L).astype(v.dtype), S.astype(s0.dtype)


def _gla_scan(q, k, v, logf, s0):
    f32 = jnp.float32
    L = k.shape[1]
    c = min(HG_CHUNK, L)
    qs, ks, vs, fs = (_chunk(a.astype(f32), c) for a in (q, k, v, logf))
    incl = jnp.tril(jnp.ones((c, c), bool))[:, :, None]

    def step(S, xs):
        qc, kc, vc, fc = xs
        G = jnp.cumsum(fc, axis=-2)
        diff = G[:, :, :, None, :] - G[:, :, None, :, :]
        dec = jnp.where(incl, jnp.exp(jnp.where(incl, diff, 0.0)), 0.0)
        attn = jnp.einsum('bhtk,bhsk,bhtsk->bhts', qc, kc, dec)
        o = jnp.einsum('bhtk,bhkv->bhtv', qc * jnp.exp(G), S) + jnp.einsum('bhts,bhsv->bhtv', attn, vc)
        S = S * jnp.exp(G[:, :, -1])[..., None] + jnp.einsum(
            'bhsk,bhsv->bhkv', kc * jnp.exp(G[:, :, -1:] - G), vc)
        return S, o

    S, o = lax.scan(step, s0.astype(f32), (qs, ks, vs, fs))
    return _unchunk(o, L).astype(v.dtype), S.astype(s0.dtype)


def _short_conv(u, buf, w):
    L = u.shape[1]
    xc = jnp.concatenate([buf, u], 1)
    y = sum(xc[:, j:j + L] * w[j] for j in range(DN_CONV))
    return jax.nn.silu(y), xc[:, -(DN_CONV - 1):]


def _deltanet_mixer(qkv_raw, a_raw, b_raw, gate_raw, conv_buf, s0, conv_w, a_log, dt_bias, norm_w):
    f32 = jnp.float32
    B, L, _ = qkv_raw.shape
    if conv_buf is None:
        conv_buf = jnp.zeros((B, DN_CONV - 1, DN_CONV_DIM), qkv_raw.dtype)
    if s0 is None:
        s0 = jnp.zeros((B, DN_HEADS, DN_DK, DN_DV), qkv_raw.dtype)
    u, new_buf = _short_conv(qkv_raw, conv_buf, conv_w)
    q, k, v = _split(u, (DN_HEADS * DN_DK, DN_HEADS * DN_DK, DN_HEADS * DN_DV))
    q = _l2norm(q.reshape(B, L, DN_HEADS, DN_DK)) * DN_DK ** -0.5
    k = _l2norm(k.reshape(B, L, DN_HEADS, DN_DK))
    v = v.reshape(B, L, DN_HEADS, DN_DV)
    beta = jax.nn.sigmoid(b_raw.astype(f32))
    g = -jnp.exp(a_log.astype(f32)) * jax.nn.softplus(a_raw.astype(f32) + dt_bias.astype(f32))
    o, s_new = _gated_delta(q, k, v, beta, g, s0)
    return _gated_rmsnorm(o, gate_raw, norm_w), new_buf, s_new


def _hgrn2_mixer(q_raw, f_raw, i_raw, gate_raw, s0, lb, norm_w):
    B, L, _ = q_raw.shape
    if s0 is None:
        s0 = jnp.zeros((B, HG_HEADS, HG_DK, HG_DV), q_raw.dtype)
    q = q_raw.reshape(B, L, HG_HEADS, HG_DK) * HG_DK ** -0.5
    f = lb + (1.0 - lb) * jax.nn.sigmoid(f_raw.astype(jnp.float32))
    logf = jnp.log(f).reshape(B, L, HG_HEADS, HG_DK)
    k = (1.0 - f).reshape(B, L, HG_HEADS, HG_DK)
    v = i_raw.reshape(B, L, HG_HEADS, HG_DV)
    o, s_new = _gla_scan(q, k, v, logf, s0)
    return _gated_rmsnorm(o, gate_raw, norm_w), s_new


def _moe(x, w_router, router_bias, wg, wu, wd):
    f32 = jnp.float32
    N, D = x.shape
    scores = jax.nn.sigmoid((x @ w_router).astype(f32))
    biased = scores + router_bias.astype(f32)
    grp = lax.top_k(biased.reshape(N, N_EXPERT_GROUPS, EXPERTS_PER_GROUP), TOP_K)[0].sum(-1)
    gsel = jnp.argmax(grp, -1)
    in_grp = (jnp.arange(N_EXPERTS) // EXPERTS_PER_GROUP)[None, :] == gsel[:, None]
    _, eidx = lax.top_k(jnp.where(in_grp, biased, -jnp.inf), TOP_K)
    gw = jnp.take_along_axis(scores, eidx, -1)
    gw = gw / gw.sum(-1, keepdims=True)
    n_assign = N * TOP_K
    e_flat = eidx.reshape(-1)
    order = jnp.argsort(e_flat)
    e_sorted = e_flat[order]
    counts = jnp.bincount(e_flat, length=N_EXPERTS)
    padded = (counts + MOE_ROW_BLOCK - 1) // MOE_ROW_BLOCK * MOE_ROW_BLOCK
    pad_end = jnp.cumsum(padded)
    dest = (pad_end - padded)[e_sorted] + jnp.arange(n_assign) - (jnp.cumsum(counts) - counts)[e_sorted]
    n_blocks = -(-n_assign // MOE_ROW_BLOCK) + N_EXPERTS
    n_rows = n_blocks * MOE_ROW_BLOCK
    tok = jnp.full((n_rows,), N, jnp.int32).at[dest].set((order // TOP_K).astype(jnp.int32))
    wt = jnp.zeros((n_rows,), f32).at[dest].set(gw.reshape(-1)[order])
    blk_exp = jnp.minimum(jnp.searchsorted(pad_end, jnp.arange(n_blocks) * MOE_ROW_BLOCK, side='right'),
                          N_EXPERTS - 1)
    x_pad = jnp.concatenate([x, jnp.zeros((1, D), x.dtype)])

    def expert_block(args):
        e, tk, w = args
        h = x_pad[tk]
        a = jax.nn.silu(h @ wg[e]) * (h @ wu[e])
        return ((a @ wd[e]) * w[:, None]).astype(x.dtype)

    out = lax.map(expert_block, (blk_exp, tok.reshape(n_blocks, MOE_ROW_BLOCK),
                                 wt.reshape(n_blocks, MOE_ROW_BLOCK)))
    y = jnp.zeros((N + 1, D), x.dtype).at[tok].add(out.reshape(n_rows, D))
    return y[:N]


def _trunk_layer(x, ple, lb, past, w_in, w_cmp1, w_cmp2, pe_cmp, dn_conv_w, dn_a_log, dn_dt_bias,
                 dn_norm_w, hg_norm_w, w_branch, w_out, ln1_g, ln1_b, ln2_g, ln2_b, w_router,
                 router_bias, w_exp_gate, w_exp_up, w_exp_down, w_ple, w_ple_gate, rel_bias):
    past_rows, win_buf, conv_buf, dn_state, hg_state = past
    B, L, D = x.shape
    (nsa_q, nsa_kv, nsa_g, dn_qkv, dn_a, dn_b, dn_g,
     hg_q, hg_f, hg_i, hg_g, merge) = _split(x @ w_in, IN_SPLITS)
    o_a, rows_new, win_new = _nsa_mixer(nsa_q, nsa_kv, nsa_g, past_rows, win_buf, w_cmp1, w_cmp2, pe_cmp, rel_bias)
    o_b, conv_new, dn_new = _deltanet_mixer(dn_qkv, dn_a, dn_b, dn_g, conv_buf, dn_state, dn_conv_w,
                                            dn_a_log, dn_dt_bias, dn_norm_w)
    o_c, hg_new = _hgrn2_mixer(hg_q, hg_f, hg_i, hg_g, hg_state, lb, hg_norm_w)
    g = jax.nn.sigmoid(merge).reshape(B, L, 3, D)
    m = (g[:, :, 0] * (o_a @ w_branch[0]) + g[:, :, 1] * (o_b @ w_branch[1])
         + g[:, :, 2] * (o_c @ w_branch[2]))
    x = _layer_norm(DEEPNORM_ALPHA * x + m @ w_out, ln1_g, ln1_b)
    ffn = _moe(x.reshape(B * L, D), w_router, router_bias, w_exp_gate, w_exp_up, w_exp_down).reshape(B, L, D)
    ple_term = (ple @ w_ple) * jax.nn.sigmoid(x @ w_ple_gate)
    x = _layer_norm(DEEPNORM_ALPHA * x + ffn + ple_term, ln2_g, ln2_b)
    return x, (rows_new, win_new, conv_new, dn_new, hg_new)


def _stack_layers(states, i):
    return jnp.stack([s[i] for s in states])


def setup_inputs(seed: int = 0) -> dict:
    key = jax.random.key(seed)
    k = jax.random.split(key, 40)
    f32 = jnp.float32

    def nrm(i, shape, scale):
        return jax.random.normal(k[i], shape, f32) * scale

    n_pages = PAST_LEN // PAGE_SIZE
    n_used = DEC_BATCH * n_pages
    n_pool = n_used + max(n_used // 4, 1)
    win_len = min(WINDOW, PAST_LEN)
    page_table = jax.random.permutation(k[7], n_pool)[:n_used].reshape(DEC_BATCH, n_pages).astype(jnp.int32)
    a_log = jnp.log(jax.random.uniform(k[14], (DEPTH, DN_HEADS), f32, 1.0, 16.0))
    dt = jnp.exp(jax.random.uniform(k[15], (DEPTH, DN_HEADS), f32, math.log(1e-3), math.log(1e-1)))
    dt_bias = dt + jnp.log(-jnp.expm1(-dt))
    return {
        'x_prompt': nrm(0, (BATCH, SEQ, D_MODEL), 1.0),
        'x_sample': nrm(1, (DEC_BATCH, DEC_SEQ, D_MODEL), 1.0),
        'cache_nsa': nrm(2, (DEPTH, n_pool, PAGE_SIZE, N_CACHE_STREAMS, NSA_KV, NSA_DH), 1.0),
        'state_win': nrm(3, (DEPTH, DEC_BATCH, win_len, 2, NSA_KV, NSA_DH), 1.0),
        'state_dn_conv': nrm(4, (DEPTH, DEC_BATCH, DN_CONV - 1, DN_CONV_DIM), 1.0),
        'state_dn': nrm(5, (DEPTH, DEC_BATCH, DN_HEADS, DN_DK, DN_DV), 0.3),
        'state_hg': nrm(6, (DEPTH, DEC_BATCH, HG_HEADS, HG_DK, HG_DV), 0.3),
        'page_table': page_table,
        'p_prompt': nrm(8, (DEPTH, BATCH, SEQ, D_PLE), 1.0),
        'p_sample': nrm(9, (DEPTH, DEC_BATCH, DEC_SEQ, D_PLE), 1.0),
        'w_in': nrm(10, (DEPTH, D_MODEL, D_IN), D_MODEL ** -0.5),
        'w_cmp1': nrm(11, (DEPTH, 2, CMP_BLOCK * NSA_DH, CMP_HID), (CMP_BLOCK * NSA_DH) ** -0.5),
        'w_cmp2': nrm(12, (DEPTH, 2, CMP_HID, NSA_DH), CMP_HID ** -0.5),
        'pe_cmp': nrm(13, (DEPTH, 2, CMP_BLOCK, NSA_DH), 0.1),
        'dn_conv_w': nrm(16, (DEPTH, DN_CONV, DN_CONV_DIM), 0.5),
        'dn_a_log': a_log,
        'dn_dt_bias': dt_bias,
        'dn_norm_w': 1.0 + nrm(17, (DEPTH, DN_DV), 0.01),
        'hg_lb_logits': nrm(18, (DEPTH, HG_HEADS * HG_DK), 0.3),
        'hg_norm_w': 1.0 + nrm(19, (DEPTH, HG_DV), 0.01),
        'w_branch': nrm(20, (DEPTH, 3, BRANCH_WIDTH, D_MODEL), BRANCH_WIDTH ** -0.5 * DEEPNORM_BETA),
        'w_out': nrm(21, (DEPTH, D_MODEL, D_MODEL), D_MODEL ** -0.5 * DEEPNORM_BETA),
        'ln1_g': 1.0 + nrm(22, (DEPTH, D_MODEL), 0.01),
        'ln1_b': nrm(23, (DEPTH, D_MODEL), 0.01),
        'ln2_g': 1.0 + nrm(24, (DEPTH, D_MODEL), 0.01),
        'ln2_b': nrm(25, (DEPTH, D_MODEL), 0.01),
        'w_router': nrm(26, (D_MODEL, N_EXPERTS), D_MODEL ** -0.5),
        'router_bias': nrm(27, (N_EXPERTS,), 0.01),
        'w_exp_gate': nrm(28, (DEPTH, N_EXPERTS, D_MODEL, D_EXPERT), D_MODEL ** -0.5),
        'w_exp_up': nrm(29, (DEPTH, N_EXPERTS, D_MODEL, D_EXPERT), D_MODEL ** -0.5),
        'w_exp_down': nrm(30, (DEPTH, N_EXPERTS, D_EXPERT, D_MODEL), D_EXPERT ** -0.5 * DEEPNORM_BETA),
        'w_ple': nrm(31, (DEPTH, D_PLE, D_MODEL), D_PLE ** -0.5 * DEEPNORM_BETA),
        'w_ple_gate': nrm(32, (DEPTH, D_MODEL, D_MODEL), D_MODEL ** -0.5),
        'rel_bias': nrm(33, (N_BUCKETS, NSA_HEADS), 0.5),
    }


def reference(x_prompt, x_sample, cache_nsa, state_win, state_dn_conv, state_dn, state_hg, page_table,
              p_prompt, p_sample, w_in, w_cmp1, w_cmp2, pe_cmp, dn_conv_w, dn_a_log, dn_dt_bias, dn_norm_w,
              hg_lb_logits, hg_norm_w, w_branch, w_out, ln1_g, ln1_b, ln2_g, ln2_b, w_router, router_bias,
              w_exp_gate, w_exp_up, w_exp_down, w_ple, w_ple_gate, rel_bias):
    lb = jnp.cumsum(jax.nn.softmax(hg_lb_logits.astype(jnp.float32), axis=0), axis=0)
    lb = lb - lb[0:1]
    n_dec = page_table.shape[0]
    y_prompt, y_sample = x_prompt, x_sample
    st_p, st_s = [], []
    for l in range(DEPTH):
        lw = (w_in[l], w_cmp1[l], w_cmp2[l], pe_cmp[l], dn_conv_w[l], dn_a_log[l], dn_dt_bias[l],
              dn_norm_w[l], hg_norm_w[l], w_branch[l], w_out[l], ln1_g[l], ln1_b[l], ln2_g[l], ln2_b[l],
              w_router, router_bias, w_exp_gate[l], w_exp_up[l], w_exp_down[l], w_ple[l], w_ple_gate[l],
              rel_bias)
        y_prompt, s = _trunk_layer(y_prompt, p_prompt[l], lb[l], (None, None, None, None, None), *lw)
        st_p.append(s)
        past_rows = cache_nsa[l][page_table].reshape(n_dec, -1, N_CACHE_STREAMS, NSA_KV, NSA_DH)
        y_sample, s = _trunk_layer(y_sample, p_sample[l], lb[l],
                                   (past_rows, state_win[l], state_dn_conv[l], state_dn[l], state_hg[l]), *lw)
        st_s.append(s)
    nsa_rows_prompt = _stack_layers(st_p, 0)
    nsa_rows_sample = _stack_layers(st_s, 0)
    win_prompt = _stack_layers(st_p, 1)
    win_sample = _stack_layers(st_s, 1)
    conv_prompt = _stack_layers(st_p, 2)
    conv_sample = _stack_layers(st_s, 2)
    dn_prompt = _stack_layers(st_p, 3)
    dn_sample = _stack_layers(st_s, 3)
    hg_prompt = _stack_layers(st_p, 4)
    hg_sample = _stack_layers(st_s, 4)
    return (y_prompt, y_sample, nsa_rows_prompt, nsa_rows_sample, win_prompt, win_sample,
            conv_prompt, conv_sample, dn_prompt, dn_sample, hg_prompt, hg_sample)
```

```python
import functools
import math

import jax
import jax.numpy as jnp
import numpy as np
from jax import lax
from jax.experimental import pallas as pl
from jax.experimental.pallas import tpu as pltpu

D_MODEL = 1024
DEPTH = 4
NSA_HEADS = 8
NSA_KV = 2
NSA_GROUP = NSA_HEADS // NSA_KV
NSA_DH = 64
CMP_STRIDE = 16
CMP_BLOCK = 2 * CMP_STRIDE
CMP_HID = 128
SEL_BLOCK = 64
N_SELECT = 16
WINDOW = 512
Q_BLOCK = 128
N_KV_STREAMS = 6
N_CACHE_STREAMS = 4
N_BUCKETS = 32
MAX_DISTANCE = 128
DN_HEADS = 4
DN_DK = 128
DN_DV = 128
DN_CONV = 4
DN_CHUNK = 64
DN_CONV_DIM = DN_HEADS * (2 * DN_DK + DN_DV)
HG_HEADS = 4
HG_DK = 128
HG_DV = 128
HG_CHUNK = 64
N_EXPERTS = 16
N_EXPERT_GROUPS = 4
EXPERTS_PER_GROUP = N_EXPERTS // N_EXPERT_GROUPS
TOP_K = 2
D_EXPERT = 512
MOE_ROW_BLOCK = 128
DEEPNORM_ALPHA = (2 * DEPTH) ** 0.25
EPS = 1e-6
NEG = -1e30
IN_SPLITS = (NSA_HEADS * NSA_DH, N_KV_STREAMS * NSA_KV * NSA_DH, 3 * NSA_HEADS,
             DN_CONV_DIM, DN_HEADS, DN_HEADS, DN_HEADS * DN_DV,
             HG_HEADS * HG_DK, HG_HEADS * HG_DK, HG_HEADS * HG_DV, HG_HEADS * HG_DV,
             3 * D_MODEL)

VMEM_LIMIT_BYTES = 48 * 1024 * 1024
MM_ROW_TILE = 1024
MM_COL_TILE = 512


def _mm_body(x_ref, w_ref, o_ref, *, precise):
    if precise:
        o_ref[...] = jnp.dot(x_ref[...], w_ref[...], preferred_element_type=jnp.float32,
                             precision=lax.Precision.HIGHEST)
    else:
        o_ref[...] = jnp.dot(x_ref[...].astype(jnp.bfloat16), w_ref[...].astype(jnp.bfloat16),
                             preferred_element_type=jnp.float32)


def _mm(x, w, precise=False):
    M, K = x.shape
    N = w.shape[1]
    tm = min(MM_ROW_TILE, M)
    tn = min(MM_COL_TILE, N)
    return pl.pallas_call(
        functools.partial(_mm_body, precise=precise),
        grid=(pl.cdiv(M, tm), pl.cdiv(N, tn)),
        in_specs=[pl.BlockSpec((tm, K), lambda i, j: (i, 0)),
                  pl.BlockSpec((K, tn), lambda i, j: (0, j))],
        out_specs=pl.BlockSpec((tm, tn), lambda i, j: (i, j)),
        out_shape=jax.ShapeDtypeStruct((M, N), jnp.float32),
        compiler_params=pltpu.CompilerParams(
            dimension_semantics=("parallel", "parallel"),
            vmem_limit_bytes=VMEM_LIMIT_BYTES),
        name="mm",
    )(x, w)


def _expert_body(blk_exp_ref, h_ref, wt_ref, wg_ref, wu_ref, wd_ref, o_ref, *, precise):
    del blk_exp_ref
    if precise:
        dot = functools.partial(jnp.dot, preferred_element_type=jnp.float32,
                                precision=lax.Precision.HIGHEST)
        h = h_ref[...]
        a = jax.nn.silu(dot(h, wg_ref[0])) * dot(h, wu_ref[0])
        o_ref[...] = dot(a, wd_ref[0]) * wt_ref[...]
    else:
        bf = jnp.bfloat16
        dot = functools.partial(jnp.dot, preferred_element_type=jnp.float32)
        h = h_ref[...].astype(bf)
        a = jax.nn.silu(dot(h, wg_ref[0].astype(bf))) * dot(h, wu_ref[0].astype(bf))
        o_ref[...] = dot(a.astype(bf), wd_ref[0].astype(bf)) * wt_ref[...]


def _expert_ffn(h, wt, blk_exp, wg, wu, wd, precise):
    n_rows, D = h.shape
    n_blocks = n_rows // MOE_ROW_BLOCK
    grid_spec = pltpu.PrefetchScalarGridSpec(
        num_scalar_prefetch=1,
        grid=(n_blocks,),
        in_specs=[pl.BlockSpec((MOE_ROW_BLOCK, D), lambda i, be: (i, 0)),
                  pl.BlockSpec((MOE_ROW_BLOCK, 1), lambda i, be: (i, 0)),
                  pl.BlockSpec((1, D, D_EXPERT), lambda i, be: (be[i], 0, 0)),
                  pl.BlockSpec((1, D, D_EXPERT), lambda i, be: (be[i], 0, 0)),
                  pl.BlockSpec((1, D_EXPERT, D), lambda i, be: (be[i], 0, 0))],
        out_specs=pl.BlockSpec((MOE_ROW_BLOCK, D), lambda i, be: (i, 0)),
    )
    return pl.pallas_call(
        functools.partial(_expert_body, precise=precise),
        grid_spec=grid_spec,
        out_shape=jax.ShapeDtypeStruct((n_rows, D), jnp.float32),
        compiler_params=pltpu.CompilerParams(
            dimension_semantics=("arbitrary",),
            vmem_limit_bytes=VMEM_LIMIT_BYTES),
        name="expert_ffn",
    )(blk_exp, h, wt.reshape(n_rows, 1), wg, wu, wd)


def _split(z, sizes):
    return jnp.split(z, np.cumsum(sizes)[:-1].tolist(), axis=-1)


def _layer_norm(x, g, b):
    mu = x.mean(-1, keepdims=True)
    var = jnp.square(x - mu).mean(-1, keepdims=True)
    return (x - mu) * lax.rsqrt(var + EPS) * g + b


def _masked_softmax(logits, mask):
    z = jnp.where(mask, logits.astype(jnp.float32), NEG)
    e = jnp.exp(z - z.max(-1, keepdims=True)) * mask
    s = e.sum(-1, keepdims=True)
    return e / jnp.where(s > 0, s, 1.0)


def _l2norm(a):
    return a * lax.rsqrt(jnp.sum(a * a, -1, keepdims=True) + EPS)


def _gated_rmsnorm(o, gate, w):
    B, L, H, dv = o.shape
    o = o * lax.rsqrt(jnp.mean(o * o, -1, keepdims=True) + EPS) * w
    return o.reshape(B, L, H * dv) * jax.nn.silu(gate)


def _t5_bucket(dist):
    n = jnp.maximum(dist, 0)
    exact = N_BUCKETS // 2
    rel = jnp.log(jnp.maximum(n, 1).astype(jnp.float32) / exact) / math.log(MAX_DISTANCE / exact)
    large = jnp.minimum(exact + (rel * (N_BUCKETS - exact)).astype(jnp.int32), N_BUCKETS - 1)
    return jnp.where(n < exact, n, large)


def _head_bias(dist, rel_bias):
    Q, K = dist.shape
    bias = rel_bias[_t5_bucket(dist)]
    return bias.reshape(Q, K, NSA_KV, NSA_GROUP).transpose(2, 3, 0, 1).astype(jnp.float32)


def _compress(rows, w1, w2, pe):
    B, T, G, dh = rows.shape
    n_chunk = T // CMP_STRIDE
    r = rows[:, :n_chunk * CMP_STRIDE].reshape(B, n_chunk, CMP_STRIDE, G, dh)
    w1h = w1.reshape(2, CMP_STRIDE, dh, CMP_HID)
    lo = jnp.einsum('bcsgd,sdh->bcgh', r, w1h[0])
    hi = jnp.einsum('bcsgd,sdh->bcgh', r, w1h[1])
    h = lo[:, :-1] + hi[:, 1:] + pe.reshape(-1) @ w1
    return jax.nn.silu(h) @ w2


def _importance_map(n_cmp, n_sb):
    c0 = jnp.arange(n_cmp) * CMP_STRIDE
    c1 = c0 + CMP_BLOCK - 1
    b0 = jnp.arange(n_sb) * SEL_BLOCK
    b1 = b0 + SEL_BLOCK - 1
    return ((c0[:, None] <= b1[None, :]) & (c1[:, None] >= b0[None, :])).astype(jnp.float32)


def _nsa_core(q, gates, t, kc, vc, cmp_end, imp_map, ks_blk, vs_blk, kw, vw, w_pos, rel_bias):
    f32 = jnp.float32
    B, Q = q.shape[:2]
    lg = jnp.einsum('bqgjd,bcgd->bgjqc', q, kc).astype(f32) + _head_bias(t[:, None] - cmp_end[None, :], rel_bias)
    p_cmp = _masked_softmax(lg, cmp_end[None, :] <= t[:, None])
    o_cmp = jnp.einsum('bgjqc,bcgd->bqgjd', p_cmp, vc)
    n_sb = ks_blk.shape[2]
    n_top = min(N_SELECT, n_sb)
    imp = jnp.einsum('bgjqc,cn->bgqn', p_cmp, imp_map)
    blk = jnp.arange(n_sb)[None, :]
    cur = (t // SEL_BLOCK)[:, None]
    forced = (blk == 0) | (blk == cur) | (blk == cur - 1)
    score = jnp.where(blk * SEL_BLOCK <= t[:, None], jnp.where(forced, jnp.inf, imp), -jnp.inf)
    top_s, top_i = lax.top_k(score, n_top)
    b_ix = jnp.arange(B)[:, None, None, None]
    g_ix = jnp.arange(NSA_KV)[None, :, None, None]
    n_keys = n_top * SEL_BLOCK
    ks = ks_blk[b_ix, g_ix, top_i].reshape(B, NSA_KV, Q, n_keys, NSA_DH)
    vs = vs_blk[b_ix, g_ix, top_i].reshape(B, NSA_KV, Q, n_keys, NSA_DH)
    s_pos = (top_i[..., None] * SEL_BLOCK + jnp.arange(SEL_BLOCK)).reshape(B, NSA_KV, Q, n_keys)
    s_ok = jnp.repeat(top_s > -jnp.inf, SEL_BLOCK, axis=-1) & (s_pos <= t[:, None])
    table = rel_bias.reshape(N_BUCKETS, NSA_KV, NSA_GROUP).transpose(1, 2, 0)
    bucket = _t5_bucket(t[:, None] - s_pos)
    s_bias = table[jnp.arange(NSA_KV)[None, :, None, None, None],
                   jnp.arange(NSA_GROUP)[None, None, :, None, None],
                   bucket[:, :, None]].astype(f32)
    lg = jnp.einsum('bqgjd,bgqmd->bgjqm', q, ks).astype(f32) + s_bias
    p_sel = _masked_softmax(lg, s_ok[:, :, None])
    o_sel = jnp.einsum('bgjqm,bgqmd->bqgjd', p_sel, vs)
    w_ok = (w_pos[None, :] <= t[:, None]) & (w_pos[None, :] > t[:, None] - WINDOW) & (w_pos[None, :] >= 0)
    lg = jnp.einsum('bqgjd,bkgd->bgjqk', q, kw).astype(f32) + _head_bias(t[:, None] - w_pos[None, :], rel_bias)
    p_win = _masked_softmax(lg, w_ok)
    o_win = jnp.einsum('bgjqk,bkgd->bqgjd', p_win, vw)
    o = gates[..., 0:1] * o_cmp + gates[..., 1:2] * o_sel + gates[..., 2:3] * o_win
    return o.reshape(B, Q, NSA_HEADS * NSA_DH)


def _nsa_mixer(q_raw, kv_raw, gate_raw, past_rows, win_buf, w_cmp1, w_cmp2, pe_cmp, rel_bias):
    B, L, _ = q_raw.shape
    q = (q_raw * NSA_DH ** -0.5).reshape(B, L, NSA_KV, NSA_GROUP, NSA_DH)
    gates = jax.nn.sigmoid(gate_raw).reshape(B, L, NSA_KV, NSA_GROUP, 3)
    kv = kv_raw.reshape(B, L, N_KV_STREAMS, NSA_KV, NSA_DH)
    new_rows, new_win = kv[:, :, :N_CACHE_STREAMS], kv[:, :, N_CACHE_STREAMS:]
    pos0 = 0 if past_rows is None else past_rows.shape[1]
    rows = new_rows if past_rows is None else jnp.concatenate([past_rows, new_rows], 1)
    T = rows.shape[1]
    kc = _compress(rows[:, :, 0], w_cmp1[0], w_cmp2[0], pe_cmp[0])
    vc = _compress(rows[:, :, 1], w_cmp1[1], w_cmp2[1], pe_cmp[1])
    n_cmp = kc.shape[1]
    cmp_end = jnp.arange(n_cmp) * CMP_STRIDE + CMP_BLOCK - 1
    n_sb = -(-T // SEL_BLOCK)
    sel = jnp.pad(rows[:, :, 2:4], ((0, 0), (0, n_sb * SEL_BLOCK - T), (0, 0), (0, 0), (0, 0)))
    sel = sel.reshape(B, n_sb, SEL_BLOCK, 2, NSA_KV, NSA_DH).transpose(3, 0, 4, 1, 2, 5)
    imp_map = _importance_map(n_cmp, n_sb)
    wrows = new_win if win_buf is None else jnp.concatenate([win_buf, new_win], 1)
    n_buf = wrows.shape[1] - L

    def attend(q_b, g_b, t, kw, w_pos):
        return _nsa_core(q_b, g_b, t, kc, vc, cmp_end, imp_map, sel[0], sel[1],
                         kw[:, :, 0], kw[:, :, 1], w_pos, rel_bias)

    if past_rows is None:
        qb = min(Q_BLOCK, L)
        wpad = jnp.pad(wrows, ((0, 0), (WINDOW, 0), (0, 0), (0, 0), (0, 0)))

        def block(i):
            s0 = i * qb
            return attend(lax.dynamic_slice_in_dim(q, s0, qb, 1),
                          lax.dynamic_slice_in_dim(gates, s0, qb, 1),
                          s0 + jnp.arange(qb),
                          lax.dynamic_slice_in_dim(wpad, s0, WINDOW + qb, 1),
                          s0 - WINDOW + jnp.arange(WINDOW + qb))

        o = lax.map(block, jnp.arange(L // qb))
        o = jnp.moveaxis(o, 0, 1).reshape(B, L, NSA_HEADS * NSA_DH)
    else:
        o = attend(q, gates, pos0 + jnp.arange(L), wrows, pos0 - n_buf + jnp.arange(n_buf + L))
    keep = min(WINDOW, pos0 + L)
    return o, new_rows, wrows[:, -keep:]


def _chunk(a, c):
    B, L = a.shape[:2]
    n = -(-L // c)
    a = jnp.pad(a, [(0, 0), (0, n * c - L)] + [(0, 0)] * (a.ndim - 2))
    a = a.reshape((B, n, c) + a.shape[2:])
    return jnp.moveaxis(a, (1, 2), (0, 3))


def _unchunk(o, L):
    n, B, H, c, d = o.shape
    return jnp.moveaxis(o, (0, 3), (1, 2)).reshape(B, n * c, H, d)[:, :L]


def _gated_delta(q, k, v, beta, g, s0):
    f32 = jnp.float32
    L = k.shape[1]
    c = min(DN_CHUNK, L)
    qs, ks, vs = (_chunk(a, c) for a in (q, k, v))
    bs, gs = _chunk(beta, c), _chunk(g, c)
    incl = jnp.tril(jnp.ones((c, c), bool))
    strict = jnp.tril(jnp.ones((c, c), bool), -1)
    eye = jnp.eye(c, dtype=f32)

    def step(S, xs):
        qc, kc, vc, bc, gc = xs
        G = jnp.cumsum(gc, -1)
        dec = jnp.where(incl, jnp.exp(jnp.where(incl, G[..., :, None] - G[..., None, :], 0.0)), 0.0)
        kb = kc * bc[..., None]
        A = jnp.where(strict, jnp.einsum('bhtk,bhsk->bhts', kb, kc) * dec, 0.0)
        Tm = lax.linalg.triangular_solve(eye + A, jnp.broadcast_to(eye, A.shape), left_side=True, lower=True)
        u = Tm @ (vc * bc[..., None])
        w = Tm @ (kb * jnp.exp(G)[..., None])
        v_new = u - w @ S
        attn = jnp.einsum('bhtk,bhsk->bhts', qc, kc) * dec
        o = (qc * jnp.exp(G)[..., None]) @ S + attn @ v_new
        S = S * jnp.exp(G[..., -1])[..., None, None] + jnp.einsum(
            'bhtk,bhtv->bhkv', kc * jnp.exp(G[..., -1:] - G)[..., None], v_new)
        return S, o

    S, o = lax.scan(step, s0, (qs, ks, vs, bs, gs))
    return _unchunk(o, L), S


def _gla_scan(q, k, v, logf, s0):
    L = k.shape[1]
    c = min(HG_CHUNK, L)
    qs, ks, vs, fs = (_chunk(a, c) for a in (q, k, v, logf))
    incl = jnp.tril(jnp.ones((c, c), bool))[:, :, None]

    def step(S, xs):
        qc, kc, vc, fc = xs
        G = jnp.cumsum(fc, axis=-2)
        diff = G[:, :, :, None, :] - G[:, :, None, :, :]
        dec = jnp.where(incl, jnp.exp(jnp.where(incl, diff, 0.0)), 0.0)
        attn = jnp.einsum('bhtk,bhsk,bhtsk->bhts', qc, kc, dec)
        o = jnp.einsum('bhtk,bhkv->bhtv', qc * jnp.exp(G), S) + jnp.einsum('bhts,bhsv->bhtv', attn, vc)
        S = S * jnp.exp(G[:, :, -1])[..., None] + jnp.einsum(
            'bhsk,bhsv->bhkv', kc * jnp.exp(G[:, :, -1:] - G), vc)
        return S, o

    S, o = lax.scan(step, s0, (qs, ks, vs, fs))
    return _unchunk(o, L), S


def _short_conv(u, buf, w):
    L = u.shape[1]
    xc = jnp.concatenate([buf, u], 1)
    y = sum(xc[:, j:j + L] * w[j] for j in range(DN_CONV))
    return jax.nn.silu(y), xc[:, -(DN_CONV - 1):]


def _deltanet_mixer(qkv_raw, a_raw, b_raw, gate_raw, conv_buf, s0, conv_w, a_log, dt_bias, norm_w):
    B, L, _ = qkv_raw.shape
    if conv_buf is None:
        conv_buf = jnp.zeros((B, DN_CONV - 1, DN_CONV_DIM), qkv_raw.dtype)
    if s0 is None:
        s0 = jnp.zeros((B, DN_HEADS, DN_DK, DN_DV), qkv_raw.dtype)
    u, new_buf = _short_conv(qkv_raw, conv_buf, conv_w)
    q, k, v = _split(u, (DN_HEADS * DN_DK, DN_HEADS * DN_DK, DN_HEADS * DN_DV))
    q = _l2norm(q.reshape(B, L, DN_HEADS, DN_DK)) * DN_DK ** -0.5
    k = _l2norm(k.reshape(B, L, DN_HEADS, DN_DK))
    v = v.reshape(B, L, DN_HEADS, DN_DV)
    beta = jax.nn.sigmoid(b_raw)
    g = -jnp.exp(a_log) * jax.nn.softplus(a_raw + dt_bias)
    o, s_new = _gated_delta(q, k, v, beta, g, s0)
    return _gated_rmsnorm(o, gate_raw, norm_w), new_buf, s_new


def _hgrn2_mixer(q_raw, f_raw, i_raw, gate_raw, s0, lb, norm_w):
    B, L, _ = q_raw.shape
    if s0 is None:
        s0 = jnp.zeros((B, HG_HEADS, HG_DK, HG_DV), q_raw.dtype)
    q = q_raw.reshape(B, L, HG_HEADS, HG_DK) * HG_DK ** -0.5
    f = lb + (1.0 - lb) * jax.nn.sigmoid(f_raw)
    logf = jnp.log(f).reshape(B, L, HG_HEADS, HG_DK)
    k = (1.0 - f).reshape(B, L, HG_HEADS, HG_DK)
    v = i_raw.reshape(B, L, HG_HEADS, HG_DV)
    o, s_new = _gla_scan(q, k, v, logf, s0)
    return _gated_rmsnorm(o, gate_raw, norm_w), s_new


def _moe(x, w_router, router_bias, wg, wu, wd, precise):
    f32 = jnp.float32
    N, D = x.shape
    scores = jax.nn.sigmoid(_mm(x, w_router, precise))
    biased = scores + router_bias
    grp = lax.top_k(biased.reshape(N, N_EXPERT_GROUPS, EXPERTS_PER_GROUP), TOP_K)[0].sum(-1)
    gsel = jnp.argmax(grp, -1)
    in_grp = (jnp.arange(N_EXPERTS) // EXPERTS_PER_GROUP)[None, :] == gsel[:, None]
    _, eidx = lax.top_k(jnp.where(in_grp, biased, -jnp.inf), TOP_K)
    gw = jnp.take_along_axis(scores, eidx, -1)
    gw = gw / gw.sum(-1, keepdims=True)
    n_assign = N * TOP_K
    e_flat = eidx.reshape(-1)
    order = jnp.argsort(e_flat)
    e_sorted = e_flat[order]
    counts = jnp.bincount(e_flat, length=N_EXPERTS)
    padded = (counts + MOE_ROW_BLOCK - 1) // MOE_ROW_BLOCK * MOE_ROW_BLOCK
    pad_end = jnp.cumsum(padded)
    dest = (pad_end - padded)[e_sorted] + jnp.arange(n_assign) - (jnp.cumsum(counts) - counts)[e_sorted]
    n_blocks = -(-n_assign // MOE_ROW_BLOCK) + N_EXPERTS
    n_rows = n_blocks * MOE_ROW_BLOCK
    tok = jnp.full((n_rows,), N, jnp.int32).at[dest].set((order // TOP_K).astype(jnp.int32))
    wt = jnp.zeros((n_rows,), f32).at[dest].set(gw.reshape(-1)[order])
    blk_exp = jnp.minimum(jnp.searchsorted(pad_end, jnp.arange(n_blocks) * MOE_ROW_BLOCK, side='right'),
                          N_EXPERTS - 1).astype(jnp.int32)
    x_pad = jnp.concatenate([x, jnp.zeros((1, D), x.dtype)])
    out = _expert_ffn(x_pad[tok], wt, blk_exp, wg, wu, wd, precise)
    slot = jnp.zeros((n_assign,), jnp.int32).at[order].set(dest.astype(jnp.int32)).reshape(N, TOP_K)
    return out[slot[:, 0]] + out[slot[:, 1]]


def _trunk_layer(x, ple, lb, past, precise, w_in, w_cmp1, w_cmp2, pe_cmp, dn_conv_w, dn_a_log, dn_dt_bias,
                 dn_norm_w, hg_norm_w, w_branch, w_out, ln1_g, ln1_b, ln2_g, ln2_b, w_router,
                 router_bias, w_exp_gate, w_exp_up, w_exp_down, w_ple, w_ple_gate, rel_bias):
    past_rows, win_buf, conv_buf, dn_state, hg_state = past
    B, L, D = x.shape
    x2 = x.reshape(B * L, D)
    mm = functools.partial(_mm, precise=precise)
    z = mm(x2, w_in).reshape(B, L, -1)
    (nsa_q, nsa_kv, nsa_g, dn_qkv, dn_a, dn_b, dn_g,
     hg_q, hg_f, hg_i, hg_g, merge) = _split(z, IN_SPLITS)
    o_a, rows_new, win_new = _nsa_mixer(nsa_q, nsa_kv, nsa_g, past_rows, win_buf, w_cmp1, w_cmp2, pe_cmp, rel_bias)
    o_b, conv_new, dn_new = _deltanet_mixer(dn_qkv, dn_a, dn_b, dn_g, conv_buf, dn_state, dn_conv_w,
                                            dn_a_log, dn_dt_bias, dn_norm_w)
    o_c, hg_new = _hgrn2_mixer(hg_q, hg_f, hg_i, hg_g, hg_state, lb, hg_norm_w)
    g = jax.nn.sigmoid(merge).reshape(B * L, 3, D)
    m = (g[:, 0] * mm(o_a.reshape(B * L, -1), w_branch[0]) + g[:, 1] * mm(o_b.reshape(B * L, -1), w_branch[1])
         + g[:, 2] * mm(o_c.reshape(B * L, -1), w_branch[2]))
    x2 = _layer_norm(DEEPNORM_ALPHA * x2 + mm(m, w_out), ln1_g, ln1_b)
    ffn = _moe(x2, w_router, router_bias, w_exp_gate, w_exp_up, w_exp_down, precise)
    ple_term = mm(ple.reshape(B * L, -1), w_ple) * jax.nn.sigmoid(mm(x2, w_ple_gate))
    x2 = _layer_norm(DEEPNORM_ALPHA * x2 + ffn + ple_term, ln2_g, ln2_b)
    return x2.reshape(B, L, D), (rows_new, win_new, conv_new, dn_new, hg_new)


def kernel(x_prompt, x_sample, cache_nsa, state_win, state_dn_conv, state_dn, state_hg, page_table,
           p_prompt, p_sample, w_in, w_cmp1, w_cmp2, pe_cmp, dn_conv_w, dn_a_log, dn_dt_bias, dn_norm_w,
           hg_lb_logits, hg_norm_w, w_branch, w_out, ln1_g, ln1_b, ln2_g, ln2_b, w_router, router_bias,
           w_exp_gate, w_exp_up, w_exp_down, w_ple, w_ple_gate, rel_bias):
    lb = jnp.cumsum(jax.nn.softmax(hg_lb_logits, axis=0), axis=0)
    lb = lb - lb[0:1]
    n_dec = page_table.shape[0]
    y_prompt, y_sample = x_prompt, x_sample
    st_p, st_s = [], []
    for l in range(DEPTH):
        lw = (w_in[l], w_cmp1[l], w_cmp2[l], pe_cmp[l], dn_conv_w[l], dn_a_log[l], dn_dt_bias[l],
              dn_norm_w[l], hg_norm_w[l], w_branch[l], w_out[l], ln1_g[l], ln1_b[l], ln2_g[l], ln2_b[l],
              w_router, router_bias, w_exp_gate[l], w_exp_up[l], w_exp_down[l], w_ple[l], w_ple_gate[l],
              rel_bias)
        y_prompt, s = _trunk_layer(y_prompt, p_prompt[l], lb[l], (None, None, None, None, None), False, *lw)
        st_p.append(s)
        pages = jnp.take(cache_nsa[l].reshape(cache_nsa.shape[1], -1), page_table.reshape(-1), axis=0)
        past_rows = pages.reshape(n_dec, -1, N_CACHE_STREAMS, NSA_KV, NSA_DH)
        y_sample, s = _trunk_layer(y_sample, p_sample[l], lb[l],
                                   (past_rows, state_win[l], state_dn_conv[l], state_dn[l], state_hg[l]),
                                   True, *lw)
        st_s.append(s)
    outs = [y_prompt, y_sample]
    for i in range(5):
        outs.append(jnp.stack([s[i] for s in st_p]))
        outs.append(jnp.stack([s[i] for s in st_s]))
    return tuple(outs)
```

```python
import functools
import math

import jax
import jax.numpy as jnp
import numpy as np
from jax import lax
from jax.experimental import pallas as pl
from jax.experimental.pallas import tpu as pltpu

D_MODEL = 1024
DEPTH = 4
NSA_HEADS = 8
NSA_KV = 2
NSA_GROUP = NSA_HEADS // NSA_KV
NSA_DH = 64
CMP_STRIDE = 16
CMP_BLOCK = 2 * CMP_STRIDE
CMP_HID = 128
SEL_BLOCK = 64
N_SELECT = 16
WINDOW = 512
Q_BLOCK = 128
N_KV_STREAMS = 6
N_CACHE_STREAMS = 4
N_BUCKETS = 32
MAX_DISTANCE = 128
DN_HEADS = 4
DN_DK = 128
DN_DV = 128
DN_CONV = 4
DN_CHUNK = 64
DN_CONV_DIM = DN_HEADS * (2 * DN_DK + DN_DV)
HG_HEADS = 4
HG_DK = 128
HG_DV = 128
HG_CHUNK = 64
N_EXPERTS = 16
N_EXPERT_GROUPS = 4
EXPERTS_PER_GROUP = N_EXPERTS // N_EXPERT_GROUPS
TOP_K = 2
D_EXPERT = 512
MOE_ROW_BLOCK = 128
DEEPNORM_ALPHA = (2 * DEPTH) ** 0.25
EPS = 1e-6
NEG = -1e30
IN_SPLITS = (NSA_HEADS * NSA_DH, N_KV_STREAMS * NSA_KV * NSA_DH, 3 * NSA_HEADS,
             DN_CONV_DIM, DN_HEADS, DN_HEADS, DN_HEADS * DN_DV,
             HG_HEADS * HG_DK, HG_HEADS * HG_DK, HG_HEADS * HG_DV, HG_HEADS * HG_DV,
             3 * D_MODEL)

VMEM_LIMIT_BYTES = 48 * 1024 * 1024
MM_ROW_TILE = 1024
MM_COL_TILE = 512


def _mm_body(x_ref, w_ref, o_ref, *, precise):
    if precise:
        o_ref[...] = jnp.dot(x_ref[...], w_ref[...], preferred_element_type=jnp.float32,
                             precision=lax.Precision.HIGHEST)
    else:
        o_ref[...] = jnp.dot(x_ref[...].astype(jnp.bfloat16), w_ref[...].astype(jnp.bfloat16),
                             preferred_element_type=jnp.float32)


def _mm(x, w, precise=False):
    M, K = x.shape
    N = w.shape[1]
    tm = min(MM_ROW_TILE, M)
    tn = min(MM_COL_TILE, N)
    return pl.pallas_call(
        functools.partial(_mm_body, precise=precise),
        grid=(pl.cdiv(M, tm), pl.cdiv(N, tn)),
        in_specs=[pl.BlockSpec((tm, K), lambda i, j: (i, 0)),
                  pl.BlockSpec((K, tn), lambda i, j: (0, j))],
        out_specs=pl.BlockSpec((tm, tn), lambda i, j: (i, j)),
        out_shape=jax.ShapeDtypeStruct((M, N), jnp.float32),
        compiler_params=pltpu.CompilerParams(
            dimension_semantics=("parallel", "parallel"),
            vmem_limit_bytes=VMEM_LIMIT_BYTES),
        name="mm",
    )(x, w)


def _expert_body(blk_exp_ref, h_ref, wt_ref, wg_ref, wu_ref, wd_ref, o_ref, *, precise):
    del blk_exp_ref
    if precise:
        dot = functools.partial(jnp.dot, preferred_element_type=jnp.float32,
                                precision=lax.Precision.HIGHEST)
        h = h_ref[...]
        a = jax.nn.silu(dot(h, wg_ref[0])) * dot(h, wu_ref[0])
        o_ref[...] = dot(a, wd_ref[0]) * wt_ref[...]
    else:
        bf = jnp.bfloat16
        dot = functools.partial(jnp.dot, preferred_element_type=jnp.float32)
        h = h_ref[...].astype(bf)
        a = jax.nn.silu(dot(h, wg_ref[0].astype(bf))) * dot(h, wu_ref[0].astype(bf))
        o_ref[...] = dot(a.astype(bf), wd_ref[0].astype(bf)) * wt_ref[...]


def _expert_ffn(h, wt, blk_exp, wg, wu, wd, precise):
    n_rows, D = h.shape
    n_blocks = n_rows // MOE_ROW_BLOCK
    grid_spec = pltpu.PrefetchScalarGridSpec(
        num_scalar_prefetch=1,
        grid=(n_blocks,),
        in_specs=[pl.BlockSpec((MOE_ROW_BLOCK, D), lambda i, be: (i, 0)),
                  pl.BlockSpec((MOE_ROW_BLOCK, 1), lambda i, be: (i, 0)),
                  pl.BlockSpec((1, D, D_EXPERT), lambda i, be: (be[i], 0, 0)),
                  pl.BlockSpec((1, D, D_EXPERT), lambda i, be: (be[i], 0, 0)),
                  pl.BlockSpec((1, D_EXPERT, D), lambda i, be: (be[i], 0, 0))],
        out_specs=pl.BlockSpec((MOE_ROW_BLOCK, D), lambda i, be: (i, 0)),
    )
    return pl.pallas_call(
        functools.partial(_expert_body, precise=precise),
        grid_spec=grid_spec,
        out_shape=jax.ShapeDtypeStruct((n_rows, D), jnp.float32),
        compiler_params=pltpu.CompilerParams(
            dimension_semantics=("arbitrary",),
            vmem_limit_bytes=VMEM_LIMIT_BYTES),
        name="expert_ffn",
    )(blk_exp, h, wt.reshape(n_rows, 1), wg, wu, wd)


NSA_ROWS = NSA_GROUP * Q_BLOCK
NEAR_CMP = 32
NEAR_CMP_BACK = 16
CMP_PER_QBLOCK = Q_BLOCK // CMP_STRIDE
SEL_PER_TILE = Q_BLOCK // SEL_BLOCK
WIN_TILES = WINDOW // Q_BLOCK
MXU_DTYPE = jnp.bfloat16


def _nsa_prompt_body(q_ref, gt_ref, kct_ref, kc_ref, vc_ref, kst_ref, vs_ref, kwt_ref, vw_ref,
                     band_ref, near_ref, far_ref, imap_ref, o_ref, m_scr, l_scr, acc_scr):
    f32 = jnp.float32
    R, QB = NSA_ROWS, Q_BLOCK
    i = pl.program_id(2)
    n_cmp_pad = kct_ref.shape[-1]
    n_sb = imap_ref.shape[-1]
    q = q_ref[0, 0, 0]
    far_bias = far_ref[0]
    r_row = lax.broadcasted_iota(jnp.int32, (R, 1), 0) & (QB - 1)
    t_row = i * QB + r_row

    def mxu(a, b):
        return jnp.dot(a.astype(MXU_DTYPE), b.astype(MXU_DTYPE), preferred_element_type=f32)

    def mxu_nt(a, b):
        return lax.dot_general(a.astype(MXU_DTYPE), b.astype(MXU_DTYPE), (((1,), (1,)), ((), ())),
                               preferred_element_type=f32)

    c0 = CMP_PER_QBLOCK * i - NEAR_CMP_BACK
    lg_far = mxu(q, kct_ref[0, 0]) + far_bias
    mk_far = lax.broadcasted_iota(jnp.int32, (R, n_cmp_pad), 1) < c0
    near_start = pl.multiple_of(CMP_PER_QBLOCK * i, 8)
    k_near = kc_ref[0, 0, pl.ds(near_start, NEAR_CMP), :]
    v_near = vc_ref[0, 0, pl.ds(near_start, NEAR_CMP), :]
    lg_near = mxu_nt(q, k_near) + near_ref[0]
    c_near = c0 + lax.broadcasted_iota(jnp.int32, (R, NEAR_CMP), 1)
    mk_near = (c_near >= 0) & (c_near * CMP_STRIDE + (CMP_BLOCK - 1) <= t_row)
    z_far = jnp.where(mk_far, lg_far, NEG)
    z_near = jnp.where(mk_near, lg_near, NEG)
    mx = jnp.maximum(z_far.max(-1, keepdims=True), z_near.max(-1, keepdims=True))
    e_far = jnp.where(mk_far, jnp.exp(z_far - mx), 0.0)
    e_near = jnp.where(mk_near, jnp.exp(z_near - mx), 0.0)
    s = e_far.sum(-1, keepdims=True) + e_near.sum(-1, keepdims=True)
    s = jnp.where(s > 0, s, 1.0)
    p_far = e_far / s
    p_near = e_near / s
    o_cmp = (mxu(p_far, vc_ref[0, 0, NEAR_CMP_BACK:NEAR_CMP_BACK + n_cmp_pad, :]) + mxu(p_near, v_near))

    cn = c0 + lax.broadcasted_iota(jnp.int32, (NEAR_CMP, n_sb), 0)
    nn = lax.broadcasted_iota(jnp.int32, (NEAR_CMP, n_sb), 1)
    blocks_per_sel = SEL_BLOCK // CMP_STRIDE
    imap_near = ((cn >= blocks_per_sel * nn - 1) & (cn <= blocks_per_sel * nn + blocks_per_sel - 1)).astype(f32)
    imp_rows = mxu(p_far, imap_ref[...]) + mxu(p_near, imap_near)
    imp = imp_rows[0:QB]
    for j in range(1, NSA_GROUP):
        imp = imp + imp_rows[j * QB:(j + 1) * QB]
    n_io = lax.broadcasted_iota(jnp.int32, (QB, n_sb), 1)
    n_f = n_io.astype(f32)
    r_q = lax.broadcasted_iota(jnp.int32, (QB, n_sb), 0)
    t_q = i * QB + r_q
    cur = SEL_PER_TILE * i + (r_q >= SEL_BLOCK).astype(jnp.int32)
    forced = (n_io == 0) | (n_io == cur) | (n_io == cur - 1)
    score = jnp.where(n_io * SEL_BLOCK <= t_q, jnp.where(forced, jnp.inf, imp), -jnp.inf)

    def pick_one(_, carry):
        score, sel = carry
        top = score.max(-1, keepdims=True)
        first = jnp.where(score == top, n_f, float(n_sb)).min(-1, keepdims=True)
        hit = n_f == first
        sel = jnp.where(hit & (top > -jnp.inf), 1.0, sel)
        return jnp.where(hit, -jnp.inf, score), sel

    _, sel = lax.fori_loop(0, min(N_SELECT, n_sb), pick_one, (score, jnp.zeros((QB, n_sb), f32)))

    r_t = lax.broadcasted_iota(jnp.int32, (R, QB), 0) & (QB - 1)
    c_t = lax.broadcasted_iota(jnp.int32, (R, QB), 1)

    def reset():
        m_scr[...] = jnp.full((R, 1), NEG, f32)
        l_scr[...] = jnp.zeros((R, 1), f32)
        acc_scr[...] = jnp.zeros_like(acc_scr)

    def update(lg, mk, v):
        z = jnp.where(mk, lg, NEG) if mk is not None else lg
        m_old = m_scr[...]
        m_new = jnp.maximum(m_old, z.max(-1, keepdims=True))
        p = jnp.exp(z - m_new)
        if mk is not None:
            p = jnp.where(mk, p, 0.0)
        alpha = jnp.exp(m_old - m_new)
        l_scr[...] = alpha * l_scr[...] + p.sum(-1, keepdims=True)
        acc_scr[...] = alpha * acc_scr[...] + mxu(p, v)
        m_scr[...] = m_new

    def result():
        l = l_scr[...]
        return acc_scr[...] / jnp.where(l > 0, l, 1.0)

    def tile(kt_ref, v_ref, kt):
        start = pl.multiple_of(kt * QB, QB)
        return mxu(q, kt_ref[0, 0, :, pl.ds(start, QB)]), v_ref[0, 0, pl.ds(start, QB), :]

    def sel_mask(kt):
        blk = SEL_PER_TILE * kt + (lax.broadcasted_iota(jnp.int32, (n_sb, QB), 1) >= SEL_BLOCK).astype(jnp.int32)
        expand = (lax.broadcasted_iota(jnp.int32, (n_sb, QB), 0) == blk).astype(f32)
        mq = mxu(sel, expand)
        return jnp.concatenate([mq] * NSA_GROUP, axis=0) > 0.5

    reset()

    def sel_far(kt, carry):
        lg, v = tile(kst_ref, vs_ref, kt)
        update(lg + far_bias, sel_mask(kt), v)
        return carry

    lax.fori_loop(0, jnp.maximum(i - 1, 0), sel_far, 0)

    @pl.when(i >= 1)
    def _():
        lg, v = tile(kst_ref, vs_ref, i - 1)
        update(lg + band_ref[0, 1], sel_mask(i - 1), v)

    lg, v = tile(kst_ref, vs_ref, i)
    update(lg + band_ref[0, 0], sel_mask(i) & (c_t <= r_t), v)
    o_sel = result()

    reset()

    @pl.when(i >= WIN_TILES)
    def _():
        lg, v = tile(kwt_ref, vw_ref, i - WIN_TILES)
        update(lg + far_bias, c_t > r_t, v)

    for back in range(WIN_TILES - 1, 1, -1):
        @pl.when(i >= back)
        def _(back=back):
            lg, v = tile(kwt_ref, vw_ref, i - back)
            update(lg + far_bias, None, v)

    @pl.when(i >= 1)
    def _():
        lg, v = tile(kwt_ref, vw_ref, i - 1)
        update(lg + band_ref[0, 1], None, v)

    lg, v = tile(kwt_ref, vw_ref, i)
    update(lg + band_ref[0, 0], c_t <= r_t, v)
    o_win = result()

    gt = gt_ref[0, 0, 0]
    o_ref[0, 0, 0] = gt[:, 0:1] * o_cmp + gt[:, 1:2] * o_sel + gt[:, 2:3] * o_win


def _nsa_prompt_attention(q, gates, kc, vc, ksel, vsel, kwin, vwin, rel_bias):
    f32 = jnp.float32
    B, L = q.shape[:2]
    G, J, dh, QB, R = NSA_KV, NSA_GROUP, NSA_DH, Q_BLOCK, NSA_ROWS
    assert L % QB == 0 and WINDOW % QB == 0 and QB % SEL_BLOCK == 0
    nq = L // QB
    n_cmp = kc.shape[1]
    n_cmp_pad = L // CMP_STRIDE
    n_sb = L // SEL_BLOCK
    assert n_cmp == n_cmp_pad - 1

    def rows(a):
        x = a.shape[-1]
        return a.reshape(B, nq, QB, G, J, x).transpose(0, 3, 1, 4, 2, 5).reshape(B, G, nq, R, x)

    q_r = rows(q).astype(MXU_DTYPE)
    gt_r = rows(gates)
    kc_p = jnp.pad(kc, ((0, 0), (NEAR_CMP_BACK, n_cmp_pad - n_cmp + NEAR_CMP - NEAR_CMP_BACK), (0, 0), (0, 0)))
    vc_p = jnp.pad(vc, ((0, 0), (NEAR_CMP_BACK, n_cmp_pad - n_cmp + NEAR_CMP - NEAR_CMP_BACK), (0, 0), (0, 0)))
    kc_p = kc_p.transpose(0, 2, 1, 3)
    vc_p = vc_p.transpose(0, 2, 1, 3)
    kct = kc_p[:, :, NEAR_CMP_BACK:NEAR_CMP_BACK + n_cmp_pad].transpose(0, 1, 3, 2).astype(MXU_DTYPE)
    kst = ksel.transpose(0, 2, 3, 1).astype(MXU_DTYPE)
    kwt = kwin.transpose(0, 2, 3, 1).astype(MXU_DTYPE)
    vs = vsel.transpose(0, 2, 1, 3).astype(MXU_DTYPE)
    vw = vwin.transpose(0, 2, 1, 3).astype(MXU_DTYPE)

    n_dist = MAX_DISTANCE
    by_dist = rel_bias[_t5_bucket(jnp.arange(n_dist))].astype(f32).reshape(n_dist, G, J)
    by_dist = by_dist.transpose(1, 2, 0)

    def table(dist):
        d = jnp.clip(dist, 0, n_dist - 1)
        return by_dist[:, :, d].reshape(G, R, dist.shape[1])

    r = jnp.arange(QB)[:, None]
    c = jnp.arange(QB)[None, :]
    band = jnp.stack([table(r - c), table(r - c + QB)], axis=1)
    u = jnp.arange(NEAR_CMP)[None, :]
    near = table(r - CMP_STRIDE * (u - NEAR_CMP_BACK) - (CMP_BLOCK - 1))
    far = jnp.repeat(by_dist[:, :, n_dist - 1], QB, axis=1).reshape(G, R, 1)
    imap = _importance_map(n_cmp_pad, n_sb)

    kv_spec = lambda shape: pl.BlockSpec((1, 1) + shape, lambda b, g, i: (b, g, 0, 0))
    out = pl.pallas_call(
        _nsa_prompt_body,
        grid=(B, G, nq),
        in_specs=[pl.BlockSpec((1, 1, 1, R, dh), lambda b, g, i: (b, g, i, 0, 0)),
                  pl.BlockSpec((1, 1, 1, R, 3), lambda b, g, i: (b, g, i, 0, 0)),
                  kv_spec((dh, n_cmp_pad)),
                  kv_spec((n_cmp_pad + NEAR_CMP, dh)),
                  kv_spec((n_cmp_pad + NEAR_CMP, dh)),
                  kv_spec((dh, L)), kv_spec((L, dh)), kv_spec((dh, L)), kv_spec((L, dh)),
                  pl.BlockSpec((1, 2, R, QB), lambda b, g, i: (g, 0, 0, 0)),
                  pl.BlockSpec((1, R, NEAR_CMP), lambda b, g, i: (g, 0, 0)),
                  pl.BlockSpec((1, R, 1), lambda b, g, i: (g, 0, 0)),
                  pl.BlockSpec((n_cmp_pad, n_sb), lambda b, g, i: (0, 0))],
        out_specs=pl.BlockSpec((1, 1, 1, R, dh), lambda b, g, i: (b, g, i, 0, 0)),
        out_shape=jax.ShapeDtypeStruct((B, G, nq, R, dh), f32),
        scratch_shapes=[pltpu.VMEM((R, 1), f32), pltpu.VMEM((R, 1), f32), pltpu.VMEM((R, dh), f32)],
        compiler_params=pltpu.CompilerParams(
            dimension_semantics=("parallel", "parallel", "arbitrary"),
            vmem_limit_bytes=VMEM_LIMIT_BYTES),
        name="nsa_prompt",
    )(q_r, gt_r, kct, kc_p, vc_p, kst, vs, kwt, vw, band, near, far, imap)
    return out.reshape(B, G, nq, J, QB, dh).transpose(0, 2, 4, 1, 3, 5).reshape(B, L, G * J * dh)


def _split(z, sizes):
    return jnp.split(z, np.cumsum(sizes)[:-1].tolist(), axis=-1)


def _layer_norm(x, g, b):
    mu = x.mean(-1, keepdims=True)
    var = jnp.square(x - mu).mean(-1, keepdims=True)
    return (x - mu) * lax.rsqrt(var + EPS) * g + b


def _masked_softmax(logits, mask):
    z = jnp.where(mask, logits.astype(jnp.float32), NEG)
    e = jnp.exp(z - z.max(-1, keepdims=True)) * mask
    s = e.sum(-1, keepdims=True)
    return e / jnp.where(s > 0, s, 1.0)


def _l2norm(a):
    return a * lax.rsqrt(jnp.sum(a * a, -1, keepdims=True) + EPS)


def _gated_rmsnorm(o, gate, w):
    B, L, H, dv = o.shape
    o = o * lax.rsqrt(jnp.mean(o * o, -1, keepdims=True) + EPS) * w
    return o.reshape(B, L, H * dv) * jax.nn.silu(gate)


def _t5_bucket(dist):
    n = jnp.maximum(dist, 0)
    exact = N_BUCKETS // 2
    rel = jnp.log(jnp.maximum(n, 1).astype(jnp.float32) / exact) / math.log(MAX_DISTANCE / exact)
    large = jnp.minimum(exact + (rel * (N_BUCKETS - exact)).astype(jnp.int32), N_BUCKETS - 1)
    return jnp.where(n < exact, n, large)


def _head_bias(dist, rel_bias):
    Q, K = dist.shape
    bias = rel_bias[_t5_bucket(dist)]
    return bias.reshape(Q, K, NSA_KV, NSA_GROUP).transpose(2, 3, 0, 1).astype(jnp.float32)


def _compress(rows, w1, w2, pe):
    B, T, G, dh = rows.shape
    n_chunk = T // CMP_STRIDE
    r = rows[:, :n_chunk * CMP_STRIDE].reshape(B, n_chunk, CMP_STRIDE, G, dh)
    w1h = w1.reshape(2, CMP_STRIDE, dh, CMP_HID)
    lo = jnp.einsum('bcsgd,sdh->bcgh', r, w1h[0])
    hi = jnp.einsum('bcsgd,sdh->bcgh', r, w1h[1])
    h = lo[:, :-1] + hi[:, 1:] + pe.reshape(-1) @ w1
    return jax.nn.silu(h) @ w2


def _importance_map(n_cmp, n_sb):
    c0 = jnp.arange(n_cmp) * CMP_STRIDE
    c1 = c0 + CMP_BLOCK - 1
    b0 = jnp.arange(n_sb) * SEL_BLOCK
    b1 = b0 + SEL_BLOCK - 1
    return ((c0[:, None] <= b1[None, :]) & (c1[:, None] >= b0[None, :])).astype(jnp.float32)


def _nsa_core(q, gates, t, kc, vc, cmp_end, imp_map, ks_blk, vs_blk, kw, vw, w_pos, rel_bias):
    f32 = jnp.float32
    B, Q = q.shape[:2]
    lg = jnp.einsum('bqgjd,bcgd->bgjqc', q, kc).astype(f32) + _head_bias(t[:, None] - cmp_end[None, :], rel_bias)
    p_cmp = _masked_softmax(lg, cmp_end[None, :] <= t[:, None])
    o_cmp = jnp.einsum('bgjqc,bcgd->bqgjd', p_cmp, vc)
    n_sb = ks_blk.shape[2]
    n_top = min(N_SELECT, n_sb)
    imp = jnp.einsum('bgjqc,cn->bgqn', p_cmp, imp_map)
    blk = jnp.arange(n_sb)[None, :]
    cur = (t // SEL_BLOCK)[:, None]
    forced = (blk == 0) | (blk == cur) | (blk == cur - 1)
    score = jnp.where(blk * SEL_BLOCK <= t[:, None], jnp.where(forced, jnp.inf, imp), -jnp.inf)
    top_s, top_i = lax.top_k(score, n_top)
    b_ix = jnp.arange(B)[:, None, None, None]
    g_ix = jnp.arange(NSA_KV)[None, :, None, None]
    n_keys = n_top * SEL_BLOCK
    ks = ks_blk[b_ix, g_ix, top_i].reshape(B, NSA_KV, Q, n_keys, NSA_DH)
    vs = vs_blk[b_ix, g_ix, top_i].reshape(B, NSA_KV, Q, n_keys, NSA_DH)
    s_pos = (top_i[..., None] * SEL_BLOCK + jnp.arange(SEL_BLOCK)).reshape(B, NSA_KV, Q, n_keys)
    s_ok = jnp.repeat(top_s > -jnp.inf, SEL_BLOCK, axis=-1) & (s_pos <= t[:, None])
    table = rel_bias.reshape(N_BUCKETS, NSA_KV, NSA_GROUP).transpose(1, 2, 0)
    bucket = _t5_bucket(t[:, None] - s_pos)
    s_bias = table[jnp.arange(NSA_KV)[None, :, None, None, None],
                   jnp.arange(NSA_GROUP)[None, None, :, None, None],
                   bucket[:, :, None]].astype(f32)
    lg = jnp.einsum('bqgjd,bgqmd->bgjqm', q, ks).astype(f32) + s_bias
    p_sel = _masked_softmax(lg, s_ok[:, :, None])
    o_sel = jnp.einsum('bgjqm,bgqmd->bqgjd', p_sel, vs)
    w_ok = (w_pos[None, :] <= t[:, None]) & (w_pos[None, :] > t[:, None] - WINDOW) & (w_pos[None, :] >= 0)
    lg = jnp.einsum('bqgjd,bkgd->bgjqk', q, kw).astype(f32) + _head_bias(t[:, None] - w_pos[None, :], rel_bias)
    p_win = _masked_softmax(lg, w_ok)
    o_win = jnp.einsum('bgjqk,bkgd->bqgjd', p_win, vw)
    o = gates[..., 0:1] * o_cmp + gates[..., 1:2] * o_sel + gates[..., 2:3] * o_win
    return o.reshape(B, Q, NSA_HEADS * NSA_DH)


def _nsa_mixer(q_raw, kv_raw, gate_raw, past_rows, win_buf, w_cmp1, w_cmp2, pe_cmp, rel_bias):
    B, L, _ = q_raw.shape
    q = (q_raw * NSA_DH ** -0.5).reshape(B, L, NSA_KV, NSA_GROUP, NSA_DH)
    gates = jax.nn.sigmoid(gate_raw).reshape(B, L, NSA_KV, NSA_GROUP, 3)
    kv = kv_raw.reshape(B, L, N_KV_STREAMS, NSA_KV, NSA_DH)
    new_rows, new_win = kv[:, :, :N_CACHE_STREAMS], kv[:, :, N_CACHE_STREAMS:]
    pos0 = 0 if past_rows is None else past_rows.shape[1]
    rows = new_rows if past_rows is None else jnp.concatenate([past_rows, new_rows], 1)
    T = rows.shape[1]
    kc = _compress(rows[:, :, 0], w_cmp1[0], w_cmp2[0], pe_cmp[0])
    vc = _compress(rows[:, :, 1], w_cmp1[1], w_cmp2[1], pe_cmp[1])
    wrows = new_win if win_buf is None else jnp.concatenate([win_buf, new_win], 1)
    keep = min(WINDOW, pos0 + L)
    if past_rows is None:
        o = _nsa_prompt_attention(q, gates, kc, vc, rows[:, :, 2], rows[:, :, 3],
                                  new_win[:, :, 0], new_win[:, :, 1], rel_bias)
        return o, new_rows, wrows[:, -keep:]
    n_cmp = kc.shape[1]
    cmp_end = jnp.arange(n_cmp) * CMP_STRIDE + CMP_BLOCK - 1
    n_sb = -(-T // SEL_BLOCK)
    sel = jnp.pad(rows[:, :, 2:4], ((0, 0), (0, n_sb * SEL_BLOCK - T), (0, 0), (0, 0), (0, 0)))
    sel = sel.reshape(B, n_sb, SEL_BLOCK, 2, NSA_KV, NSA_DH).transpose(3, 0, 4, 1, 2, 5)
    imp_map = _importance_map(n_cmp, n_sb)
    n_buf = wrows.shape[1] - L
    o = _nsa_core(q, gates, pos0 + jnp.arange(L), kc, vc, cmp_end, imp_map, sel[0], sel[1],
                  wrows[:, :, 0], wrows[:, :, 1], pos0 - n_buf + jnp.arange(n_buf + L), rel_bias)
    return o, new_rows, wrows[:, -keep:]


def _chunk(a, c):
    B, L = a.shape[:2]
    n = -(-L // c)
    a = jnp.pad(a, [(0, 0), (0, n * c - L)] + [(0, 0)] * (a.ndim - 2))
    a = a.reshape((B, n, c) + a.shape[2:])
    return jnp.moveaxis(a, (1, 2), (0, 3))


def _unchunk(o, L):
    n, B, H, c, d = o.shape
    return jnp.moveaxis(o, (0, 3), (1, 2)).reshape(B, n * c, H, d)[:, :L]


def _gated_delta(q, k, v, beta, g, s0):
    f32 = jnp.float32
    L = k.shape[1]
    c = min(DN_CHUNK, L)
    qs, ks, vs = (_chunk(a, c) for a in (q, k, v))
    bs, gs = _chunk(beta, c), _chunk(g, c)
    incl = jnp.tril(jnp.ones((c, c), bool))
    strict = jnp.tril(jnp.ones((c, c), bool), -1)
    eye = jnp.eye(c, dtype=f32)

    def step(S, xs):
        qc, kc, vc, bc, gc = xs
        G = jnp.cumsum(gc, -1)
        dec = jnp.where(incl, jnp.exp(jnp.where(incl, G[..., :, None] - G[..., None, :], 0.0)), 0.0)
        kb = kc * bc[..., None]
        A = jnp.where(strict, jnp.einsum('bhtk,bhsk->bhts', kb, kc) * dec, 0.0)
        Tm = lax.linalg.triangular_solve(eye + A, jnp.broadcast_to(eye, A.shape), left_side=True, lower=True)
        u = Tm @ (vc * bc[..., None])
        w = Tm @ (kb * jnp.exp(G)[..., None])
        v_new = u - w @ S
        attn = jnp.einsum('bhtk,bhsk->bhts', qc, kc) * dec
        o = (qc * jnp.exp(G)[..., None]) @ S + attn @ v_new
        S = S * jnp.exp(G[..., -1])[..., None, None] + jnp.einsum(
            'bhtk,bhtv->bhkv', kc * jnp.exp(G[..., -1:] - G)[..., None], v_new)
        return S, o

    S, o = lax.scan(step, s0, (qs, ks, vs, bs, gs))
    return _unchunk(o, L), S


def _gla_scan(q, k, v, logf, s0):
    L = k.shape[1]
    c = min(HG_CHUNK, L)
    qs, ks, vs, fs = (_chunk(a, c) for a in (q, k, v, logf))
    incl = jnp.tril(jnp.ones((c, c), bool))[:, :, None]

    def step(S, xs):
        qc, kc, vc, fc = xs
        G = jnp.cumsum(fc, axis=-2)
        diff = G[:, :, :, None, :] - G[:, :, None, :, :]
        dec = jnp.where(incl, jnp.exp(jnp.where(incl, diff, 0.0)), 0.0)
        attn = jnp.einsum('bhtk,bhsk,bhtsk->bhts', qc, kc, dec)
        o = jnp.einsum('bhtk,bhkv->bhtv', qc * jnp.exp(G), S) + jnp.einsum('bhts,bhsv->bhtv', attn, vc)
        S = S * jnp.exp(G[:, :, -1])[..., None] + jnp.einsum(
            'bhsk,bhsv->bhkv', kc * jnp.exp(G[:, :, -1:] - G), vc)
        return S, o

    S, o = lax.scan(step, s0, (qs, ks, vs, fs))
    return _unchunk(o, L), S


def _short_conv(u, buf, w):
    L = u.shape[1]
    xc = jnp.concatenate([buf, u], 1)
    y = sum(xc[:, j:j + L] * w[j] for j in range(DN_CONV))
    return jax.nn.silu(y), xc[:, -(DN_CONV - 1):]


def _deltanet_mixer(qkv_raw, a_raw, b_raw, gate_raw, conv_buf, s0, conv_w, a_log, dt_bias, norm_w):
    B, L, _ = qkv_raw.shape
    if conv_buf is None:
        conv_buf = jnp.zeros((B, DN_CONV - 1, DN_CONV_DIM), qkv_raw.dtype)
    if s0 is None:
        s0 = jnp.zeros((B, DN_HEADS, DN_DK, DN_DV), qkv_raw.dtype)
    u, new_buf = _short_conv(qkv_raw, conv_buf, conv_w)
    q, k, v = _split(u, (DN_HEADS * DN_DK, DN_HEADS * DN_DK, DN_HEADS * DN_DV))
    q = _l2norm(q.reshape(B, L, DN_HEADS, DN_DK)) * DN_DK ** -0.5
    k = _l2norm(k.reshape(B, L, DN_HEADS, DN_DK))
    v = v.reshape(B, L, DN_HEADS, DN_DV)
    beta = jax.nn.sigmoid(b_raw)
    g = -jnp.exp(a_log) * jax.nn.softplus(a_raw + dt_bias)
    o, s_new = _gated_delta(q, k, v, beta, g, s0)
    return _gated_rmsnorm(o, gate_raw, norm_w), new_buf, s_new


def _hgrn2_mixer(q_raw, f_raw, i_raw, gate_raw, s0, lb, norm_w):
    B, L, _ = q_raw.shape
    if s0 is None:
        s0 = jnp.zeros((B, HG_HEADS, HG_DK, HG_DV), q_raw.dtype)
    q = q_raw.reshape(B, L, HG_HEADS, HG_DK) * HG_DK ** -0.5
    f = lb + (1.0 - lb) * jax.nn.sigmoid(f_raw)
    logf = jnp.log(f).reshape(B, L, HG_HEADS, HG_DK)
    k = (1.0 - f).reshape(B, L, HG_HEADS, HG_DK)
    v = i_raw.reshape(B, L, HG_HEADS, HG_DV)
    o, s_new = _gla_scan(q, k, v, logf, s0)
    return _gated_rmsnorm(o, gate_raw, norm_w), s_new


def _moe(x, w_router, router_bias, wg, wu, wd, precise):
    f32 = jnp.float32
    N, D = x.shape
    scores = jax.nn.sigmoid(_mm(x, w_router, precise))
    biased = scores + router_bias
    grp = lax.top_k(biased.reshape(N, N_EXPERT_GROUPS, EXPERTS_PER_GROUP), TOP_K)[0].sum(-1)
    gsel = jnp.argmax(grp, -1)
    in_grp = (jnp.arange(N_EXPERTS) // EXPERTS_PER_GROUP)[None, :] == gsel[:, None]
    _, eidx = lax.top_k(jnp.where(in_grp, biased, -jnp.inf), TOP_K)
    gw = jnp.take_along_axis(scores, eidx, -1)
    gw = gw / gw.sum(-1, keepdims=True)
    n_assign = N * TOP_K
    e_flat = eidx.reshape(-1)
    onehot = (e_flat[:, None] == jnp.arange(N_EXPERTS)[None, :]).astype(jnp.int32)
    running = jnp.cumsum(onehot, axis=0)
    counts = running[-1]
    rank = jnp.take_along_axis(running, e_flat[:, None], axis=1)[:, 0] - 1
    padded = (counts + MOE_ROW_BLOCK - 1) // MOE_ROW_BLOCK * MOE_ROW_BLOCK
    pad_end = jnp.cumsum(padded)
    dest = ((pad_end - padded)[e_flat] + rank).astype(jnp.int32)
    n_blocks = -(-n_assign // MOE_ROW_BLOCK) + N_EXPERTS
    n_rows = n_blocks * MOE_ROW_BLOCK
    tok = jnp.full((n_rows,), N, jnp.int32).at[dest].set(jnp.arange(n_assign, dtype=jnp.int32) // TOP_K)
    wt = jnp.zeros((n_rows,), f32).at[dest].set(gw.reshape(-1))
    blk_exp = jnp.minimum(jnp.searchsorted(pad_end, jnp.arange(n_blocks) * MOE_ROW_BLOCK, side='right'),
                          N_EXPERTS - 1).astype(jnp.int32)
    x_pad = jnp.concatenate([x, jnp.zeros((1, D), x.dtype)])
    out = _expert_ffn(x_pad[tok], wt, blk_exp, wg, wu, wd, precise)
    slot = dest.reshape(N, TOP_K)
    return out[slot[:, 0]] + out[slot[:, 1]]


def _trunk_layer(x, ple, lb, past, precise, w_in, w_cmp1, w_cmp2, pe_cmp, dn_conv_w, dn_a_log, dn_dt_bias,
                 dn_norm_w, hg_norm_w, w_branch, w_out, ln1_g, ln1_b, ln2_g, ln2_b, w_router,
                 router_bias, w_exp_gate, w_exp_up, w_exp_down, w_ple, w_ple_gate, rel_bias):
    past_rows, win_buf, conv_buf, dn_state, hg_state = past
    B, L, D = x.shape
    x2 = x.reshape(B * L, D)
    mm = functools.partial(_mm, precise=precise)
    z = mm(x2, w_in).reshape(B, L, -1)
    (nsa_q, nsa_kv, nsa_g, dn_qkv, dn_a, dn_b, dn_g,
     hg_q, hg_f, hg_i, hg_g, merge) = _split(z, IN_SPLITS)
    o_a, rows_new, win_new = _nsa_mixer(nsa_q, nsa_kv, nsa_g, past_rows, win_buf, w_cmp1, w_cmp2, pe_cmp, rel_bias)
    o_b, conv_new, dn_new = _deltanet_mixer(dn_qkv, dn_a, dn_b, dn_g, conv_buf, dn_state, dn_conv_w,
                                            dn_a_log, dn_dt_bias, dn_norm_w)
    o_c, hg_new = _hgrn2_mixer(hg_q, hg_f, hg_i, hg_g, hg_state, lb, hg_norm_w)
    g = jax.nn.sigmoid(merge).reshape(B * L, 3, D)
    m = (g[:, 0] * mm(o_a.reshape(B * L, -1), w_branch[0]) + g[:, 1] * mm(o_b.reshape(B * L, -1), w_branch[1])
         + g[:, 2] * mm(o_c.reshape(B * L, -1), w_branch[2]))
    x2 = _layer_norm(DEEPNORM_ALPHA * x2 + mm(m, w_out), ln1_g, ln1_b)
    ffn = _moe(x2, w_router, router_bias, w_exp_gate, w_exp_up, w_exp_down, precise)
    ple_term = mm(ple.reshape(B * L, -1), w_ple) * jax.nn.sigmoid(mm(x2, w_ple_gate))
    x2 = _layer_norm(DEEPNORM_ALPHA * x2 + ffn + ple_term, ln2_g, ln2_b)
    return x2.reshape(B, L, D), (rows_new, win_new, conv_new, dn_new, hg_new)


def kernel(x_prompt, x_sample, cache_nsa, state_win, state_dn_conv, state_dn, state_hg, page_table,
           p_prompt, p_sample, w_in, w_cmp1, w_cmp2, pe_cmp, dn_conv_w, dn_a_log, dn_dt_bias, dn_norm_w,
           hg_lb_logits, hg_norm_w, w_branch, w_out, ln1_g, ln1_b, ln2_g, ln2_b, w_router, router_bias,
           w_exp_gate, w_exp_up, w_exp_down, w_ple, w_ple_gate, rel_bias):
    lb = jnp.cumsum(jax.nn.softmax(hg_lb_logits, axis=0), axis=0)
    lb = lb - lb[0:1]
    n_dec = page_table.shape[0]
    y_prompt, y_sample = x_prompt, x_sample
    st_p, st_s = [], []
    for l in range(DEPTH):
        lw = (w_in[l], w_cmp1[l], w_cmp2[l], pe_cmp[l], dn_conv_w[l], dn_a_log[l], dn_dt_bias[l],
              dn_norm_w[l], hg_norm_w[l], w_branch[l], w_out[l], ln1_g[l], ln1_b[l], ln2_g[l], ln2_b[l],
              w_router, router_bias, w_exp_gate[l], w_exp_up[l], w_exp_down[l], w_ple[l], w_ple_gate[l],
              rel_bias)
        y_prompt, s = _trunk_layer(y_prompt, p_prompt[l], lb[l], (None, None, None, None, None), False, *lw)
        st_p.append(s)
        pages = jnp.take(cache_nsa[l].reshape(cache_nsa.shape[1], -1), page_table.reshape(-1), axis=0)
        past_rows = pages.reshape(n_dec, -1, N_CACHE_STREAMS, NSA_KV, NSA_DH)
        y_sample, s = _trunk_layer(y_sample, p_sample[l], lb[l],
                                   (past_rows, state_win[l], state_dn_conv[l], state_dn[l], state_hg[l]),
                                   True, *lw)
        st_s.append(s)
    outs = [y_prompt, y_sample]
    for i in range(5):
        outs.append(jnp.stack([s[i] for s in st_p]))
        outs.append(jnp.stack([s[i] for s in st_s]))
    return tuple(outs)
```

```python
import functools
import math

import jax
import jax.numpy as jnp
import numpy as np
from jax import lax
from jax.experimental import pallas as pl
from jax.experimental.pallas import tpu as pltpu

D_MODEL = 1024
DEPTH = 4
NSA_HEADS = 8
NSA_KV = 2
NSA_GROUP = NSA_HEADS // NSA_KV
NSA_DH = 64
CMP_STRIDE = 16
CMP_BLOCK = 2 * CMP_STRIDE
CMP_HID = 128
SEL_BLOCK = 64
N_SELECT = 16
WINDOW = 512
Q_BLOCK = 128
N_KV_STREAMS = 6
N_CACHE_STREAMS = 4
N_BUCKETS = 32
MAX_DISTANCE = 128
DN_HEADS = 4
DN_DK = 128
DN_DV = 128
DN_CONV = 4
DN_CHUNK = 64
DN_CONV_DIM = DN_HEADS * (2 * DN_DK + DN_DV)
HG_HEADS = 4
HG_DK = 128
HG_DV = 128
HG_CHUNK = 64
N_EXPERTS = 16
N_EXPERT_GROUPS = 4
EXPERTS_PER_GROUP = N_EXPERTS // N_EXPERT_GROUPS
TOP_K = 2
D_EXPERT = 512
MOE_ROW_BLOCK = 128
DEEPNORM_ALPHA = (2 * DEPTH) ** 0.25
EPS = 1e-6
NEG = -1e30
IN_SPLITS = (NSA_HEADS * NSA_DH, N_KV_STREAMS * NSA_KV * NSA_DH, 3 * NSA_HEADS,
             DN_CONV_DIM, DN_HEADS, DN_HEADS, DN_HEADS * DN_DV,
             HG_HEADS * HG_DK, HG_HEADS * HG_DK, HG_HEADS * HG_DV, HG_HEADS * HG_DV,
             3 * D_MODEL)

VMEM_LIMIT_BYTES = 48 * 1024 * 1024
MM_ROW_TILE = 1024
MM_COL_TILE = 512


def _mm_body(x_ref, w_ref, o_ref, *, precise):
    if precise:
        o_ref[...] = jnp.dot(x_ref[...], w_ref[...], preferred_element_type=jnp.float32,
                             precision=lax.Precision.HIGHEST)
    else:
        o_ref[...] = jnp.dot(x_ref[...].astype(jnp.bfloat16), w_ref[...].astype(jnp.bfloat16),
                             preferred_element_type=jnp.float32)


def _mm(x, w, precise=False):
    M, K = x.shape
    N = w.shape[1]
    tm = min(MM_ROW_TILE, M)
    tn = min(MM_COL_TILE, N)
    return pl.pallas_call(
        functools.partial(_mm_body, precise=precise),
        grid=(pl.cdiv(M, tm), pl.cdiv(N, tn)),
        in_specs=[pl.BlockSpec((tm, K), lambda i, j: (i, 0)),
                  pl.BlockSpec((K, tn), lambda i, j: (0, j))],
        out_specs=pl.BlockSpec((tm, tn), lambda i, j: (i, j)),
        out_shape=jax.ShapeDtypeStruct((M, N), jnp.float32),
        compiler_params=pltpu.CompilerParams(
            dimension_semantics=("parallel", "parallel"),
            vmem_limit_bytes=VMEM_LIMIT_BYTES),
        name="mm",
    )(x, w)


def _expert_body(blk_exp_ref, h_ref, wt_ref, wg_ref, wu_ref, wd_ref, o_ref, *, precise):
    del blk_exp_ref
    if precise:
        dot = functools.partial(jnp.dot, preferred_element_type=jnp.float32,
                                precision=lax.Precision.HIGHEST)
        h = h_ref[...]
        a = jax.nn.silu(dot(h, wg_ref[0])) * dot(h, wu_ref[0])
        o_ref[...] = dot(a, wd_ref[0]) * wt_ref[...]
    else:
        bf = jnp.bfloat16
        dot = functools.partial(jnp.dot, preferred_element_type=jnp.float32)
        h = h_ref[...].astype(bf)
        a = jax.nn.silu(dot(h, wg_ref[0].astype(bf))) * dot(h, wu_ref[0].astype(bf))
        o_ref[...] = dot(a.astype(bf), wd_ref[0].astype(bf)) * wt_ref[...]


def _expert_ffn(h, wt, blk_exp, wg, wu, wd, precise):
    n_rows, D = h.shape
    n_blocks = n_rows // MOE_ROW_BLOCK
    grid_spec = pltpu.PrefetchScalarGridSpec(
        num_scalar_prefetch=1,
        grid=(n_blocks,),
        in_specs=[pl.BlockSpec((MOE_ROW_BLOCK, D), lambda i, be: (i, 0)),
                  pl.BlockSpec((MOE_ROW_BLOCK, 1), lambda i, be: (i, 0)),
                  pl.BlockSpec((1, D, D_EXPERT), lambda i, be: (be[i], 0, 0)),
                  pl.BlockSpec((1, D, D_EXPERT), lambda i, be: (be[i], 0, 0)),
                  pl.BlockSpec((1, D_EXPERT, D), lambda i, be: (be[i], 0, 0))],
        out_specs=pl.BlockSpec((MOE_ROW_BLOCK, D), lambda i, be: (i, 0)),
    )
    return pl.pallas_call(
        functools.partial(_expert_body, precise=precise),
        grid_spec=grid_spec,
        out_shape=jax.ShapeDtypeStruct((n_rows, D), jnp.float32),
        compiler_params=pltpu.CompilerParams(
            dimension_semantics=("arbitrary",),
            vmem_limit_bytes=VMEM_LIMIT_BYTES),
        name="expert_ffn",
    )(blk_exp, h, wt.reshape(n_rows, 1), wg, wu, wd)


NSA_ROWS = NSA_GROUP * Q_BLOCK
NEAR_CMP = 32
NEAR_CMP_BACK = 16
CMP_PER_QBLOCK = Q_BLOCK // CMP_STRIDE
SEL_PER_TILE = Q_BLOCK // SEL_BLOCK
WIN_TILES = WINDOW // Q_BLOCK
MXU_DTYPE = jnp.bfloat16


def _nsa_prompt_body(q_ref, gt_ref, kct_ref, kc_ref, vc_ref, kst_ref, vs_ref, kwt_ref, vw_ref,
                     band_ref, near_ref, far_ref, imap_ref, o_ref, m_scr, l_scr, acc_scr):
    f32 = jnp.float32
    R, QB = NSA_ROWS, Q_BLOCK
    i = pl.program_id(2)
    n_cmp_pad = kct_ref.shape[-1]
    n_sb = imap_ref.shape[-1]
    q = q_ref[0, 0, 0]
    far_bias = far_ref[0]
    r_row = lax.broadcasted_iota(jnp.int32, (R, 1), 0) & (QB - 1)
    t_row = i * QB + r_row

    def mxu(a, b):
        return jnp.dot(a.astype(MXU_DTYPE), b.astype(MXU_DTYPE), preferred_element_type=f32)

    def mxu_nt(a, b):
        return lax.dot_general(a.astype(MXU_DTYPE), b.astype(MXU_DTYPE), (((1,), (1,)), ((), ())),
                               preferred_element_type=f32)

    c0 = CMP_PER_QBLOCK * i - NEAR_CMP_BACK
    lg_far = mxu(q, kct_ref[0, 0]) + far_bias
    mk_far = lax.broadcasted_iota(jnp.int32, (R, n_cmp_pad), 1) < c0
    near_start = pl.multiple_of(CMP_PER_QBLOCK * i, 8)
    k_near = kc_ref[0, 0, pl.ds(near_start, NEAR_CMP), :]
    v_near = vc_ref[0, 0, pl.ds(near_start, NEAR_CMP), :]
    lg_near = mxu_nt(q, k_near) + near_ref[0]
    c_near = c0 + lax.broadcasted_iota(jnp.int32, (R, NEAR_CMP), 1)
    mk_near = (c_near >= 0) & (c_near * CMP_STRIDE + (CMP_BLOCK - 1) <= t_row)
    z_far = jnp.where(mk_far, lg_far, NEG)
    z_near = jnp.where(mk_near, lg_near, NEG)
    mx = jnp.maximum(z_far.max(-1, keepdims=True), z_near.max(-1, keepdims=True))
    e_far = jnp.where(mk_far, jnp.exp(z_far - mx), 0.0)
    e_near = jnp.where(mk_near, jnp.exp(z_near - mx), 0.0)
    s = e_far.sum(-1, keepdims=True) + e_near.sum(-1, keepdims=True)
    s = jnp.where(s > 0, s, 1.0)
    p_far = e_far / s
    p_near = e_near / s
    o_cmp = (mxu(p_far, vc_ref[0, 0, NEAR_CMP_BACK:NEAR_CMP_BACK + n_cmp_pad, :]) + mxu(p_near, v_near))

    cn = c0 + lax.broadcasted_iota(jnp.int32, (NEAR_CMP, n_sb), 0)
    nn = lax.broadcasted_iota(jnp.int32, (NEAR_CMP, n_sb), 1)
    blocks_per_sel = SEL_BLOCK // CMP_STRIDE
    imap_near = ((cn >= blocks_per_sel * nn - 1) & (cn <= blocks_per_sel * nn + blocks_per_sel - 1)).astype(f32)
    imp_rows = mxu(p_far, imap_ref[...]) + mxu(p_near, imap_near)
    imp = imp_rows[0:QB]
    for j in range(1, NSA_GROUP):
        imp = imp + imp_rows[j * QB:(j + 1) * QB]
    n_io = lax.broadcasted_iota(jnp.int32, (QB, n_sb), 1)
    n_f = n_io.astype(f32)
    r_q = lax.broadcasted_iota(jnp.int32, (QB, n_sb), 0)
    t_q = i * QB + r_q
    cur = SEL_PER_TILE * i + (r_q >= SEL_BLOCK).astype(jnp.int32)
    forced = (n_io == 0) | (n_io == cur) | (n_io == cur - 1)
    score = jnp.where(n_io * SEL_BLOCK <= t_q, jnp.where(forced, jnp.inf, imp), -jnp.inf)

    def pick_one(_, carry):
        score, sel = carry
        top = score.max(-1, keepdims=True)
        first = jnp.where(score == top, n_f, float(n_sb)).min(-1, keepdims=True)
        hit = n_f == first
        sel = jnp.where(hit & (top > -jnp.inf), 1.0, sel)
        return jnp.where(hit, -jnp.inf, score), sel

    _, sel = lax.fori_loop(0, min(N_SELECT, n_sb), pick_one, (score, jnp.zeros((QB, n_sb), f32)))

    r_t = lax.broadcasted_iota(jnp.int32, (R, QB), 0) & (QB - 1)
    c_t = lax.broadcasted_iota(jnp.int32, (R, QB), 1)

    def reset():
        m_scr[...] = jnp.full((R, 1), NEG, f32)
        l_scr[...] = jnp.zeros((R, 1), f32)
        acc_scr[...] = jnp.zeros_like(acc_scr)

    def update(lg, mk, v):
        z = jnp.where(mk, lg, NEG) if mk is not None else lg
        m_old = m_scr[...]
        m_new = jnp.maximum(m_old, z.max(-1, keepdims=True))
        p = jnp.exp(z - m_new)
        if mk is not None:
            p = jnp.where(mk, p, 0.0)
        alpha = jnp.exp(m_old - m_new)
        l_scr[...] = alpha * l_scr[...] + p.sum(-1, keepdims=True)
        acc_scr[...] = alpha * acc_scr[...] + mxu(p, v)
        m_scr[...] = m_new

    def result():
        l = l_scr[...]
        return acc_scr[...] / jnp.where(l > 0, l, 1.0)

    def tile(kt_ref, v_ref, kt):
        start = pl.multiple_of(kt * QB, QB)
        return mxu(q, kt_ref[0, 0, :, pl.ds(start, QB)]), v_ref[0, 0, pl.ds(start, QB), :]

    def sel_mask(kt):
        blk = SEL_PER_TILE * kt + (lax.broadcasted_iota(jnp.int32, (n_sb, QB), 1) >= SEL_BLOCK).astype(jnp.int32)
        expand = (lax.broadcasted_iota(jnp.int32, (n_sb, QB), 0) == blk).astype(f32)
        mq = mxu(sel, expand)
        return jnp.concatenate([mq] * NSA_GROUP, axis=0) > 0.5

    reset()

    def sel_far(kt, carry):
        lg, v = tile(kst_ref, vs_ref, kt)
        update(lg + far_bias, sel_mask(kt), v)
        return carry

    lax.fori_loop(0, jnp.maximum(i - 1, 0), sel_far, 0)

    @pl.when(i >= 1)
    def _():
        lg, v = tile(kst_ref, vs_ref, i - 1)
        update(lg + band_ref[0, 1], sel_mask(i - 1), v)

    lg, v = tile(kst_ref, vs_ref, i)
    update(lg + band_ref[0, 0], sel_mask(i) & (c_t <= r_t), v)
    o_sel = result()

    reset()

    @pl.when(i >= WIN_TILES)
    def _():
        lg, v = tile(kwt_ref, vw_ref, i - WIN_TILES)
        update(lg + far_bias, c_t > r_t, v)

    for back in range(WIN_TILES - 1, 1, -1):
        @pl.when(i >= back)
        def _(back=back):
            lg, v = tile(kwt_ref, vw_ref, i - back)
            update(lg + far_bias, None, v)

    @pl.when(i >= 1)
    def _():
        lg, v = tile(kwt_ref, vw_ref, i - 1)
        update(lg + band_ref[0, 1], None, v)

    lg, v = tile(kwt_ref, vw_ref, i)
    update(lg + band_ref[0, 0], c_t <= r_t, v)
    o_win = result()

    gt = gt_ref[0, 0, 0]
    o_ref[0, 0, 0] = gt[:, 0:1] * o_cmp + gt[:, 1:2] * o_sel + gt[:, 2:3] * o_win


def _nsa_prompt_attention(q, gates, kc, vc, ksel, vsel, kwin, vwin, rel_bias):
    f32 = jnp.float32
    B, L = q.shape[:2]
    G, J, dh, QB, R = NSA_KV, NSA_GROUP, NSA_DH, Q_BLOCK, NSA_ROWS
    assert L % QB == 0 and WINDOW % QB == 0 and QB % SEL_BLOCK == 0
    nq = L // QB
    n_cmp = kc.shape[1]
    n_cmp_pad = L // CMP_STRIDE
    n_sb = L // SEL_BLOCK
    assert n_cmp == n_cmp_pad - 1

    def rows(a):
        x = a.shape[-1]
        return a.reshape(B, nq, QB, G, J, x).transpose(0, 3, 1, 4, 2, 5).reshape(B, G, nq, R, x)

    q_r = rows(q).astype(MXU_DTYPE)
    gt_r = rows(gates)
    kc_p = jnp.pad(kc, ((0, 0), (NEAR_CMP_BACK, n_cmp_pad - n_cmp + NEAR_CMP - NEAR_CMP_BACK), (0, 0), (0, 0)))
    vc_p = jnp.pad(vc, ((0, 0), (NEAR_CMP_BACK, n_cmp_pad - n_cmp + NEAR_CMP - NEAR_CMP_BACK), (0, 0), (0, 0)))
    kc_p = kc_p.transpose(0, 2, 1, 3)
    vc_p = vc_p.transpose(0, 2, 1, 3)
    kct = kc_p[:, :, NEAR_CMP_BACK:NEAR_CMP_BACK + n_cmp_pad].transpose(0, 1, 3, 2).astype(MXU_DTYPE)
    kst = ksel.transpose(0, 2, 3, 1).astype(MXU_DTYPE)
    kwt = kwin.transpose(0, 2, 3, 1).astype(MXU_DTYPE)
    vs = vsel.transpose(0, 2, 1, 3).astype(MXU_DTYPE)
    vw = vwin.transpose(0, 2, 1, 3).astype(MXU_DTYPE)

    n_dist = MAX_DISTANCE
    by_dist = rel_bias[_t5_bucket(jnp.arange(n_dist))].astype(f32).reshape(n_dist, G, J)
    by_dist = by_dist.transpose(1, 2, 0)

    def table(dist):
        d = jnp.clip(dist, 0, n_dist - 1)
        return by_dist[:, :, d].reshape(G, R, dist.shape[1])

    r = jnp.arange(QB)[:, None]
    c = jnp.arange(QB)[None, :]
    band = jnp.stack([table(r - c), table(r - c + QB)], axis=1)
    u = jnp.arange(NEAR_CMP)[None, :]
    near = table(r - CMP_STRIDE * (u - NEAR_CMP_BACK) - (CMP_BLOCK - 1))
    far = jnp.repeat(by_dist[:, :, n_dist - 1], QB, axis=1).reshape(G, R, 1)
    imap = _importance_map(n_cmp_pad, n_sb)

    kv_spec = lambda shape: pl.BlockSpec((1, 1) + shape, lambda b, g, i: (b, g, 0, 0))
    out = pl.pallas_call(
        _nsa_prompt_body,
        grid=(B, G, nq),
        in_specs=[pl.BlockSpec((1, 1, 1, R, dh), lambda b, g, i: (b, g, i, 0, 0)),
                  pl.BlockSpec((1, 1, 1, R, 3), lambda b, g, i: (b, g, i, 0, 0)),
                  kv_spec((dh, n_cmp_pad)),
                  kv_spec((n_cmp_pad + NEAR_CMP, dh)),
                  kv_spec((n_cmp_pad + NEAR_CMP, dh)),
                  kv_spec((dh, L)), kv_spec((L, dh)), kv_spec((dh, L)), kv_spec((L, dh)),
                  pl.BlockSpec((1, 2, R, QB), lambda b, g, i: (g, 0, 0, 0)),
                  pl.BlockSpec((1, R, NEAR_CMP), lambda b, g, i: (g, 0, 0)),
                  pl.BlockSpec((1, R, 1), lambda b, g, i: (g, 0, 0)),
                  pl.BlockSpec((n_cmp_pad, n_sb), lambda b, g, i: (0, 0))],
        out_specs=pl.BlockSpec((1, 1, 1, R, dh), lambda b, g, i: (b, g, i, 0, 0)),
        out_shape=jax.ShapeDtypeStruct((B, G, nq, R, dh), f32),
        scratch_shapes=[pltpu.VMEM((R, 1), f32), pltpu.VMEM((R, 1), f32), pltpu.VMEM((R, dh), f32)],
        compiler_params=pltpu.CompilerParams(
            dimension_semantics=("parallel", "parallel", "arbitrary"),
            vmem_limit_bytes=VMEM_LIMIT_BYTES),
        name="nsa_prompt",
    )(q_r, gt_r, kct, kc_p, vc_p, kst, vs, kwt, vw, band, near, far, imap)
    return out.reshape(B, G, nq, J, QB, dh).transpose(0, 2, 4, 1, 3, 5).reshape(B, L, G * J * dh)


def _hdot(a, b):
    return jnp.dot(a, b, preferred_element_type=jnp.float32, precision=lax.Precision.HIGHEST)


def _hdot_nt(a, b):
    return lax.dot_general(a, b, (((1,), (1,)), ((), ())), preferred_element_type=jnp.float32,
                           precision=lax.Precision.HIGHEST)


ROUTER_OUT_LANES = 128


def _router_body(x_ref, w_ref, b_ref, o_ref, *, precise):
    f32 = jnp.float32
    if precise:
        logits = _hdot(x_ref[...], w_ref[...])
    else:
        logits = jnp.dot(x_ref[...].astype(jnp.bfloat16), w_ref[...].astype(jnp.bfloat16),
                         preferred_element_type=f32)
    scores = jax.nn.sigmoid(logits)
    biased = scores + b_ref[...]
    tm = scores.shape[0]
    lane = lax.broadcasted_iota(jnp.int32, (tm, N_EXPERTS), 1).astype(f32)
    e_io = lax.broadcasted_iota(jnp.int32, (tm, N_EXPERTS), 1)
    grp = sum((e_io >= gi * EXPERTS_PER_GROUP).astype(jnp.int32) for gi in range(1, N_EXPERT_GROUPS))

    def top2(vals):
        m1 = vals.max(-1, keepdims=True)
        i1 = jnp.where(vals == m1, lane, float(N_EXPERTS)).min(-1, keepdims=True)
        rest = jnp.where(lane == i1, -jnp.inf, vals)
        m2 = rest.max(-1, keepdims=True)
        i2 = jnp.where(rest == m2, lane, float(N_EXPERTS)).min(-1, keepdims=True)
        return m1, i1, m2, i2

    best = None
    gsel = None
    for gi in range(N_EXPERT_GROUPS):
        m1, _, m2, _ = top2(jnp.where(grp == gi, biased, -jnp.inf))
        gsum = m1 + m2
        if best is None:
            best, gsel = gsum, jnp.zeros_like(gsum, dtype=jnp.int32)
        else:
            better = gsum > best
            best = jnp.where(better, gsum, best)
            gsel = jnp.where(better, gi, gsel)
    _, i1, _, i2 = top2(jnp.where(grp == gsel, biased, -jnp.inf))
    s1 = jnp.where(lane == i1, scores, 0.0).sum(-1, keepdims=True)
    s2 = jnp.where(lane == i2, scores, 0.0).sum(-1, keepdims=True)
    tot = s1 + s2
    col = lax.broadcasted_iota(jnp.int32, (tm, ROUTER_OUT_LANES), 1)
    o_ref[...] = jnp.where(col == 0, i1, jnp.where(col == 1, i2, jnp.where(col == 2, s1 / tot, s2 / tot)))


def _router(x, w_router, router_bias, precise):
    N, D = x.shape
    tm = min(MM_ROW_TILE, N)
    out = pl.pallas_call(
        functools.partial(_router_body, precise=precise),
        grid=(pl.cdiv(N, tm),),
        in_specs=[pl.BlockSpec((tm, D), lambda i: (i, 0)),
                  pl.BlockSpec((D, N_EXPERTS), lambda i: (0, 0)),
                  pl.BlockSpec((1, N_EXPERTS), lambda i: (0, 0))],
        out_specs=pl.BlockSpec((tm, ROUTER_OUT_LANES), lambda i: (i, 0)),
        out_shape=jax.ShapeDtypeStruct((N, ROUTER_OUT_LANES), jnp.float32),
        compiler_params=pltpu.CompilerParams(dimension_semantics=("parallel",),
                                             vmem_limit_bytes=VMEM_LIMIT_BYTES),
        name="router",
    )(x, w_router, router_bias.reshape(1, N_EXPERTS))
    return out[:, 0:2].astype(jnp.int32), out[:, 2:4]


def _tri(c, strict=False, upper=False):
    r = lax.broadcasted_iota(jnp.int32, (c, c), 0)
    s = lax.broadcasted_iota(jnp.int32, (c, c), 1)
    if upper:
        r, s = s, r
    return (r > s) if strict else (r >= s)


def _dn_body(u_ref, gcol_ref, grow_ref, beta_ref, gate_ref, nw_ref, s0_ref, o_ref, s_ref):
    f32 = jnp.float32
    nb, c = u_ref.shape[0], u_ref.shape[1]
    dk, dv = DN_DK, DN_DV

    @pl.when(pl.program_id(0) == 0)
    def _():
        s_ref[...] = s0_ref[...]

    incl = _tri(c)
    strict = _tri(c, strict=True)
    eye = (lax.broadcasted_iota(jnp.int32, (c, c), 0) == lax.broadcasted_iota(jnp.int32, (c, c), 1)).astype(f32)
    nw = nw_ref[...]
    for b in range(nb):
        g_col = gcol_ref[b, 0]
        g_row = grow_ref[b, 0]
        beta_all = beta_ref[b, 0]
        for h in range(DN_HEADS):
            q = u_ref[b, :, h * dk:(h + 1) * dk]
            k = u_ref[b, :, DN_HEADS * dk + h * dk:DN_HEADS * dk + (h + 1) * dk]
            v = u_ref[b, :, 2 * DN_HEADS * dk + h * dv:2 * DN_HEADS * dk + (h + 1) * dv]
            q = q * lax.rsqrt(jnp.sum(q * q, -1, keepdims=True) + EPS) * dk ** -0.5
            k = k * lax.rsqrt(jnp.sum(k * k, -1, keepdims=True) + EPS)
            gc = g_col[:, h:h + 1]
            gr = g_row[h:h + 1, :]
            beta = beta_all[:, h:h + 1]
            dec = jnp.where(incl, jnp.exp(jnp.where(incl, gc - gr, 0.0)), 0.0)
            kb = k * beta
            a = jnp.where(strict, _hdot_nt(kb, k) * dec, 0.0)
            p = -a
            tm = eye + p
            n_sq = max(1, math.ceil(math.log2(c))) - 1
            for _ in range(n_sq):
                p = _hdot(p, p)
                tm = tm + _hdot(tm, p)
            eg = jnp.exp(gc)
            u = _hdot(tm, v * beta)
            w = _hdot(tm, kb * eg)
            s = s_ref[b, h]
            v_new = u - _hdot(w, s)
            attn = _hdot_nt(q, k) * dec
            o = _hdot(q * eg, s) + _hdot(attn, v_new)
            g_last = gc[c - 1:c, :]
            s_ref[b, h] = s * jnp.exp(g_last) + _hdot((k * jnp.exp(g_last - gc)).T, v_new)
            o = o * lax.rsqrt(jnp.mean(o * o, -1, keepdims=True) + EPS) * nw
            o_ref[b, :, h * dv:(h + 1) * dv] = o * jax.nn.silu(gate_ref[b, :, h * dv:(h + 1) * dv])


def _dn_scan(u, g, beta, gate, norm_w, s0):
    f32 = jnp.float32
    B, L, _ = u.shape
    c = min(DN_CHUNK, L)
    assert L % c == 0
    n = L // c
    H = DN_HEADS
    g4 = jnp.cumsum(g.reshape(B, n, c, H), axis=2)
    return pl.pallas_call(
        _dn_body,
        grid=(n,),
        in_specs=[pl.BlockSpec((B, c, DN_CONV_DIM), lambda i: (0, i, 0)),
                  pl.BlockSpec((B, 1, c, H), lambda i: (0, i, 0, 0)),
                  pl.BlockSpec((B, 1, H, c), lambda i: (0, i, 0, 0)),
                  pl.BlockSpec((B, 1, c, H), lambda i: (0, i, 0, 0)),
                  pl.BlockSpec((B, c, H * DN_DV), lambda i: (0, i, 0)),
                  pl.BlockSpec((1, DN_DV), lambda i: (0, 0)),
                  pl.BlockSpec((B, H, DN_DK, DN_DV), lambda i: (0, 0, 0, 0))],
        out_specs=[pl.BlockSpec((B, c, H * DN_DV), lambda i: (0, i, 0)),
                   pl.BlockSpec((B, H, DN_DK, DN_DV), lambda i: (0, 0, 0, 0))],
        out_shape=[jax.ShapeDtypeStruct((B, L, H * DN_DV), f32),
                   jax.ShapeDtypeStruct((B, H, DN_DK, DN_DV), f32)],
        compiler_params=pltpu.CompilerParams(dimension_semantics=("arbitrary",),
                                             vmem_limit_bytes=VMEM_LIMIT_BYTES),
        name="deltanet_scan",
    )(u, g4, g4.transpose(0, 1, 3, 2), beta.reshape(B, n, c, H), gate, norm_w.reshape(1, DN_DV), s0)


HG_SUB = 16


def _hg_body(q_ref, f_ref, i_ref, gate_ref, lb_ref, nw_ref, s0_ref, o_ref, s_ref):
    f32 = jnp.float32
    nb, c = q_ref.shape[0], q_ref.shape[1]
    dk, dv = HG_DK, HG_DV
    sub = min(HG_SUB, c)

    @pl.when(pl.program_id(0) == 0)
    def _():
        s_ref[...] = s0_ref[...]

    lower = _tri(c).astype(f32)
    t_sub = lax.broadcasted_iota(jnp.int32, (sub, 1), 0)
    nw = nw_ref[...]
    for b in range(nb):
        for h in range(HG_HEADS):
            sl = slice(h * dk, (h + 1) * dk)
            q = q_ref[b, :, sl] * dk ** -0.5
            lb = lb_ref[:, sl]
            f = lb + (1.0 - lb) * jax.nn.sigmoid(f_ref[b, :, sl])
            k = 1.0 - f
            v = i_ref[b, :, h * dv:(h + 1) * dv]
            gcum = _hdot(lower, jnp.log(f))
            st = s_ref[b, h]
            o_inter = _hdot_nt(q * jnp.exp(gcum), st)
            for blk in range(c // sub):
                r0 = blk * sub
                q_i = q[r0:r0 + sub]
                g_i = gcum[r0:r0 + sub]
                o_i = o_inter[r0:r0 + sub]
                if blk > 0:
                    g_b = gcum[r0 - 1:r0]
                    a = _hdot_nt(q_i * jnp.exp(g_i - g_b), k[:r0] * jnp.exp(g_b - gcum[:r0]))
                    o_i = o_i + _hdot(a, v[:r0])
                for s in range(sub):
                    seen = t_sub >= s
                    term = q_i * k[r0 + s:r0 + s + 1] * jnp.exp(jnp.where(seen, g_i - gcum[r0 + s:r0 + s + 1], 0.0))
                    a_col = jnp.where(seen, term.sum(-1, keepdims=True), 0.0)
                    o_i = o_i + a_col * v[r0 + s:r0 + s + 1]
                o_i = o_i * lax.rsqrt(jnp.mean(o_i * o_i, -1, keepdims=True) + EPS) * nw
                o_ref[b, r0:r0 + sub, h * dv:(h + 1) * dv] = o_i * jax.nn.silu(gate_ref[b, r0:r0 + sub, h * dv:(h + 1) * dv])
            g_last = gcum[c - 1:c]
            s_ref[b, h] = st * jnp.exp(g_last) + _hdot(v.T, k * jnp.exp(g_last - gcum))


def _hg_scan(q_raw, f_raw, i_raw, gate, lb, norm_w, s0):
    f32 = jnp.float32
    B, L, _ = q_raw.shape
    c = min(HG_CHUNK, L)
    assert L % c == 0
    n = L // c
    H = HG_HEADS
    tok = lambda w: pl.BlockSpec((B, c, w), lambda i: (0, i, 0))
    o, st = pl.pallas_call(
        _hg_body,
        grid=(n,),
        in_specs=[tok(H * HG_DK), tok(H * HG_DK), tok(H * HG_DV), tok(H * HG_DV),
                  pl.BlockSpec((1, H * HG_DK), lambda i: (0, 0)),
                  pl.BlockSpec((1, HG_DV), lambda i: (0, 0)),
                  pl.BlockSpec((B, H, HG_DV, HG_DK), lambda i: (0, 0, 0, 0))],
        out_specs=[tok(H * HG_DV), pl.BlockSpec((B, H, HG_DV, HG_DK), lambda i: (0, 0, 0, 0))],
        out_shape=[jax.ShapeDtypeStruct((B, L, H * HG_DV), f32),
                   jax.ShapeDtypeStruct((B, H, HG_DV, HG_DK), f32)],
        compiler_params=pltpu.CompilerParams(dimension_semantics=("arbitrary",),
                                             vmem_limit_bytes=VMEM_LIMIT_BYTES),
        name="hgrn2_scan",
    )(q_raw, f_raw, i_raw, gate, lb.reshape(1, H * HG_DK), norm_w.reshape(1, HG_DV), s0.transpose(0, 1, 3, 2))
    return o, st.transpose(0, 1, 3, 2)


def _split(z, sizes):
    return jnp.split(z, np.cumsum(sizes)[:-1].tolist(), axis=-1)


def _layer_norm(x, g, b):
    mu = x.mean(-1, keepdims=True)
    var = jnp.square(x - mu).mean(-1, keepdims=True)
    return (x - mu) * lax.rsqrt(var + EPS) * g + b


def _masked_softmax(logits, mask):
    z = jnp.where(mask, logits.astype(jnp.float32), NEG)
    e = jnp.exp(z - z.max(-1, keepdims=True)) * mask
    s = e.sum(-1, keepdims=True)
    return e / jnp.where(s > 0, s, 1.0)


def _l2norm(a):
    return a * lax.rsqrt(jnp.sum(a * a, -1, keepdims=True) + EPS)


def _gated_rmsnorm(o, gate, w):
    B, L, H, dv = o.shape
    o = o * lax.rsqrt(jnp.mean(o * o, -1, keepdims=True) + EPS) * w
    return o.reshape(B, L, H * dv) * jax.nn.silu(gate)


def _t5_bucket(dist):
    n = jnp.maximum(dist, 0)
    exact = N_BUCKETS // 2
    rel = jnp.log(jnp.maximum(n, 1).astype(jnp.float32) / exact) / math.log(MAX_DISTANCE / exact)
    large = jnp.minimum(exact + (rel * (N_BUCKETS - exact)).astype(jnp.int32), N_BUCKETS - 1)
    return jnp.where(n < exact, n, large)


def _head_bias(dist, rel_bias):
    Q, K = dist.shape
    bias = rel_bias[_t5_bucket(dist)]
    return bias.reshape(Q, K, NSA_KV, NSA_GROUP).transpose(2, 3, 0, 1).astype(jnp.float32)


def _compress(rows, w1, w2, pe):
    B, T, G, dh = rows.shape
    n_chunk = T // CMP_STRIDE
    r = rows[:, :n_chunk * CMP_STRIDE].reshape(B, n_chunk, CMP_STRIDE, G, dh)
    w1h = w1.reshape(2, CMP_STRIDE, dh, CMP_HID)
    lo = jnp.einsum('bcsgd,sdh->bcgh', r, w1h[0])
    hi = jnp.einsum('bcsgd,sdh->bcgh', r, w1h[1])
    h = lo[:, :-1] + hi[:, 1:] + pe.reshape(-1) @ w1
    return jax.nn.silu(h) @ w2


def _importance_map(n_cmp, n_sb):
    c0 = jnp.arange(n_cmp) * CMP_STRIDE
    c1 = c0 + CMP_BLOCK - 1
    b0 = jnp.arange(n_sb) * SEL_BLOCK
    b1 = b0 + SEL_BLOCK - 1
    return ((c0[:, None] <= b1[None, :]) & (c1[:, None] >= b0[None, :])).astype(jnp.float32)


def _nsa_core(q, gates, t, kc, vc, cmp_end, imp_map, ks_blk, vs_blk, kw, vw, w_pos, rel_bias):
    f32 = jnp.float32
    B, Q = q.shape[:2]
    lg = jnp.einsum('bqgjd,bcgd->bgjqc', q, kc).astype(f32) + _head_bias(t[:, None] - cmp_end[None, :], rel_bias)
    p_cmp = _masked_softmax(lg, cmp_end[None, :] <= t[:, None])
    o_cmp = jnp.einsum('bgjqc,bcgd->bqgjd', p_cmp, vc)
    n_sb = ks_blk.shape[2]
    n_top = min(N_SELECT, n_sb)
    imp = jnp.einsum('bgjqc,cn->bgqn', p_cmp, imp_map)
    blk = jnp.arange(n_sb)[None, :]
    cur = (t // SEL_BLOCK)[:, None]
    forced = (blk == 0) | (blk == cur) | (blk == cur - 1)
    score = jnp.where(blk * SEL_BLOCK <= t[:, None], jnp.where(forced, jnp.inf, imp), -jnp.inf)
    top_s, top_i = lax.top_k(score, n_top)
    b_ix = jnp.arange(B)[:, None, None, None]
    g_ix = jnp.arange(NSA_KV)[None, :, None, None]
    n_keys = n_top * SEL_BLOCK
    ks = ks_blk[b_ix, g_ix, top_i].reshape(B, NSA_KV, Q, n_keys, NSA_DH)
    vs = vs_blk[b_ix, g_ix, top_i].reshape(B, NSA_KV, Q, n_keys, NSA_DH)
    s_pos = (top_i[..., None] * SEL_BLOCK + jnp.arange(SEL_BLOCK)).reshape(B, NSA_KV, Q, n_keys)
    s_ok = jnp.repeat(top_s > -jnp.inf, SEL_BLOCK, axis=-1) & (s_pos <= t[:, None])
    table = rel_bias.reshape(N_BUCKETS, NSA_KV, NSA_GROUP).transpose(1, 2, 0)
    bucket = _t5_bucket(t[:, None] - s_pos)
    onehot = (bucket[..., None] == jnp.arange(N_BUCKETS)).astype(f32)
    s_bias = jnp.einsum('bgqmn,gjn->bgjqm', onehot, table.astype(f32), precision=lax.Precision.HIGHEST)
    lg = jnp.einsum('bqgjd,bgqmd->bgjqm', q, ks).astype(f32) + s_bias
    p_sel = _masked_softmax(lg, s_ok[:, :, None])
    o_sel = jnp.einsum('bgjqm,bgqmd->bqgjd', p_sel, vs)
    w_ok = (w_pos[None, :] <= t[:, None]) & (w_pos[None, :] > t[:, None] - WINDOW) & (w_pos[None, :] >= 0)
    lg = jnp.einsum('bqgjd,bkgd->bgjqk', q, kw).astype(f32) + _head_bias(t[:, None] - w_pos[None, :], rel_bias)
    p_win = _masked_softmax(lg, w_ok)
    o_win = jnp.einsum('bgjqk,bkgd->bqgjd', p_win, vw)
    o = gates[..., 0:1] * o_cmp + gates[..., 1:2] * o_sel + gates[..., 2:3] * o_win
    return o.reshape(B, Q, NSA_HEADS * NSA_DH)


def _nsa_mixer(q_raw, kv_raw, gate_raw, past_rows, win_buf, w_cmp1, w_cmp2, pe_cmp, rel_bias):
    B, L, _ = q_raw.shape
    q = (q_raw * NSA_DH ** -0.5).reshape(B, L, NSA_KV, NSA_GROUP, NSA_DH)
    gates = jax.nn.sigmoid(gate_raw).reshape(B, L, NSA_KV, NSA_GROUP, 3)
    kv = kv_raw.reshape(B, L, N_KV_STREAMS, NSA_KV, NSA_DH)
    new_rows, new_win = kv[:, :, :N_CACHE_STREAMS], kv[:, :, N_CACHE_STREAMS:]
    pos0 = 0 if past_rows is None else past_rows.shape[1]
    rows = new_rows if past_rows is None else jnp.concatenate([past_rows, new_rows], 1)
    T = rows.shape[1]
    kc = _compress(rows[:, :, 0], w_cmp1[0], w_cmp2[0], pe_cmp[0])
    vc = _compress(rows[:, :, 1], w_cmp1[1], w_cmp2[1], pe_cmp[1])
    wrows = new_win if win_buf is None else jnp.concatenate([win_buf, new_win], 1)
    keep = min(WINDOW, pos0 + L)
    if past_rows is None:
        o = _nsa_prompt_attention(q, gates, kc, vc, rows[:, :, 2], rows[:, :, 3],
                                  new_win[:, :, 0], new_win[:, :, 1], rel_bias)
        return o, new_rows, wrows[:, -keep:]
    n_cmp = kc.shape[1]
    cmp_end = jnp.arange(n_cmp) * CMP_STRIDE + CMP_BLOCK - 1
    n_sb = -(-T // SEL_BLOCK)
    sel = jnp.pad(rows[:, :, 2:4], ((0, 0), (0, n_sb * SEL_BLOCK - T), (0, 0), (0, 0), (0, 0)))
    sel = sel.reshape(B, n_sb, SEL_BLOCK, 2, NSA_KV, NSA_DH).transpose(3, 0, 4, 1, 2, 5)
    imp_map = _importance_map(n_cmp, n_sb)
    n_buf = wrows.shape[1] - L
    o = _nsa_core(q, gates, pos0 + jnp.arange(L), kc, vc, cmp_end, imp_map, sel[0], sel[1],
                  wrows[:, :, 0], wrows[:, :, 1], pos0 - n_buf + jnp.arange(n_buf + L), rel_bias)
    return o, new_rows, wrows[:, -keep:]


def _chunk(a, c):
    B, L = a.shape[:2]
    n = -(-L // c)
    a = jnp.pad(a, [(0, 0), (0, n * c - L)] + [(0, 0)] * (a.ndim - 2))
    a = a.reshape((B, n, c) + a.shape[2:])
    return jnp.moveaxis(a, (1, 2), (0, 3))


def _unchunk(o, L):
    n, B, H, c, d = o.shape
    return jnp.moveaxis(o, (0, 3), (1, 2)).reshape(B, n * c, H, d)[:, :L]


def _gated_delta(q, k, v, beta, g, s0):
    f32 = jnp.float32
    L = k.shape[1]
    c = min(DN_CHUNK, L)
    qs, ks, vs = (_chunk(a, c) for a in (q, k, v))
    bs, gs = _chunk(beta, c), _chunk(g, c)
    incl = jnp.tril(jnp.ones((c, c), bool))
    strict = jnp.tril(jnp.ones((c, c), bool), -1)
    eye = jnp.eye(c, dtype=f32)

    def step(S, xs):
        qc, kc, vc, bc, gc = xs
        G = jnp.cumsum(gc, -1)
        dec = jnp.where(incl, jnp.exp(jnp.where(incl, G[..., :, None] - G[..., None, :], 0.0)), 0.0)
        kb = kc * bc[..., None]
        A = jnp.where(strict, jnp.einsum('bhtk,bhsk->bhts', kb, kc) * dec, 0.0)
        Tm = lax.linalg.triangular_solve(eye + A, jnp.broadcast_to(eye, A.shape), left_side=True, lower=True)
        u = Tm @ (vc * bc[..., None])
        w = Tm @ (kb * jnp.exp(G)[..., None])
        v_new = u - w @ S
        attn = jnp.einsum('bhtk,bhsk->bhts', qc, kc) * dec
        o = (qc * jnp.exp(G)[..., None]) @ S + attn @ v_new
        S = S * jnp.exp(G[..., -1])[..., None, None] + jnp.einsum(
            'bhtk,bhtv->bhkv', kc * jnp.exp(G[..., -1:] - G)[..., None], v_new)
        return S, o

    S, o = lax.scan(step, s0, (qs, ks, vs, bs, gs))
    return _unchunk(o, L), S


def _gla_scan(q, k, v, logf, s0):
    L = k.shape[1]
    c = min(HG_CHUNK, L)
    qs, ks, vs, fs = (_chunk(a, c) for a in (q, k, v, logf))
    incl = jnp.tril(jnp.ones((c, c), bool))[:, :, None]

    def step(S, xs):
        qc, kc, vc, fc = xs
        G = jnp.cumsum(fc, axis=-2)
        diff = G[:, :, :, None, :] - G[:, :, None, :, :]
        dec = jnp.where(incl, jnp.exp(jnp.where(incl, diff, 0.0)), 0.0)
        attn = jnp.einsum('bhtk,bhsk,bhtsk->bhts', qc, kc, dec)
        o = jnp.einsum('bhtk,bhkv->bhtv', qc * jnp.exp(G), S) + jnp.einsum('bhts,bhsv->bhtv', attn, vc)
        S = S * jnp.exp(G[:, :, -1])[..., None] + jnp.einsum(
            'bhsk,bhsv->bhkv', kc * jnp.exp(G[:, :, -1:] - G), vc)
        return S, o

    S, o = lax.scan(step, s0, (qs, ks, vs, fs))
    return _unchunk(o, L), S


def _short_conv(u, buf, w):
    L = u.shape[1]
    xc = jnp.concatenate([buf, u], 1)
    y = sum(xc[:, j:j + L] * w[j] for j in range(DN_CONV))
    return jax.nn.silu(y), xc[:, -(DN_CONV - 1):]


def _deltanet_mixer(qkv_raw, a_raw, b_raw, gate_raw, conv_buf, s0, conv_w, a_log, dt_bias, norm_w):
    B, L, _ = qkv_raw.shape
    if conv_buf is None:
        conv_buf = jnp.zeros((B, DN_CONV - 1, DN_CONV_DIM), qkv_raw.dtype)
    if s0 is None:
        s0 = jnp.zeros((B, DN_HEADS, DN_DK, DN_DV), qkv_raw.dtype)
    u, new_buf = _short_conv(qkv_raw, conv_buf, conv_w)
    if L % DN_CHUNK == 0:
        beta = jax.nn.sigmoid(b_raw)
        g = -jnp.exp(a_log) * jax.nn.softplus(a_raw + dt_bias)
        o, s_new = _dn_scan(u, g, beta, gate_raw, norm_w, s0)
        return o, new_buf, s_new
    q, k, v = _split(u, (DN_HEADS * DN_DK, DN_HEADS * DN_DK, DN_HEADS * DN_DV))
    q = _l2norm(q.reshape(B, L, DN_HEADS, DN_DK)) * DN_DK ** -0.5
    k = _l2norm(k.reshape(B, L, DN_HEADS, DN_DK))
    v = v.reshape(B, L, DN_HEADS, DN_DV)
    beta = jax.nn.sigmoid(b_raw)
    g = -jnp.exp(a_log) * jax.nn.softplus(a_raw + dt_bias)
    o, s_new = _gated_delta(q, k, v, beta, g, s0)
    return _gated_rmsnorm(o, gate_raw, norm_w), new_buf, s_new


def _hgrn2_mixer(q_raw, f_raw, i_raw, gate_raw, s0, lb, norm_w):
    B, L, _ = q_raw.shape
    if s0 is None:
        s0 = jnp.zeros((B, HG_HEADS, HG_DK, HG_DV), q_raw.dtype)
    if L % HG_CHUNK == 0:
        return _hg_scan(q_raw, f_raw, i_raw, gate_raw, lb, norm_w, s0)
    q = q_raw.reshape(B, L, HG_HEADS, HG_DK) * HG_DK ** -0.5
    f = lb + (1.0 - lb) * jax.nn.sigmoid(f_raw)
    logf = jnp.log(f).reshape(B, L, HG_HEADS, HG_DK)
    k = (1.0 - f).reshape(B, L, HG_HEADS, HG_DK)
    v = i_raw.reshape(B, L, HG_HEADS, HG_DV)
    o, s_new = _gla_scan(q, k, v, logf, s0)
    return _gated_rmsnorm(o, gate_raw, norm_w), s_new


def _moe(x, w_router, router_bias, wg, wu, wd, precise):
    f32 = jnp.float32
    N, D = x.shape
    eidx, gw = _router(x, w_router, router_bias, precise)
    n_assign = N * TOP_K
    e_flat = eidx.reshape(-1)
    onehot = (e_flat[:, None] == jnp.arange(N_EXPERTS)[None, :]).astype(jnp.int32)
    running = jnp.cumsum(onehot, axis=0)
    counts = running[-1]
    rank = jnp.take_along_axis(running, e_flat[:, None], axis=1)[:, 0] - 1
    padded = (counts + MOE_ROW_BLOCK - 1) // MOE_ROW_BLOCK * MOE_ROW_BLOCK
    pad_end = jnp.cumsum(padded)
    dest = ((pad_end - padded)[e_flat] + rank).astype(jnp.int32)
    n_blocks = -(-n_assign // MOE_ROW_BLOCK) + N_EXPERTS
    n_rows = n_blocks * MOE_ROW_BLOCK
    tok = jnp.full((n_rows,), N, jnp.int32).at[dest].set(jnp.arange(n_assign, dtype=jnp.int32) // TOP_K)
    wt = jnp.zeros((n_rows,), f32).at[dest].set(gw.reshape(-1))
    blk_exp = jnp.minimum(jnp.searchsorted(pad_end, jnp.arange(n_blocks) * MOE_ROW_BLOCK, side='right'),
                          N_EXPERTS - 1).astype(jnp.int32)
    x_pad = jnp.concatenate([x, jnp.zeros((1, D), x.dtype)])
    out = _expert_ffn(x_pad[tok], wt, blk_exp, wg, wu, wd, precise)
    slot = dest.reshape(N, TOP_K)
    return out[slot[:, 0]] + out[slot[:, 1]]


def _trunk_layer(x, ple, lb, past, precise, w_in, w_cmp1, w_cmp2, pe_cmp, dn_conv_w, dn_a_log, dn_dt_bias,
                 dn_norm_w, hg_norm_w, w_branch, w_out, ln1_g, ln1_b, ln2_g, ln2_b, w_router,
                 router_bias, w_exp_gate, w_exp_up, w_exp_down, w_ple, w_ple_gate, rel_bias):
    past_rows, win_buf, conv_buf, dn_state, hg_state = past
    B, L, D = x.shape
    x2 = x.reshape(B * L, D)
    mm = functools.partial(_mm, precise=precise)
    z = mm(x2, w_in).reshape(B, L, -1)
    (nsa_q, nsa_kv, nsa_g, dn_qkv, dn_a, dn_b, dn_g,
     hg_q, hg_f, hg_i, hg_g, merge) = _split(z, IN_SPLITS)
    o_a, rows_new, win_new = _nsa_mixer(nsa_q, nsa_kv, nsa_g, past_rows, win_buf, w_cmp1, w_cmp2, pe_cmp, rel_bias)
    o_b, conv_new, dn_new = _deltanet_mixer(dn_qkv, dn_a, dn_b, dn_g, conv_buf, dn_state, dn_conv_w,
                                            dn_a_log, dn_dt_bias, dn_norm_w)
    o_c, hg_new = _hgrn2_mixer(hg_q, hg_f, hg_i, hg_g, hg_state, lb, hg_norm_w)
    g = jax.nn.sigmoid(merge).reshape(B * L, 3, D)
    m = (g[:, 0] * mm(o_a.reshape(B * L, -1), w_branch[0]) + g[:, 1] * mm(o_b.reshape(B * L, -1), w_branch[1])
         + g[:, 2] * mm(o_c.reshape(B * L, -1), w_branch[2]))
    x2 = _layer_norm(DEEPNORM_ALPHA * x2 + mm(m, w_out), ln1_g, ln1_b)
    ffn = _moe(x2, w_router, router_bias, w_exp_gate, w_exp_up, w_exp_down, precise)
    ple_term = mm(ple.reshape(B * L, -1), w_ple) * jax.nn.sigmoid(mm(x2, w_ple_gate))
    x2 = _layer_norm(DEEPNORM_ALPHA * x2 + ffn + ple_term, ln2_g, ln2_b)
    return x2.reshape(B, L, D), (rows_new, win_new, conv_new, dn_new, hg_new)


def kernel(x_prompt, x_sample, cache_nsa, state_win, state_dn_conv, state_dn, state_hg, page_table,
           p_prompt, p_sample, w_in, w_cmp1, w_cmp2, pe_cmp, dn_conv_w, dn_a_log, dn_dt_bias, dn_norm_w,
           hg_lb_logits, hg_norm_w, w_branch, w_out, ln1_g, ln1_b, ln2_g, ln2_b, w_router, router_bias,
           w_exp_gate, w_exp_up, w_exp_down, w_ple, w_ple_gate, rel_bias):
    lb = jnp.cumsum(jax.nn.softmax(hg_lb_logits, axis=0), axis=0)
    lb = lb - lb[0:1]
    n_dec = page_table.shape[0]
    y_prompt, y_sample = x_prompt, x_sample
    st_p, st_s = [], []
    for l in range(DEPTH):
        lw = (w_in[l], w_cmp1[l], w_cmp2[l], pe_cmp[l], dn_conv_w[l], dn_a_log[l], dn_dt_bias[l],
              dn_norm_w[l], hg_norm_w[l], w_branch[l], w_out[l], ln1_g[l], ln1_b[l], ln2_g[l], ln2_b[l],
              w_router, router_bias, w_exp_gate[l], w_exp_up[l], w_exp_down[l], w_ple[l], w_ple_gate[l],
              rel_bias)
        y_prompt, s = _trunk_layer(y_prompt, p_prompt[l], lb[l], (None, None, None, None, None), False, *lw)
        st_p.append(s)
        pages = jnp.take(cache_nsa[l].reshape(cache_nsa.shape[1], -1), page_table.reshape(-1), axis=0)
        past_rows = pages.reshape(n_dec, -1, N_CACHE_STREAMS, NSA_KV, NSA_DH)
        y_sample, s = _trunk_layer(y_sample, p_sample[l], lb[l],
                                   (past_rows, state_win[l], state_dn_conv[l], state_dn[l], state_hg[l]),
                                   True, *lw)
        st_s.append(s)
    outs = [y_prompt, y_sample]
    for i in range(5):
        outs.append(jnp.stack([s[i] for s in st_p]))
        outs.append(jnp.stack([s[i] for s in st_s]))
    return tuple(outs)
```

```python
import functools
import math

import jax
import jax.numpy as jnp
import numpy as np
from jax import lax
from jax.experimental import pallas as pl
from jax.experimental.pallas import tpu as pltpu

D_MODEL = 1024
DEPTH = 4
NSA_HEADS = 8
NSA_KV = 2
NSA_GROUP = NSA_HEADS // NSA_KV
NSA_DH = 64
CMP_STRIDE = 16
CMP_BLOCK = 2 * CMP_STRIDE
CMP_HID = 128
SEL_BLOCK = 64
N_SELECT = 16
WINDOW = 512
Q_BLOCK = 128
N_KV_STREAMS = 6
N_CACHE_STREAMS = 4
N_BUCKETS = 32
MAX_DISTANCE = 128
DN_HEADS = 4
DN_DK = 128
DN_DV = 128
DN_CONV = 4
DN_CHUNK = 64
DN_CONV_DIM = DN_HEADS * (2 * DN_DK + DN_DV)
HG_HEADS = 4
HG_DK = 128
HG_DV = 128
HG_CHUNK = 64
N_EXPERTS = 16
N_EXPERT_GROUPS = 4
EXPERTS_PER_GROUP = N_EXPERTS // N_EXPERT_GROUPS
TOP_K = 2
D_EXPERT = 512
MOE_ROW_BLOCK = 128
DEEPNORM_ALPHA = (2 * DEPTH) ** 0.25
EPS = 1e-6
NEG = -1e30
IN_SPLITS = (NSA_HEADS * NSA_DH, N_KV_STREAMS * NSA_KV * NSA_DH, 3 * NSA_HEADS,
             DN_CONV_DIM, DN_HEADS, DN_HEADS, DN_HEADS * DN_DV,
             HG_HEADS * HG_DK, HG_HEADS * HG_DK, HG_HEADS * HG_DV, HG_HEADS * HG_DV,
             3 * D_MODEL)

VMEM_LIMIT_BYTES = 48 * 1024 * 1024
MM_ROW_TILE = 1024
MM_COL_TILE = 512


def _mm_body(x_ref, w_ref, o_ref, *, precise):
    if precise:
        o_ref[...] = jnp.dot(x_ref[...], w_ref[...], preferred_element_type=jnp.float32,
                             precision=lax.Precision.HIGHEST)
    else:
        o_ref[...] = jnp.dot(x_ref[...].astype(jnp.bfloat16), w_ref[...].astype(jnp.bfloat16),
                             preferred_element_type=jnp.float32)


def _mm(x, w, precise=False):
    M, K = x.shape
    N = w.shape[1]
    tm = min(MM_ROW_TILE, M)
    tn = min(MM_COL_TILE, N)
    return pl.pallas_call(
        functools.partial(_mm_body, precise=precise),
        grid=(pl.cdiv(M, tm), pl.cdiv(N, tn)),
        in_specs=[pl.BlockSpec((tm, K), lambda i, j: (i, 0)),
                  pl.BlockSpec((K, tn), lambda i, j: (0, j))],
        out_specs=pl.BlockSpec((tm, tn), lambda i, j: (i, j)),
        out_shape=jax.ShapeDtypeStruct((M, N), jnp.float32),
        compiler_params=pltpu.CompilerParams(
            dimension_semantics=("parallel", "parallel"),
            vmem_limit_bytes=VMEM_LIMIT_BYTES),
        name="mm",
    )(x, w)


def _expert_body(blk_exp_ref, h_ref, wt_ref, wg_ref, wu_ref, wd_ref, o_ref, *, precise):
    del blk_exp_ref
    if precise:
        dot = functools.partial(jnp.dot, preferred_element_type=jnp.float32,
                                precision=lax.Precision.HIGHEST)
        h = h_ref[...]
        a = jax.nn.silu(dot(h, wg_ref[0])) * dot(h, wu_ref[0])
        o_ref[...] = dot(a, wd_ref[0]) * wt_ref[...]
    else:
        bf = jnp.bfloat16
        dot = functools.partial(jnp.dot, preferred_element_type=jnp.float32)
        h = h_ref[...].astype(bf)
        a = jax.nn.silu(dot(h, wg_ref[0].astype(bf))) * dot(h, wu_ref[0].astype(bf))
        o_ref[...] = dot(a.astype(bf), wd_ref[0].astype(bf)) * wt_ref[...]


def _expert_ffn(h, wt, blk_exp, wg, wu, wd, precise):
    n_rows, D = h.shape
    n_blocks = n_rows // MOE_ROW_BLOCK
    grid_spec = pltpu.PrefetchScalarGridSpec(
        num_scalar_prefetch=1,
        grid=(n_blocks,),
        in_specs=[pl.BlockSpec((MOE_ROW_BLOCK, D), lambda i, be: (i, 0)),
                  pl.BlockSpec((MOE_ROW_BLOCK, 1), lambda i, be: (i, 0)),
                  pl.BlockSpec((1, D, D_EXPERT), lambda i, be: (be[i], 0, 0)),
                  pl.BlockSpec((1, D, D_EXPERT), lambda i, be: (be[i], 0, 0)),
                  pl.BlockSpec((1, D_EXPERT, D), lambda i, be: (be[i], 0, 0))],
        out_specs=pl.BlockSpec((MOE_ROW_BLOCK, D), lambda i, be: (i, 0)),
    )
    return pl.pallas_call(
        functools.partial(_expert_body, precise=precise),
        grid_spec=grid_spec,
        out_shape=jax.ShapeDtypeStruct((n_rows, D), jnp.float32),
        compiler_params=pltpu.CompilerParams(
            dimension_semantics=("arbitrary",),
            vmem_limit_bytes=VMEM_LIMIT_BYTES),
        name="expert_ffn",
    )(blk_exp, h, wt.reshape(n_rows, 1), wg, wu, wd)


NSA_ROWS = NSA_GROUP * Q_BLOCK
NEAR_CMP = 32
NEAR_CMP_BACK = 16
CMP_PER_QBLOCK = Q_BLOCK // CMP_STRIDE
SEL_PER_TILE = Q_BLOCK // SEL_BLOCK
WIN_TILES = WINDOW // Q_BLOCK
MXU_DTYPE = jnp.bfloat16


def _nsa_prompt_body(q_ref, gt_ref, kct_ref, kc_ref, vc_ref, kst_ref, vs_ref, kwt_ref, vw_ref,
                     band_ref, near_ref, far_ref, imap_ref, o_ref, m_scr, l_scr, acc_scr):
    f32 = jnp.float32
    R, QB = NSA_ROWS, Q_BLOCK
    i = pl.program_id(2)
    n_cmp_pad = kct_ref.shape[-1]
    n_sb = imap_ref.shape[-1]
    q = q_ref[0, 0, 0]
    far_bias = far_ref[0]
    r_row = lax.broadcasted_iota(jnp.int32, (R, 1), 0) & (QB - 1)
    t_row = i * QB + r_row

    def mxu(a, b):
        return jnp.dot(a.astype(MXU_DTYPE), b.astype(MXU_DTYPE), preferred_element_type=f32)

    def mxu_nt(a, b):
        return lax.dot_general(a.astype(MXU_DTYPE), b.astype(MXU_DTYPE), (((1,), (1,)), ((), ())),
                               preferred_element_type=f32)

    c0 = CMP_PER_QBLOCK * i - NEAR_CMP_BACK
    lg_far = mxu(q, kct_ref[0, 0]) + far_bias
    mk_far = lax.broadcasted_iota(jnp.int32, (R, n_cmp_pad), 1) < c0
    near_start = pl.multiple_of(CMP_PER_QBLOCK * i, 8)
    k_near = kc_ref[0, 0, pl.ds(near_start, NEAR_CMP), :]
    v_near = vc_ref[0, 0, pl.ds(near_start, NEAR_CMP), :]
    lg_near = mxu_nt(q, k_near) + near_ref[0]
    c_near = c0 + lax.broadcasted_iota(jnp.int32, (R, NEAR_CMP), 1)
    mk_near = (c_near >= 0) & (c_near * CMP_STRIDE + (CMP_BLOCK - 1) <= t_row)
    z_far = jnp.where(mk_far, lg_far, NEG)
    z_near = jnp.where(mk_near, lg_near, NEG)
    mx = jnp.maximum(z_far.max(-1, keepdims=True), z_near.max(-1, keepdims=True))
    e_far = jnp.where(mk_far, jnp.exp(z_far - mx), 0.0)
    e_near = jnp.where(mk_near, jnp.exp(z_near - mx), 0.0)
    s = e_far.sum(-1, keepdims=True) + e_near.sum(-1, keepdims=True)
    s = jnp.where(s > 0, s, 1.0)
    p_far = e_far / s
    p_near = e_near / s
    o_cmp = (mxu(p_far, vc_ref[0, 0, NEAR_CMP_BACK:NEAR_CMP_BACK + n_cmp_pad, :]) + mxu(p_near, v_near))

    cn = c0 + lax.broadcasted_iota(jnp.int32, (NEAR_CMP, n_sb), 0)
    nn = lax.broadcasted_iota(jnp.int32, (NEAR_CMP, n_sb), 1)
    blocks_per_sel = SEL_BLOCK // CMP_STRIDE
    imap_near = ((cn >= blocks_per_sel * nn - 1) & (cn <= blocks_per_sel * nn + blocks_per_sel - 1)).astype(f32)
    imp_rows = mxu(p_far, imap_ref[...]) + mxu(p_near, imap_near)
    imp = imp_rows[0:QB]
    for j in range(1, NSA_GROUP):
        imp = imp + imp_rows[j * QB:(j + 1) * QB]
    n_io = lax.broadcasted_iota(jnp.int32, (QB, n_sb), 1)
    n_f = n_io.astype(f32)
    r_q = lax.broadcasted_iota(jnp.int32, (QB, n_sb), 0)
    t_q = i * QB + r_q
    cur = SEL_PER_TILE * i + (r_q >= SEL_BLOCK).astype(jnp.int32)
    forced = (n_io == 0) | (n_io == cur) | (n_io == cur - 1)
    score = jnp.where(n_io * SEL_BLOCK <= t_q, jnp.where(forced, jnp.inf, imp), -jnp.inf)

    def pick_one(_, carry):
        score, sel = carry
        top = score.max(-1, keepdims=True)
        first = jnp.where(score == top, n_f, float(n_sb)).min(-1, keepdims=True)
        hit = n_f == first
        sel = jnp.where(hit & (top > -jnp.inf), 1.0, sel)
        return jnp.where(hit, -jnp.inf, score), sel

    _, sel = lax.fori_loop(0, min(N_SELECT, n_sb), pick_one, (score, jnp.zeros((QB, n_sb), f32)))

    r_t = lax.broadcasted_iota(jnp.int32, (R, QB), 0) & (QB - 1)
    c_t = lax.broadcasted_iota(jnp.int32, (R, QB), 1)

    def reset():
        m_scr[...] = jnp.full((R, 1), NEG, f32)
        l_scr[...] = jnp.zeros((R, 1), f32)
        acc_scr[...] = jnp.zeros_like(acc_scr)

    def update(lg, mk, v):
        z = jnp.where(mk, lg, NEG) if mk is not None else lg
        m_old = m_scr[...]
        m_new = jnp.maximum(m_old, z.max(-1, keepdims=True))
        p = jnp.exp(z - m_new)
        if mk is not None:
            p = jnp.where(mk, p, 0.0)
        alpha = jnp.exp(m_old - m_new)
        l_scr[...] = alpha * l_scr[...] + p.sum(-1, keepdims=True)
        acc_scr[...] = alpha * acc_scr[...] + mxu(p, v)
        m_scr[...] = m_new

    def result():
        l = l_scr[...]
        return acc_scr[...] / jnp.where(l > 0, l, 1.0)

    def tile(kt_ref, v_ref, kt):
        start = pl.multiple_of(kt * QB, QB)
        return mxu(q, kt_ref[0, 0, :, pl.ds(start, QB)]), v_ref[0, 0, pl.ds(start, QB), :]

    def sel_mask(kt):
        blk = SEL_PER_TILE * kt + (lax.broadcasted_iota(jnp.int32, (n_sb, QB), 1) >= SEL_BLOCK).astype(jnp.int32)
        expand = (lax.broadcasted_iota(jnp.int32, (n_sb, QB), 0) == blk).astype(f32)
        mq = mxu(sel, expand)
        return jnp.concatenate([mq] * NSA_GROUP, axis=0) > 0.5

    reset()

    def sel_far(kt, carry):
        lg, v = tile(kst_ref, vs_ref, kt)
        update(lg + far_bias, sel_mask(kt), v)
        return carry

    lax.fori_loop(0, jnp.maximum(i - 1, 0), sel_far, 0)

    @pl.when(i >= 1)
    def _():
        lg, v = tile(kst_ref, vs_ref, i - 1)
        update(lg + band_ref[0, 1], sel_mask(i - 1), v)

    lg, v = tile(kst_ref, vs_ref, i)
    update(lg + band_ref[0, 0], sel_mask(i) & (c_t <= r_t), v)
    o_sel = result()

    reset()

    @pl.when(i >= WIN_TILES)
    def _():
        lg, v = tile(kwt_ref, vw_ref, i - WIN_TILES)
        update(lg + far_bias, c_t > r_t, v)

    for back in range(WIN_TILES - 1, 1, -1):
        @pl.when(i >= back)
        def _(back=back):
            lg, v = tile(kwt_ref, vw_ref, i - back)
            update(lg + far_bias, None, v)

    @pl.when(i >= 1)
    def _():
        lg, v = tile(kwt_ref, vw_ref, i - 1)
        update(lg + band_ref[0, 1], None, v)

    lg, v = tile(kwt_ref, vw_ref, i)
    update(lg + band_ref[0, 0], c_t <= r_t, v)
    o_win = result()

    gt = gt_ref[0, 0, 0]
    o_ref[0, 0, 0] = gt[:, 0:1] * o_cmp + gt[:, 1:2] * o_sel + gt[:, 2:3] * o_win


def _nsa_prompt_attention(q, gates, kc, vc, ksel, vsel, kwin, vwin, rel_bias):
    f32 = jnp.float32
    B, L = q.shape[:2]
    G, J, dh, QB, R = NSA_KV, NSA_GROUP, NSA_DH, Q_BLOCK, NSA_ROWS
    assert L % QB == 0 and WINDOW % QB == 0 and QB % SEL_BLOCK == 0
    nq = L // QB
    n_cmp = kc.shape[1]
    n_cmp_pad = L // CMP_STRIDE
    n_sb = L // SEL_BLOCK
    assert n_cmp == n_cmp_pad - 1

    def rows(a):
        x = a.shape[-1]
        return a.reshape(B, nq, QB, G, J, x).transpose(0, 3, 1, 4, 2, 5).reshape(B, G, nq, R, x)

    q_r = rows(q).astype(MXU_DTYPE)
    gt_r = rows(gates)
    kc_p = jnp.pad(kc, ((0, 0), (NEAR_CMP_BACK, n_cmp_pad - n_cmp + NEAR_CMP - NEAR_CMP_BACK), (0, 0), (0, 0)))
    vc_p = jnp.pad(vc, ((0, 0), (NEAR_CMP_BACK, n_cmp_pad - n_cmp + NEAR_CMP - NEAR_CMP_BACK), (0, 0), (0, 0)))
    kc_p = kc_p.transpose(0, 2, 1, 3)
    vc_p = vc_p.transpose(0, 2, 1, 3)
    kct = kc_p[:, :, NEAR_CMP_BACK:NEAR_CMP_BACK + n_cmp_pad].transpose(0, 1, 3, 2).astype(MXU_DTYPE)
    kst = ksel.transpose(0, 2, 3, 1).astype(MXU_DTYPE)
    kwt = kwin.transpose(0, 2, 3, 1).astype(MXU_DTYPE)
    vs = vsel.transpose(0, 2, 1, 3).astype(MXU_DTYPE)
    vw = vwin.transpose(0, 2, 1, 3).astype(MXU_DTYPE)

    n_dist = MAX_DISTANCE
    by_dist = rel_bias[_t5_bucket(jnp.arange(n_dist))].astype(f32).reshape(n_dist, G, J)
    by_dist = by_dist.transpose(1, 2, 0)

    def table(dist):
        d = jnp.clip(dist, 0, n_dist - 1)
        return by_dist[:, :, d].reshape(G, R, dist.shape[1])

    r = jnp.arange(QB)[:, None]
    c = jnp.arange(QB)[None, :]
    band = jnp.stack([table(r - c), table(r - c + QB)], axis=1)
    u = jnp.arange(NEAR_CMP)[None, :]
    near = table(r - CMP_STRIDE * (u - NEAR_CMP_BACK) - (CMP_BLOCK - 1))
    far = jnp.repeat(by_dist[:, :, n_dist - 1], QB, axis=1).reshape(G, R, 1)
    imap = _importance_map(n_cmp_pad, n_sb)

    kv_spec = lambda shape: pl.BlockSpec((1, 1) + shape, lambda b, g, i: (b, g, 0, 0))
    out = pl.pallas_call(
        _nsa_prompt_body,
        grid=(B, G, nq),
        in_specs=[pl.BlockSpec((1, 1, 1, R, dh), lambda b, g, i: (b, g, i, 0, 0)),
                  pl.BlockSpec((1, 1, 1, R, 3), lambda b, g, i: (b, g, i, 0, 0)),
                  kv_spec((dh, n_cmp_pad)),
                  kv_spec((n_cmp_pad + NEAR_CMP, dh)),
                  kv_spec((n_cmp_pad + NEAR_CMP, dh)),
                  kv_spec((dh, L)), kv_spec((L, dh)), kv_spec((dh, L)), kv_spec((L, dh)),
                  pl.BlockSpec((1, 2, R, QB), lambda b, g, i: (g, 0, 0, 0)),
                  pl.BlockSpec((1, R, NEAR_CMP), lambda b, g, i: (g, 0, 0)),
                  pl.BlockSpec((1, R, 1), lambda b, g, i: (g, 0, 0)),
                  pl.BlockSpec((n_cmp_pad, n_sb), lambda b, g, i: (0, 0))],
        out_specs=pl.BlockSpec((1, 1, 1, R, dh), lambda b, g, i: (b, g, i, 0, 0)),
        out_shape=jax.ShapeDtypeStruct((B, G, nq, R, dh), f32),
        scratch_shapes=[pltpu.VMEM((R, 1), f32), pltpu.VMEM((R, 1), f32), pltpu.VMEM((R, dh), f32)],
        compiler_params=pltpu.CompilerParams(
            dimension_semantics=("parallel", "parallel", "arbitrary"),
            vmem_limit_bytes=VMEM_LIMIT_BYTES),
        name="nsa_prompt",
    )(q_r, gt_r, kct, kc_p, vc_p, kst, vs, kwt, vw, band, near, far, imap)
    return out.reshape(B, G, nq, J, QB, dh).transpose(0, 2, 4, 1, 3, 5).reshape(B, L, G * J * dh)


MASK_OFF = -1e30
FAR_TILES = 4
SEL_SHIFT = SEL_BLOCK.bit_length() - 1
QB_SHIFT = Q_BLOCK.bit_length() - 1
assert 1 << SEL_SHIFT == SEL_BLOCK and 1 << QB_SHIFT == Q_BLOCK


def _nsa_prompt_body_t(q_ref, gt_ref, kc_ref, vct_ref, vc_ref, ks_ref, vst_ref, kw_ref, vwt_ref,
                       band_ref, near_ref, far_ref, winb_ref, imapt_ref, o_ref, m_scr, l_scr, acc_scr):
    f32 = jnp.float32
    R, QB = NSA_ROWS, Q_BLOCK
    i = pl.program_id(2)
    n_cmp_pad = vct_ref.shape[-1]
    n_sb = imapt_ref.shape[0]
    qt = q_ref[0, 0, 0]
    far_bias = far_ref[0]
    r_lane = lax.broadcasted_iota(jnp.int32, (1, R), 1) & (QB - 1)
    t_lane = i * QB + r_lane

    def mxu(a, b):
        return jnp.dot(a.astype(MXU_DTYPE), b.astype(MXU_DTYPE), preferred_element_type=f32)

    c0 = CMP_PER_QBLOCK * i - NEAR_CMP_BACK
    kc_far = kc_ref[0, 0, NEAR_CMP_BACK:NEAR_CMP_BACK + n_cmp_pad, :]
    lg_far = mxu(kc_far, qt) + far_bias
    mk_far = lax.broadcasted_iota(jnp.int32, (n_cmp_pad, R), 0) < c0
    near_start = pl.multiple_of(CMP_PER_QBLOCK * i, 8)
    k_near = kc_ref[0, 0, pl.ds(near_start, NEAR_CMP), :]
    v_near = vc_ref[0, 0, pl.ds(near_start, NEAR_CMP), :]
    lg_near = mxu(k_near, qt) + near_ref[0]
    c_near = c0 + lax.broadcasted_iota(jnp.int32, (NEAR_CMP, R), 0)
    mk_near = (c_near >= 0) & (c_near * CMP_STRIDE + (CMP_BLOCK - 1) <= t_lane)
    z_far = jnp.where(mk_far, lg_far, NEG)
    z_near = jnp.where(mk_near, lg_near, NEG)
    mx = jnp.maximum(z_far.max(0, keepdims=True), z_near.max(0, keepdims=True))
    e_far = jnp.where(mk_far, jnp.exp(z_far - mx), 0.0)
    e_near = jnp.where(mk_near, jnp.exp(z_near - mx), 0.0)
    s = e_far.sum(0, keepdims=True) + e_near.sum(0, keepdims=True)
    inv = 1.0 / jnp.where(s > 0, s, 1.0)
    p_far = (e_far * inv).astype(MXU_DTYPE)
    p_near = (e_near * inv).astype(MXU_DTYPE)
    o_cmp = mxu(vct_ref[0, 0], p_far) + mxu(v_near.T, p_near)

    nn = lax.broadcasted_iota(jnp.int32, (n_sb, NEAR_CMP), 0)
    cn = c0 + lax.broadcasted_iota(jnp.int32, (n_sb, NEAR_CMP), 1)
    per_sel = SEL_BLOCK // CMP_STRIDE
    imapt_near = ((cn >= per_sel * nn - 1) & (cn <= per_sel * nn + per_sel - 1)).astype(f32)
    imp_rows = mxu(imapt_ref[...], p_far) + mxu(imapt_near, p_near)
    imp = imp_rows[:, 0:QB]
    for j in range(1, NSA_GROUP):
        imp = imp + imp_rows[:, j * QB:(j + 1) * QB]
    n_io = lax.broadcasted_iota(jnp.int32, (n_sb, QB), 0)
    n_f = n_io.astype(f32)
    r_q = lax.broadcasted_iota(jnp.int32, (n_sb, QB), 1)
    t_q = i * QB + r_q
    cur = SEL_PER_TILE * i + (r_q >= SEL_BLOCK).astype(jnp.int32)
    forced = (n_io == 0) | (n_io == cur) | (n_io == cur - 1)
    score = jnp.where(n_io * SEL_BLOCK <= t_q, jnp.where(forced, jnp.inf, imp), -jnp.inf)

    def pick_one(_, carry):
        score, sel = carry
        top = score.max(0, keepdims=True)
        first = jnp.where(score == top, n_f, float(n_sb)).min(0, keepdims=True)
        hit = n_f == first
        sel = jnp.where(hit & (top > -jnp.inf), 1.0, sel)
        return jnp.where(hit, -jnp.inf, score), sel

    _, sel = lax.fori_loop(0, min(N_SELECT, n_sb), pick_one, (score, jnp.zeros((n_sb, QB), f32)))
    sel = sel.astype(MXU_DTYPE)

    c_t = lax.broadcasted_iota(jnp.int32, (QB, R), 0)
    r_t = lax.broadcasted_iota(jnp.int32, (QB, R), 1) & (QB - 1)
    causal_off = jnp.where(c_t <= r_t, 0.0, MASK_OFF)

    def reset():
        m_scr[...] = jnp.full((1, R), NEG, f32)
        l_scr[...] = jnp.zeros((1, R), f32)
        acc_scr[...] = jnp.zeros_like(acc_scr)

    def update(z, vt):
        m_old = m_scr[...]
        m_new = jnp.maximum(m_old, z.max(0, keepdims=True))
        p = jnp.exp(z - m_new)
        alpha = jnp.exp(m_old - m_new)
        l_scr[...] = alpha * l_scr[...] + p.sum(0, keepdims=True)
        acc_scr[...] = alpha * acc_scr[...] + mxu(vt, p)
        m_scr[...] = m_new

    def result():
        return acc_scr[...] * (1.0 / l_scr[...])

    def tile(k_ref, vt_ref, kt):
        start = pl.multiple_of(kt * QB, QB)
        return mxu(k_ref[0, 0, pl.ds(start, QB), :], qt), vt_ref[0, 0, :, pl.ds(start, QB)]

    def sel_off(kt):
        blk = SEL_PER_TILE * kt + (lax.broadcasted_iota(jnp.int32, (QB, n_sb), 0) >= SEL_BLOCK).astype(jnp.int32)
        expand = (lax.broadcasted_iota(jnp.int32, (QB, n_sb), 1) == blk).astype(MXU_DTYPE)
        off = (mxu(expand, sel) - 1.0) * (-MASK_OFF)
        return jnp.concatenate([off] * NSA_GROUP, axis=1)

    reset()
    lg, vt = tile(ks_ref, vst_ref, i)
    update(lg + band_ref[0, 0] + (sel_off(i) + causal_off), vt)

    @pl.when(i >= 1)
    def _():
        lg, vt = tile(ks_ref, vst_ref, i - 1)
        update(lg + band_ref[0, 1] + sel_off(i - 1), vt)

    KF = FAR_TILES * QB
    key_f = lax.broadcasted_iota(jnp.int32, (KF, n_sb), 0)
    n_f2 = lax.broadcasted_iota(jnp.int32, (KF, n_sb), 1)
    tile_f = lax.broadcasted_iota(jnp.int32, (KF, 1), 0) >> QB_SHIFT

    def sel_far(it, carry):
        start = pl.multiple_of(it * KF, KF)
        lg = mxu(ks_ref[0, 0, pl.ds(start, KF), :], qt)
        expand = (n_f2 == it * (KF // SEL_BLOCK) + (key_f >> SEL_SHIFT)).astype(MXU_DTYPE)
        off = (mxu(expand, sel) - 1.0) * (-MASK_OFF)
        off = off + jnp.where(it * FAR_TILES + tile_f < i - 1, 0.0, MASK_OFF)
        update(lg + far_bias + jnp.concatenate([off] * NSA_GROUP, axis=1), vst_ref[0, 0, :, pl.ds(start, KF)])
        return carry

    lax.fori_loop(0, (jnp.maximum(i - 1, 0) + FAR_TILES - 1) // FAR_TILES, sel_far, 0)
    o_sel = result()

    reset()
    lg, vt = tile(kw_ref, vwt_ref, i)
    update(lg + band_ref[0, 0] + causal_off, vt)

    @pl.when(i >= WIN_TILES)
    def _():
        start = pl.multiple_of((i - WIN_TILES) * QB, QB)
        lg = mxu(kw_ref[0, 0, pl.ds(start, WINDOW), :], qt)
        update(lg + winb_ref[0], vwt_ref[0, 0, :, pl.ds(start, WINDOW)])

    @pl.when((i >= 1) & (i < WIN_TILES))
    def _():
        lg, vt = tile(kw_ref, vwt_ref, i - 1)
        update(lg + band_ref[0, 1], vt)

    for back in range(2, WIN_TILES):
        @pl.when((i >= back) & (i < WIN_TILES))
        def _(back=back):
            lg, vt = tile(kw_ref, vwt_ref, i - back)
            update(lg + far_bias, vt)

    o_win = result()
    gt = gt_ref[0, 0, 0]
    o_ref[0, 0, 0] = gt[0:1] * o_cmp + gt[1:2] * o_sel + gt[2:3] * o_win


def _nsa_prompt_attention_t(q, gates, kc, vc, ksel, vsel, kwin, vwin, rel_bias):
    f32 = jnp.float32
    B, L = q.shape[:2]
    G, J, dh, QB, R = NSA_KV, NSA_GROUP, NSA_DH, Q_BLOCK, NSA_ROWS
    assert L % QB == 0 and WINDOW % QB == 0 and QB % SEL_BLOCK == 0
    nq = L // QB
    n_cmp = kc.shape[1]
    n_cmp_pad = L // CMP_STRIDE
    n_sb = L // SEL_BLOCK
    assert n_cmp == n_cmp_pad - 1

    def cols(a):
        x = a.shape[-1]
        return a.reshape(B, nq, QB, G, J, x).transpose(0, 3, 1, 5, 4, 2).reshape(B, G, nq, x, R)

    q_c = cols(q).astype(MXU_DTYPE)
    gt_c = cols(gates)
    pad = ((0, 0), (NEAR_CMP_BACK, n_cmp_pad - n_cmp + NEAR_CMP - NEAR_CMP_BACK), (0, 0), (0, 0))
    kc_p = jnp.pad(kc, pad).transpose(0, 2, 1, 3)
    vc_p = jnp.pad(vc, pad).transpose(0, 2, 1, 3)
    vct = vc_p[:, :, NEAR_CMP_BACK:NEAR_CMP_BACK + n_cmp_pad].transpose(0, 1, 3, 2).astype(MXU_DTYPE)
    ks = ksel.transpose(0, 2, 1, 3).astype(MXU_DTYPE)
    kw = kwin.transpose(0, 2, 1, 3).astype(MXU_DTYPE)
    vst = vsel.transpose(0, 2, 3, 1).astype(MXU_DTYPE)
    vwt = vwin.transpose(0, 2, 3, 1).astype(MXU_DTYPE)

    n_dist = MAX_DISTANCE
    by_dist = rel_bias[_t5_bucket(jnp.arange(n_dist))].astype(f32).reshape(n_dist, G, J)
    by_dist = by_dist.transpose(1, 2, 0)

    def table(dist):
        d = jnp.clip(dist, 0, n_dist - 1)
        return by_dist[:, :, d].transpose(0, 2, 1, 3).reshape(G, dist.shape[0], R)

    c = jnp.arange(QB)[:, None]
    r = jnp.arange(QB)[None, :]
    band = jnp.stack([table(r - c), table(r - c + QB)], axis=1)
    u = jnp.arange(NEAR_CMP)[:, None]
    near = table(r - CMP_STRIDE * (u - NEAR_CMP_BACK) - (CMP_BLOCK - 1))
    far = jnp.repeat(by_dist[:, :, n_dist - 1], QB, axis=1).reshape(G, 1, R)
    assert nq % FAR_TILES == 0
    edge = jnp.tile(jnp.where(c > r, 0.0, MASK_OFF), (1, J))[None] + far
    winb = jnp.concatenate([edge] + [jnp.broadcast_to(far, (G, QB, R))] * (WIN_TILES - 2) + [band[:, 1]], axis=1)
    imapt = _importance_map(n_cmp_pad, n_sb).T

    kv_spec = lambda shape: pl.BlockSpec((1, 1) + shape, lambda b, g, i: (b, g, 0, 0))
    out = pl.pallas_call(
        _nsa_prompt_body_t,
        grid=(B, G, nq),
        in_specs=[pl.BlockSpec((1, 1, 1, dh, R), lambda b, g, i: (b, g, i, 0, 0)),
                  pl.BlockSpec((1, 1, 1, 3, R), lambda b, g, i: (b, g, i, 0, 0)),
                  kv_spec((n_cmp_pad + NEAR_CMP, dh)),
                  kv_spec((dh, n_cmp_pad)),
                  kv_spec((n_cmp_pad + NEAR_CMP, dh)),
                  kv_spec((L, dh)), kv_spec((dh, L)), kv_spec((L, dh)), kv_spec((dh, L)),
                  pl.BlockSpec((1, 2, QB, R), lambda b, g, i: (g, 0, 0, 0)),
                  pl.BlockSpec((1, NEAR_CMP, R), lambda b, g, i: (g, 0, 0)),
                  pl.BlockSpec((1, 1, R), lambda b, g, i: (g, 0, 0)),
                  pl.BlockSpec((1, WINDOW, R), lambda b, g, i: (g, 0, 0)),
                  pl.BlockSpec((n_sb, n_cmp_pad), lambda b, g, i: (0, 0))],
        out_specs=pl.BlockSpec((1, 1, 1, dh, R), lambda b, g, i: (b, g, i, 0, 0)),
        out_shape=jax.ShapeDtypeStruct((B, G, nq, dh, R), f32),
        scratch_shapes=[pltpu.VMEM((1, R), f32), pltpu.VMEM((1, R), f32), pltpu.VMEM((dh, R), f32)],
        compiler_params=pltpu.CompilerParams(
            dimension_semantics=("parallel", "parallel", "arbitrary"),
            vmem_limit_bytes=VMEM_LIMIT_BYTES),
        name="nsa_prompt",
    )(q_c, gt_c, kc_p, vct, vc_p, ks, vst, kw, vwt, band, near, far, winb, imapt)
    return out.reshape(B, G, nq, dh, J, QB).transpose(0, 2, 5, 1, 4, 3).reshape(B, L, G * J * dh)


def _hdot(a, b):
    return jnp.dot(a, b, preferred_element_type=jnp.float32, precision=lax.Precision.HIGHEST)


def _hdot_nt(a, b):
    return lax.dot_general(a, b, (((1,), (1,)), ((), ())), preferred_element_type=jnp.float32,
                           precision=lax.Precision.HIGHEST)


def _bdot(a, b):
    return jnp.dot(a.astype(MXU_DTYPE), b.astype(MXU_DTYPE), preferred_element_type=jnp.float32)


def _bdot_nt(a, b):
    return lax.dot_general(a.astype(MXU_DTYPE), b.astype(MXU_DTYPE), (((1,), (1,)), ((), ())),
                           preferred_element_type=jnp.float32)


ROUTER_OUT_LANES = 128


def _router_body(x_ref, w_ref, b_ref, o_ref, *, precise):
    f32 = jnp.float32
    if precise:
        logits = _hdot(x_ref[...], w_ref[...])
    else:
        logits = jnp.dot(x_ref[...].astype(jnp.bfloat16), w_ref[...].astype(jnp.bfloat16),
                         preferred_element_type=f32)
    scores = jax.nn.sigmoid(logits)
    biased = scores + b_ref[...]
    tm = scores.shape[0]
    lane = lax.broadcasted_iota(jnp.int32, (tm, N_EXPERTS), 1).astype(f32)
    e_io = lax.broadcasted_iota(jnp.int32, (tm, N_EXPERTS), 1)
    grp = sum((e_io >= gi * EXPERTS_PER_GROUP).astype(jnp.int32) for gi in range(1, N_EXPERT_GROUPS))

    def top2(vals):
        m1 = vals.max(-1, keepdims=True)
        i1 = jnp.where(vals == m1, lane, float(N_EXPERTS)).min(-1, keepdims=True)
        rest = jnp.where(lane == i1, -jnp.inf, vals)
        m2 = rest.max(-1, keepdims=True)
        i2 = jnp.where(rest == m2, lane, float(N_EXPERTS)).min(-1, keepdims=True)
        return m1, i1, m2, i2

    best = None
    gsel = None
    for gi in range(N_EXPERT_GROUPS):
        m1, _, m2, _ = top2(jnp.where(grp == gi, biased, -jnp.inf))
        gsum = m1 + m2
        if best is None:
            best, gsel = gsum, jnp.zeros_like(gsum, dtype=jnp.int32)
        else:
            better = gsum > best
            best = jnp.where(better, gsum, best)
            gsel = jnp.where(better, gi, gsel)
    _, i1, _, i2 = top2(jnp.where(grp == gsel, biased, -jnp.inf))
    s1 = jnp.where(lane == i1, scores, 0.0).sum(-1, keepdims=True)
    s2 = jnp.where(lane == i2, scores, 0.0).sum(-1, keepdims=True)
    tot = s1 + s2
    col = lax.broadcasted_iota(jnp.int32, (tm, ROUTER_OUT_LANES), 1)
    o_ref[...] = jnp.where(col == 0, i1, jnp.where(col == 1, i2, jnp.where(col == 2, s1 / tot, s2 / tot)))


def _router(x, w_router, router_bias, precise):
    N, D = x.shape
    tm = min(MM_ROW_TILE, N)
    out = pl.pallas_call(
        functools.partial(_router_body, precise=precise),
        grid=(pl.cdiv(N, tm),),
        in_specs=[pl.BlockSpec((tm, D), lambda i: (i, 0)),
                  pl.BlockSpec((D, N_EXPERTS), lambda i: (0, 0)),
                  pl.BlockSpec((1, N_EXPERTS), lambda i: (0, 0))],
        out_specs=pl.BlockSpec((tm, ROUTER_OUT_LANES), lambda i: (i, 0)),
        out_shape=jax.ShapeDtypeStruct((N, ROUTER_OUT_LANES), jnp.float32),
        compiler_params=pltpu.CompilerParams(dimension_semantics=("parallel",),
                                             vmem_limit_bytes=VMEM_LIMIT_BYTES),
        name="router",
    )(x, w_router, router_bias.reshape(1, N_EXPERTS))
    return out[:, 0:2].astype(jnp.int32), out[:, 2:4]


def _tri(c, strict=False, upper=False):
    r = lax.broadcasted_iota(jnp.int32, (c, c), 0)
    s = lax.broadcasted_iota(jnp.int32, (c, c), 1)
    if upper:
        r, s = s, r
    return (r > s) if strict else (r >= s)


def _dn_body(u_ref, gcol_ref, grow_ref, beta_ref, gate_ref, nw_ref, s0_ref, o_ref, s_ref):
    f32 = jnp.float32
    nb, c = u_ref.shape[0], u_ref.shape[1]
    dk, dv = DN_DK, DN_DV

    @pl.when(pl.program_id(0) == 0)
    def _():
        s_ref[...] = s0_ref[...]

    incl = _tri(c)
    strict = _tri(c, strict=True)
    eye = (lax.broadcasted_iota(jnp.int32, (c, c), 0) == lax.broadcasted_iota(jnp.int32, (c, c), 1)).astype(f32)
    nw = nw_ref[...]
    for b in range(nb):
        g_col = gcol_ref[b, 0]
        g_row = grow_ref[b, 0]
        beta_all = beta_ref[b, 0]
        for h in range(DN_HEADS):
            q = u_ref[b, :, h * dk:(h + 1) * dk]
            k = u_ref[b, :, DN_HEADS * dk + h * dk:DN_HEADS * dk + (h + 1) * dk]
            v = u_ref[b, :, 2 * DN_HEADS * dk + h * dv:2 * DN_HEADS * dk + (h + 1) * dv]
            q = q * lax.rsqrt(jnp.sum(q * q, -1, keepdims=True) + EPS) * dk ** -0.5
            k = k * lax.rsqrt(jnp.sum(k * k, -1, keepdims=True) + EPS)
            gc = g_col[:, h:h + 1]
            gr = g_row[h:h + 1, :]
            beta = beta_all[:, h:h + 1]
            dec = jnp.where(incl, jnp.exp(jnp.where(incl, gc - gr, 0.0)), 0.0)
            kb = k * beta
            a = jnp.where(strict, _bdot_nt(kb, k) * dec, 0.0)
            p = -a
            tm = eye + p
            n_sq = max(1, math.ceil(math.log2(c))) - 1
            for _ in range(n_sq):
                p = _hdot(p, p)
                tm = tm + _hdot(tm, p)
            eg = jnp.exp(gc)
            u = _bdot(tm, v * beta)
            w = _bdot(tm, kb * eg)
            s = s_ref[b, h]
            v_new = u - _bdot(w, s)
            attn = _bdot_nt(q, k) * dec
            o = _bdot(q * eg, s) + _bdot(attn, v_new)
            g_last = gc[c - 1:c, :]
            s_ref[b, h] = s * jnp.exp(g_last) + _bdot((k * jnp.exp(g_last - gc)).T, v_new)
            o = o * lax.rsqrt(jnp.mean(o * o, -1, keepdims=True) + EPS) * nw
            o_ref[b, :, h * dv:(h + 1) * dv] = o * jax.nn.silu(gate_ref[b, :, h * dv:(h + 1) * dv])


def _dn_scan(u, g, beta, gate, norm_w, s0):
    f32 = jnp.float32
    B, L, _ = u.shape
    c = min(DN_CHUNK, L)
    assert L % c == 0
    n = L // c
    H = DN_HEADS
    g4 = jnp.cumsum(g.reshape(B, n, c, H), axis=2)
    return pl.pallas_call(
        _dn_body,
        grid=(n,),
        in_specs=[pl.BlockSpec((B, c, DN_CONV_DIM), lambda i: (0, i, 0)),
                  pl.BlockSpec((B, 1, c, H), lambda i: (0, i, 0, 0)),
                  pl.BlockSpec((B, 1, H, c), lambda i: (0, i, 0, 0)),
                  pl.BlockSpec((B, 1, c, H), lambda i: (0, i, 0, 0)),
                  pl.BlockSpec((B, c, H * DN_DV), lambda i: (0, i, 0)),
                  pl.BlockSpec((1, DN_DV), lambda i: (0, 0)),
                  pl.BlockSpec((B, H, DN_DK, DN_DV), lambda i: (0, 0, 0, 0))],
        out_specs=[pl.BlockSpec((B, c, H * DN_DV), lambda i: (0, i, 0)),
                   pl.BlockSpec((B, H, DN_DK, DN_DV), lambda i: (0, 0, 0, 0))],
        out_shape=[jax.ShapeDtypeStruct((B, L, H * DN_DV), f32),
                   jax.ShapeDtypeStruct((B, H, DN_DK, DN_DV), f32)],
        compiler_params=pltpu.CompilerParams(dimension_semantics=("arbitrary",),
                                             vmem_limit_bytes=VMEM_LIMIT_BYTES),
        name="deltanet_scan",
    )(u, g4, g4.transpose(0, 1, 3, 2), beta.reshape(B, n, c, H), gate, norm_w.reshape(1, DN_DV), s0)


HG_SUB = 16


def _hg_body(q_ref, f_ref, i_ref, gate_ref, lb_ref, nw_ref, s0_ref, o_ref, s_ref):
    f32 = jnp.float32
    nb, c = q_ref.shape[0], q_ref.shape[1]
    dk, dv = HG_DK, HG_DV
    sub = min(HG_SUB, c)

    @pl.when(pl.program_id(0) == 0)
    def _():
        s_ref[...] = s0_ref[...]

    lower = _tri(c).astype(f32)
    t_sub = lax.broadcasted_iota(jnp.int32, (sub, 1), 0)
    nw = nw_ref[...]
    for b in range(nb):
        for h in range(HG_HEADS):
            sl = slice(h * dk, (h + 1) * dk)
            q = q_ref[b, :, sl] * dk ** -0.5
            lb = lb_ref[:, sl]
            f = lb + (1.0 - lb) * jax.nn.sigmoid(f_ref[b, :, sl])
            k = 1.0 - f
            v = i_ref[b, :, h * dv:(h + 1) * dv]
            gcum = _hdot(lower, jnp.log(f))
            st = s_ref[b, h]
            o_inter = _bdot_nt(q * jnp.exp(gcum), st)
            v_r = v.astype(MXU_DTYPE).astype(f32)
            for blk in range(c // sub):
                r0 = blk * sub
                q_i = q[r0:r0 + sub]
                g_i = gcum[r0:r0 + sub]
                o_i = o_inter[r0:r0 + sub]
                if blk > 0:
                    g_b = gcum[r0 - 1:r0]
                    a = _hdot_nt(q_i * jnp.exp(g_i - g_b), k[:r0] * jnp.exp(g_b - gcum[:r0]))
                    o_i = o_i + _bdot(a, v[:r0])
                for s in range(sub):
                    seen = t_sub >= s
                    term = q_i * k[r0 + s:r0 + s + 1] * jnp.exp(jnp.where(seen, g_i - gcum[r0 + s:r0 + s + 1], 0.0))
                    a_col = jnp.where(seen, term.sum(-1, keepdims=True), 0.0)
                    o_i = o_i + a_col.astype(MXU_DTYPE).astype(f32) * v_r[r0 + s:r0 + s + 1]
                o_i = o_i * lax.rsqrt(jnp.mean(o_i * o_i, -1, keepdims=True) + EPS) * nw
                o_ref[b, r0:r0 + sub, h * dv:(h + 1) * dv] = o_i * jax.nn.silu(gate_ref[b, r0:r0 + sub, h * dv:(h + 1) * dv])
            g_last = gcum[c - 1:c]
            s_ref[b, h] = st * jnp.exp(g_last) + _bdot(v.T, k * jnp.exp(g_last - gcum))


def _hg_scan(q_raw, f_raw, i_raw, gate, lb, norm_w, s0):
    f32 = jnp.float32
    B, L, _ = q_raw.shape
    c = min(HG_CHUNK, L)
    assert L % c == 0
    n = L // c
    H = HG_HEADS
    tok = lambda w: pl.BlockSpec((B, c, w), lambda i: (0, i, 0))
    o, st = pl.pallas_call(
        _hg_body,
        grid=(n,),
        in_specs=[tok(H * HG_DK), tok(H * HG_DK), tok(H * HG_DV), tok(H * HG_DV),
                  pl.BlockSpec((1, H * HG_DK), lambda i: (0, 0)),
                  pl.BlockSpec((1, HG_DV), lambda i: (0, 0)),
                  pl.BlockSpec((B, H, HG_DV, HG_DK), lambda i: (0, 0, 0, 0))],
        out_specs=[tok(H * HG_DV), pl.BlockSpec((B, H, HG_DV, HG_DK), lambda i: (0, 0, 0, 0))],
        out_shape=[jax.ShapeDtypeStruct((B, L, H * HG_DV), f32),
                   jax.ShapeDtypeStruct((B, H, HG_DV, HG_DK), f32)],
        compiler_params=pltpu.CompilerParams(dimension_semantics=("arbitrary",),
                                             vmem_limit_bytes=VMEM_LIMIT_BYTES),
        name="hgrn2_scan",
    )(q_raw, f_raw, i_raw, gate, lb.reshape(1, H * HG_DK), norm_w.reshape(1, HG_DV), s0.transpose(0, 1, 3, 2))
    return o, st.transpose(0, 1, 3, 2)


def _split(z, sizes):
    return jnp.split(z, np.cumsum(sizes)[:-1].tolist(), axis=-1)


def _layer_norm(x, g, b):
    mu = x.mean(-1, keepdims=True)
    var = jnp.square(x - mu).mean(-1, keepdims=True)
    return (x - mu) * lax.rsqrt(var + EPS) * g + b


def _masked_softmax(logits, mask):
    z = jnp.where(mask, logits.astype(jnp.float32), NEG)
    e = jnp.exp(z - z.max(-1, keepdims=True)) * mask
    s = e.sum(-1, keepdims=True)
    return e / jnp.where(s > 0, s, 1.0)


def _l2norm(a):
    return a * lax.rsqrt(jnp.sum(a * a, -1, keepdims=True) + EPS)


def _gated_rmsnorm(o, gate, w):
    B, L, H, dv = o.shape
    o = o * lax.rsqrt(jnp.mean(o * o, -1, keepdims=True) + EPS) * w
    return o.reshape(B, L, H * dv) * jax.nn.silu(gate)


def _t5_bucket(dist):
    n = jnp.maximum(dist, 0)
    exact = N_BUCKETS // 2
    rel = jnp.log(jnp.maximum(n, 1).astype(jnp.float32) / exact) / math.log(MAX_DISTANCE / exact)
    large = jnp.minimum(exact + (rel * (N_BUCKETS - exact)).astype(jnp.int32), N_BUCKETS - 1)
    return jnp.where(n < exact, n, large)


def _head_bias(dist, rel_bias):
    Q, K = dist.shape
    bias = rel_bias[_t5_bucket(dist)]
    return bias.reshape(Q, K, NSA_KV, NSA_GROUP).transpose(2, 3, 0, 1).astype(jnp.float32)


def _compress(rows, w1, w2, pe):
    B, T, G, dh = rows.shape
    n_chunk = T // CMP_STRIDE
    r = rows[:, :n_chunk * CMP_STRIDE].reshape(B, n_chunk, CMP_STRIDE, G, dh)
    w1h = w1.reshape(2, CMP_STRIDE, dh, CMP_HID)
    lo = jnp.einsum('bcsgd,sdh->bcgh', r, w1h[0])
    hi = jnp.einsum('bcsgd,sdh->bcgh', r, w1h[1])
    h = lo[:, :-1] + hi[:, 1:] + pe.reshape(-1) @ w1
    return jax.nn.silu(h) @ w2


def _importance_map(n_cmp, n_sb):
    c0 = jnp.arange(n_cmp) * CMP_STRIDE
    c1 = c0 + CMP_BLOCK - 1
    b0 = jnp.arange(n_sb) * SEL_BLOCK
    b1 = b0 + SEL_BLOCK - 1
    return ((c0[:, None] <= b1[None, :]) & (c1[:, None] >= b0[None, :])).astype(jnp.float32)


def _nsa_core(q, gates, t, kc, vc, cmp_end, imp_map, ks_blk, vs_blk, kw, vw, w_pos, rel_bias):
    f32 = jnp.float32
    B, Q = q.shape[:2]
    lg = jnp.einsum('bqgjd,bcgd->bgjqc', q, kc).astype(f32) + _head_bias(t[:, None] - cmp_end[None, :], rel_bias)
    p_cmp = _masked_softmax(lg, cmp_end[None, :] <= t[:, None])
    o_cmp = jnp.einsum('bgjqc,bcgd->bqgjd', p_cmp, vc)
    n_sb = ks_blk.shape[2]
    n_top = min(N_SELECT, n_sb)
    imp = jnp.einsum('bgjqc,cn->bgqn', p_cmp, imp_map)
    blk = jnp.arange(n_sb)[None, :]
    cur = (t // SEL_BLOCK)[:, None]
    forced = (blk == 0) | (blk == cur) | (blk == cur - 1)
    score = jnp.where(blk * SEL_BLOCK <= t[:, None], jnp.where(forced, jnp.inf, imp), -jnp.inf)
    top_s, top_i = lax.top_k(score, n_top)
    b_ix = jnp.arange(B)[:, None, None, None]
    g_ix = jnp.arange(NSA_KV)[None, :, None, None]
    n_keys = n_top * SEL_BLOCK
    ks = ks_blk[b_ix, g_ix, top_i].reshape(B, NSA_KV, Q, n_keys, NSA_DH)
    vs = vs_blk[b_ix, g_ix, top_i].reshape(B, NSA_KV, Q, n_keys, NSA_DH)
    s_pos = (top_i[..., None] * SEL_BLOCK + jnp.arange(SEL_BLOCK)).reshape(B, NSA_KV, Q, n_keys)
    s_ok = jnp.repeat(top_s > -jnp.inf, SEL_BLOCK, axis=-1) & (s_pos <= t[:, None])
    table = rel_bias.reshape(N_BUCKETS, NSA_KV, NSA_GROUP).transpose(1, 2, 0)
    bucket = _t5_bucket(t[:, None] - s_pos)
    onehot = (bucket[..., None] == jnp.arange(N_BUCKETS)).astype(f32)
    s_bias = jnp.einsum('bgqmn,gjn->bgjqm', onehot, table.astype(f32), precision=lax.Precision.HIGHEST)
    lg = jnp.einsum('bqgjd,bgqmd->bgjqm', q, ks).astype(f32) + s_bias
    p_sel = _masked_softmax(lg, s_ok[:, :, None])
    o_sel = jnp.einsum('bgjqm,bgqmd->bqgjd', p_sel, vs)
    w_ok = (w_pos[None, :] <= t[:, None]) & (w_pos[None, :] > t[:, None] - WINDOW) & (w_pos[None, :] >= 0)
    lg = jnp.einsum('bqgjd,bkgd->bgjqk', q, kw).astype(f32) + _head_bias(t[:, None] - w_pos[None, :], rel_bias)
    p_win = _masked_softmax(lg, w_ok)
    o_win = jnp.einsum('bgjqk,bkgd->bqgjd', p_win, vw)
    o = gates[..., 0:1] * o_cmp + gates[..., 1:2] * o_sel + gates[..., 2:3] * o_win
    return o.reshape(B, Q, NSA_HEADS * NSA_DH)


def _nsa_mixer(q_raw, kv_raw, gate_raw, past_rows, win_buf, w_cmp1, w_cmp2, pe_cmp, rel_bias):
    B, L, _ = q_raw.shape
    q = (q_raw * NSA_DH ** -0.5).reshape(B, L, NSA_KV, NSA_GROUP, NSA_DH)
    gates = jax.nn.sigmoid(gate_raw).reshape(B, L, NSA_KV, NSA_GROUP, 3)
    kv = kv_raw.reshape(B, L, N_KV_STREAMS, NSA_KV, NSA_DH)
    new_rows, new_win = kv[:, :, :N_CACHE_STREAMS], kv[:, :, N_CACHE_STREAMS:]
    pos0 = 0 if past_rows is None else past_rows.shape[1]
    rows = new_rows if past_rows is None else jnp.concatenate([past_rows, new_rows], 1)
    T = rows.shape[1]
    kc = _compress(rows[:, :, 0], w_cmp1[0], w_cmp2[0], pe_cmp[0])
    vc = _compress(rows[:, :, 1], w_cmp1[1], w_cmp2[1], pe_cmp[1])
    wrows = new_win if win_buf is None else jnp.concatenate([win_buf, new_win], 1)
    keep = min(WINDOW, pos0 + L)
    if past_rows is None:
        o = _nsa_prompt_attention_t(q, gates, kc, vc, rows[:, :, 2], rows[:, :, 3],
                                  new_win[:, :, 0], new_win[:, :, 1], rel_bias)
        return o, new_rows, wrows[:, -keep:]
    n_cmp = kc.shape[1]
    cmp_end = jnp.arange(n_cmp) * CMP_STRIDE + CMP_BLOCK - 1
    n_sb = -(-T // SEL_BLOCK)
    sel = jnp.pad(rows[:, :, 2:4], ((0, 0), (0, n_sb * SEL_BLOCK - T), (0, 0), (0, 0), (0, 0)))
    sel = sel.reshape(B, n_sb, SEL_BLOCK, 2, NSA_KV, NSA_DH).transpose(3, 0, 4, 1, 2, 5)
    imp_map = _importance_map(n_cmp, n_sb)
    n_buf = wrows.shape[1] - L
    o = _nsa_core(q, gates, pos0 + jnp.arange(L), kc, vc, cmp_end, imp_map, sel[0], sel[1],
                  wrows[:, :, 0], wrows[:, :, 1], pos0 - n_buf + jnp.arange(n_buf + L), rel_bias)
    return o, new_rows, wrows[:, -keep:]


def _chunk(a, c):
    B, L = a.shape[:2]
    n = -(-L // c)
    a = jnp.pad(a, [(0, 0), (0, n * c - L)] + [(0, 0)] * (a.ndim - 2))
    a = a.reshape((B, n, c) + a.shape[2:])
    return jnp.moveaxis(a, (1, 2), (0, 3))


def _unchunk(o, L):
    n, B, H, c, d = o.shape
    return jnp.moveaxis(o, (0, 3), (1, 2)).reshape(B, n * c, H, d)[:, :L]


def _gated_delta(q, k, v, beta, g, s0):
    f32 = jnp.float32
    L = k.shape[1]
    c = min(DN_CHUNK, L)
    qs, ks, vs = (_chunk(a, c) for a in (q, k, v))
    bs, gs = _chunk(beta, c), _chunk(g, c)
    incl = jnp.tril(jnp.ones((c, c), bool))
    strict = jnp.tril(jnp.ones((c, c), bool), -1)
    eye = jnp.eye(c, dtype=f32)

    def step(S, xs):
        qc, kc, vc, bc, gc = xs
        G = jnp.cumsum(gc, -1)
        dec = jnp.where(incl, jnp.exp(jnp.where(incl, G[..., :, None] - G[..., None, :], 0.0)), 0.0)
        kb = kc * bc[..., None]
        A = jnp.where(strict, jnp.einsum('bhtk,bhsk->bhts', kb, kc) * dec, 0.0)
        Tm = lax.linalg.triangular_solve(eye + A, jnp.broadcast_to(eye, A.shape), left_side=True, lower=True)
        u = Tm @ (vc * bc[..., None])
        w = Tm @ (kb * jnp.exp(G)[..., None])
        v_new = u - w @ S
        attn = jnp.einsum('bhtk,bhsk->bhts', qc, kc) * dec
        o = (qc * jnp.exp(G)[..., None]) @ S + attn @ v_new
        S = S * jnp.exp(G[..., -1])[..., None, None] + jnp.einsum(
            'bhtk,bhtv->bhkv', kc * jnp.exp(G[..., -1:] - G)[..., None], v_new)
        return S, o

    S, o = lax.scan(step, s0, (qs, ks, vs, bs, gs))
    return _unchunk(o, L), S


def _gla_scan(q, k, v, logf, s0):
    L = k.shape[1]
    c = min(HG_CHUNK, L)
    qs, ks, vs, fs = (_chunk(a, c) for a in (q, k, v, logf))
    incl = jnp.tril(jnp.ones((c, c), bool))[:, :, None]

    def step(S, xs):
        qc, kc, vc, fc = xs
        G = jnp.cumsum(fc, axis=-2)
        diff = G[:, :, :, None, :] - G[:, :, None, :, :]
        dec = jnp.where(incl, jnp.exp(jnp.where(incl, diff, 0.0)), 0.0)
        attn = jnp.einsum('bhtk,bhsk,bhtsk->bhts', qc, kc, dec)
        o = jnp.einsum('bhtk,bhkv->bhtv', qc * jnp.exp(G), S) + jnp.einsum('bhts,bhsv->bhtv', attn, vc)
        S = S * jnp.exp(G[:, :, -1])[..., None] + jnp.einsum(
            'bhsk,bhsv->bhkv', kc * jnp.exp(G[:, :, -1:] - G), vc)
        return S, o

    S, o = lax.scan(step, s0, (qs, ks, vs, fs))
    return _unchunk(o, L), S


def _short_conv(u, buf, w):
    L = u.shape[1]
    xc = jnp.concatenate([buf, u], 1)
    y = sum(xc[:, j:j + L] * w[j] for j in range(DN_CONV))
    return jax.nn.silu(y), xc[:, -(DN_CONV - 1):]


def _deltanet_mixer(qkv_raw, a_raw, b_raw, gate_raw, conv_buf, s0, conv_w, a_log, dt_bias, norm_w):
    B, L, _ = qkv_raw.shape
    if conv_buf is None:
        conv_buf = jnp.zeros((B, DN_CONV - 1, DN_CONV_DIM), qkv_raw.dtype)
    if s0 is None:
        s0 = jnp.zeros((B, DN_HEADS, DN_DK, DN_DV), qkv_raw.dtype)
    u, new_buf = _short_conv(qkv_raw, conv_buf, conv_w)
    if L % DN_CHUNK == 0:
        beta = jax.nn.sigmoid(b_raw)
        g = -jnp.exp(a_log) * jax.nn.softplus(a_raw + dt_bias)
        o, s_new = _dn_scan(u, g, beta, gate_raw, norm_w, s0)
        return o, new_buf, s_new
    q, k, v = _split(u, (DN_HEADS * DN_DK, DN_HEADS * DN_DK, DN_HEADS * DN_DV))
    q = _l2norm(q.reshape(B, L, DN_HEADS, DN_DK)) * DN_DK ** -0.5
    k = _l2norm(k.reshape(B, L, DN_HEADS, DN_DK))
    v = v.reshape(B, L, DN_HEADS, DN_DV)
    beta = jax.nn.sigmoid(b_raw)
    g = -jnp.exp(a_log) * jax.nn.softplus(a_raw + dt_bias)
    o, s_new = _gated_delta(q, k, v, beta, g, s0)
    return _gated_rmsnorm(o, gate_raw, norm_w), new_buf, s_new


def _hgrn2_mixer(q_raw, f_raw, i_raw, gate_raw, s0, lb, norm_w):
    B, L, _ = q_raw.shape
    if s0 is None:
        s0 = jnp.zeros((B, HG_HEADS, HG_DK, HG_DV), q_raw.dtype)
    if L % HG_CHUNK == 0:
        return _hg_scan(q_raw, f_raw, i_raw, gate_raw, lb, norm_w, s0)
    q = q_raw.reshape(B, L, HG_HEADS, HG_DK) * HG_DK ** -0.5
    f = lb + (1.0 - lb) * jax.nn.sigmoid(f_raw)
    logf = jnp.log(f).reshape(B, L, HG_HEADS, HG_DK)
    k = (1.0 - f).reshape(B, L, HG_HEADS, HG_DK)
    v = i_raw.reshape(B, L, HG_HEADS, HG_DV)
    o, s_new = _gla_scan(q, k, v, logf, s0)
    return _gated_rmsnorm(o, gate_raw, norm_w), s_new


def _moe(x, w_router, router_bias, wg, wu, wd, precise):
    f32 = jnp.float32
    N, D = x.shape
    eidx, gw = _router(x, w_router, router_bias, precise)
    n_assign = N * TOP_K
    e_flat = eidx.reshape(-1)
    onehot = (e_flat[:, None] == jnp.arange(N_EXPERTS)[None, :]).astype(jnp.int32)
    running = jnp.cumsum(onehot, axis=0)
    counts = running[-1]
    rank = jnp.take_along_axis(running, e_flat[:, None], axis=1)[:, 0] - 1
    padded = (counts + MOE_ROW_BLOCK - 1) // MOE_ROW_BLOCK * MOE_ROW_BLOCK
    pad_end = jnp.cumsum(padded)
    dest = ((pad_end - padded)[e_flat] + rank).astype(jnp.int32)
    n_blocks = -(-n_assign // MOE_ROW_BLOCK) + N_EXPERTS
    n_rows = n_blocks * MOE_ROW_BLOCK
    tok = jnp.full((n_rows,), N, jnp.int32).at[dest].set(jnp.arange(n_assign, dtype=jnp.int32) // TOP_K)
    wt = jnp.zeros((n_rows,), f32).at[dest].set(gw.reshape(-1))
    blk_exp = jnp.minimum(jnp.searchsorted(pad_end, jnp.arange(n_blocks) * MOE_ROW_BLOCK, side='right'),
                          N_EXPERTS - 1).astype(jnp.int32)
    x_pad = jnp.concatenate([x, jnp.zeros((1, D), x.dtype)])
    out = _expert_ffn(x_pad[tok], wt, blk_exp, wg, wu, wd, precise)
    slot = dest.reshape(N, TOP_K)
    return out[slot[:, 0]] + out[slot[:, 1]]


def _trunk_layer(x, ple, lb, past, precise, w_in, w_cmp1, w_cmp2, pe_cmp, dn_conv_w, dn_a_log, dn_dt_bias,
                 dn_norm_w, hg_norm_w, w_branch, w_out, ln1_g, ln1_b, ln2_g, ln2_b, w_router,
                 router_bias, w_exp_gate, w_exp_up, w_exp_down, w_ple, w_ple_gate, rel_bias):
    past_rows, win_buf, conv_buf, dn_state, hg_state = past
    B, L, D = x.shape
    x2 = x.reshape(B * L, D)
    mm = functools.partial(_mm, precise=precise)
    z = mm(x2, w_in).reshape(B, L, -1)
    (nsa_q, nsa_kv, nsa_g, dn_qkv, dn_a, dn_b, dn_g,
     hg_q, hg_f, hg_i, hg_g, merge) = _split(z, IN_SPLITS)
    o_a, rows_new, win_new = _nsa_mixer(nsa_q, nsa_kv, nsa_g, past_rows, win_buf, w_cmp1, w_cmp2, pe_cmp, rel_bias)
    o_b, conv_new, dn_new = _deltanet_mixer(dn_qkv, dn_a, dn_b, dn_g, conv_buf, dn_state, dn_conv_w,
                                            dn_a_log, dn_dt_bias, dn_norm_w)
    o_c, hg_new = _hgrn2_mixer(hg_q, hg_f, hg_i, hg_g, hg_state, lb, hg_norm_w)
    g = jax.nn.sigmoid(merge).reshape(B * L, 3, D)
    m = (g[:, 0] * mm(o_a.reshape(B * L, -1), w_branch[0]) + g[:, 1] * mm(o_b.reshape(B * L, -1), w_branch[1])
         + g[:, 2] * mm(o_c.reshape(B * L, -1), w_branch[2]))
    x2 = _layer_norm(DEEPNORM_ALPHA * x2 + mm(m, w_out), ln1_g, ln1_b)
    ffn = _moe(x2, w_router, router_bias, w_exp_gate, w_exp_up, w_exp_down, precise)
    ple_term = mm(ple.reshape(B * L, -1), w_ple) * jax.nn.sigmoid(mm(x2, w_ple_gate))
    x2 = _layer_norm(DEEPNORM_ALPHA * x2 + ffn + ple_term, ln2_g, ln2_b)
    return x2.reshape(B, L, D), (rows_new, win_new, conv_new, dn_new, hg_new)


def kernel(x_prompt, x_sample, cache_nsa, state_win, state_dn_conv, state_dn, state_hg, page_table,
           p_prompt, p_sample, w_in, w_cmp1, w_cmp2, pe_cmp, dn_conv_w, dn_a_log, dn_dt_bias, dn_norm_w,
           hg_lb_logits, hg_norm_w, w_branch, w_out, ln1_g, ln1_b, ln2_g, ln2_b, w_router, router_bias,
           w_exp_gate, w_exp_up, w_exp_down, w_ple, w_ple_gate, rel_bias):
    lb = jnp.cumsum(jax.nn.softmax(hg_lb_logits, axis=0), axis=0)
    lb = lb - lb[0:1]
    n_dec = page_table.shape[0]
    y_prompt, y_sample = x_prompt, x_sample
    st_p, st_s = [], []
    for l in range(DEPTH):
        lw = (w_in[l], w_cmp1[l], w_cmp2[l], pe_cmp[l], dn_conv_w[l], dn_a_log[l], dn_dt_bias[l],
              dn_norm_w[l], hg_norm_w[l], w_branch[l], w_out[l], ln1_g[l], ln1_b[l], ln2_g[l], ln2_b[l],
              w_router, router_bias, w_exp_gate[l], w_exp_up[l], w_exp_down[l], w_ple[l], w_ple_gate[l],
              rel_bias)
        y_prompt, s = _trunk_layer(y_prompt, p_prompt[l], lb[l], (None, None, None, None, None), False, *lw)
        st_p.append(s)
        pages = jnp.take(cache_nsa[l].reshape(cache_nsa.shape[1], -1), page_table.reshape(-1), axis=0)
        past_rows = pages.reshape(n_dec, -1, N_CACHE_STREAMS, NSA_KV, NSA_DH)
        y_sample, s = _trunk_layer(y_sample, p_sample[l], lb[l],
                                   (past_rows, state_win[l], state_dn_conv[l], state_dn[l], state_hg[l]),
                                   False, *lw)
        st_s.append(s)
    outs = [y_prompt, y_sample]
    for i in range(5):
        outs.append(jnp.stack([s[i] for s in st_p]))
        outs.append(jnp.stack([s[i] for s in st_s]))
    return tuple(outs)
```

```python
import functools
import math

import jax
import jax.numpy as jnp
import numpy as np
from jax import lax
from jax.experimental import pallas as pl
from jax.experimental.pallas import tpu as pltpu

D_MODEL = 1024
DEPTH = 4
NSA_HEADS = 8
NSA_KV = 2
NSA_GROUP = NSA_HEADS // NSA_KV
NSA_DH = 64
CMP_STRIDE = 16
CMP_BLOCK = 2 * CMP_STRIDE
CMP_HID = 128
SEL_BLOCK = 64
N_SELECT = 16
WINDOW = 512
Q_BLOCK = 128
N_KV_STREAMS = 6
N_CACHE_STREAMS = 4
N_BUCKETS = 32
MAX_DISTANCE = 128
DN_HEADS = 4
DN_DK = 128
DN_DV = 128
DN_CONV = 4
DN_CHUNK = 64
DN_CONV_DIM = DN_HEADS * (2 * DN_DK + DN_DV)
HG_HEADS = 4
HG_DK = 128
HG_DV = 128
HG_CHUNK = 64
N_EXPERTS = 16
N_EXPERT_GROUPS = 4
EXPERTS_PER_GROUP = N_EXPERTS // N_EXPERT_GROUPS
TOP_K = 2
D_EXPERT = 512
MOE_ROW_BLOCK = 128
DEEPNORM_ALPHA = (2 * DEPTH) ** 0.25
EPS = 1e-6
NEG = -1e30
IN_SPLITS = (NSA_HEADS * NSA_DH, N_KV_STREAMS * NSA_KV * NSA_DH, 3 * NSA_HEADS,
             DN_CONV_DIM, DN_HEADS, DN_HEADS, DN_HEADS * DN_DV,
             HG_HEADS * HG_DK, HG_HEADS * HG_DK, HG_HEADS * HG_DV, HG_HEADS * HG_DV,
             3 * D_MODEL)

VMEM_LIMIT_BYTES = 48 * 1024 * 1024
MM_ROW_TILE = 1024
MM_COL_TILE = 512


def _mm_body(x_ref, w_ref, o_ref, *, precise):
    if precise:
        o_ref[...] = jnp.dot(x_ref[...], w_ref[...], preferred_element_type=jnp.float32,
                             precision=lax.Precision.HIGHEST)
    else:
        o_ref[...] = jnp.dot(x_ref[...].astype(jnp.bfloat16), w_ref[...].astype(jnp.bfloat16),
                             preferred_element_type=jnp.float32)


def _mm(x, w, precise=False):
    M, K = x.shape
    N = w.shape[1]
    tm = min(MM_ROW_TILE, M)
    tn = min(MM_COL_TILE, N)
    return pl.pallas_call(
        functools.partial(_mm_body, precise=precise),
        grid=(pl.cdiv(M, tm), pl.cdiv(N, tn)),
        in_specs=[pl.BlockSpec((tm, K), lambda i, j: (i, 0)),
                  pl.BlockSpec((K, tn), lambda i, j: (0, j))],
        out_specs=pl.BlockSpec((tm, tn), lambda i, j: (i, j)),
        out_shape=jax.ShapeDtypeStruct((M, N), jnp.float32),
        compiler_params=pltpu.CompilerParams(
            dimension_semantics=("parallel", "parallel"),
            vmem_limit_bytes=VMEM_LIMIT_BYTES),
        name="mm",
    )(x, w)


def _expert_body(blk_exp_ref, h_ref, wt_ref, wg_ref, wu_ref, wd_ref, o_ref, *, precise):
    del blk_exp_ref
    if precise:
        dot = functools.partial(jnp.dot, preferred_element_type=jnp.float32,
                                precision=lax.Precision.HIGHEST)
        h = h_ref[...]
        a = jax.nn.silu(dot(h, wg_ref[0])) * dot(h, wu_ref[0])
        o_ref[...] = dot(a, wd_ref[0]) * wt_ref[...]
    else:
        bf = jnp.bfloat16
        dot = functools.partial(jnp.dot, preferred_element_type=jnp.float32)
        h = h_ref[...].astype(bf)
        a = jax.nn.silu(dot(h, wg_ref[0].astype(bf))) * dot(h, wu_ref[0].astype(bf))
        o_ref[...] = dot(a.astype(bf), wd_ref[0].astype(bf)) * wt_ref[...]


def _expert_ffn(h, wt, blk_exp, wg, wu, wd, precise):
    n_rows, D = h.shape
    n_blocks = n_rows // MOE_ROW_BLOCK
    grid_spec = pltpu.PrefetchScalarGridSpec(
        num_scalar_prefetch=1,
        grid=(n_blocks,),
        in_specs=[pl.BlockSpec((MOE_ROW_BLOCK, D), lambda i, be: (i, 0)),
                  pl.BlockSpec((MOE_ROW_BLOCK, 1), lambda i, be: (i, 0)),
                  pl.BlockSpec((1, D, D_EXPERT), lambda i, be: (be[i], 0, 0)),
                  pl.BlockSpec((1, D, D_EXPERT), lambda i, be: (be[i], 0, 0)),
                  pl.BlockSpec((1, D_EXPERT, D), lambda i, be: (be[i], 0, 0))],
        out_specs=pl.BlockSpec((MOE_ROW_BLOCK, D), lambda i, be: (i, 0)),
    )
    return pl.pallas_call(
        functools.partial(_expert_body, precise=precise),
        grid_spec=grid_spec,
        out_shape=jax.ShapeDtypeStruct((n_rows, D), jnp.float32),
        compiler_params=pltpu.CompilerParams(
            dimension_semantics=("arbitrary",),
            vmem_limit_bytes=VMEM_LIMIT_BYTES),
        name="expert_ffn",
    )(blk_exp, h, wt.reshape(n_rows, 1), wg, wu, wd)


NSA_ROWS = NSA_GROUP * Q_BLOCK
NEAR_CMP = 32
NEAR_CMP_BACK = 16
CMP_PER_QBLOCK = Q_BLOCK // CMP_STRIDE
SEL_PER_TILE = Q_BLOCK // SEL_BLOCK
WIN_TILES = WINDOW // Q_BLOCK
MXU_DTYPE = jnp.bfloat16


def _nsa_prompt_body(q_ref, gt_ref, kct_ref, kc_ref, vc_ref, kst_ref, vs_ref, kwt_ref, vw_ref,
                     band_ref, near_ref, far_ref, imap_ref, o_ref, m_scr, l_scr, acc_scr):
    f32 = jnp.float32
    R, QB = NSA_ROWS, Q_BLOCK
    i = pl.program_id(2)
    n_cmp_pad = kct_ref.shape[-1]
    n_sb = imap_ref.shape[-1]
    q = q_ref[0, 0, 0]
    far_bias = far_ref[0]
    r_row = lax.broadcasted_iota(jnp.int32, (R, 1), 0) & (QB - 1)
    t_row = i * QB + r_row

    def mxu(a, b):
        return jnp.dot(a.astype(MXU_DTYPE), b.astype(MXU_DTYPE), preferred_element_type=f32)

    def mxu_nt(a, b):
        return lax.dot_general(a.astype(MXU_DTYPE), b.astype(MXU_DTYPE), (((1,), (1,)), ((), ())),
                               preferred_element_type=f32)

    c0 = CMP_PER_QBLOCK * i - NEAR_CMP_BACK
    lg_far = mxu(q, kct_ref[0, 0]) + far_bias
    mk_far = lax.broadcasted_iota(jnp.int32, (R, n_cmp_pad), 1) < c0
    near_start = pl.multiple_of(CMP_PER_QBLOCK * i, 8)
    k_near = kc_ref[0, 0, pl.ds(near_start, NEAR_CMP), :]
    v_near = vc_ref[0, 0, pl.ds(near_start, NEAR_CMP), :]
    lg_near = mxu_nt(q, k_near) + near_ref[0]
    c_near = c0 + lax.broadcasted_iota(jnp.int32, (R, NEAR_CMP), 1)
    mk_near = (c_near >= 0) & (c_near * CMP_STRIDE + (CMP_BLOCK - 1) <= t_row)
    z_far = jnp.where(mk_far, lg_far, NEG)
    z_near = jnp.where(mk_near, lg_near, NEG)
    mx = jnp.maximum(z_far.max(-1, keepdims=True), z_near.max(-1, keepdims=True))
    e_far = jnp.where(mk_far, jnp.exp(z_far - mx), 0.0)
    e_near = jnp.where(mk_near, jnp.exp(z_near - mx), 0.0)
    s = e_far.sum(-1, keepdims=True) + e_near.sum(-1, keepdims=True)
    s = jnp.where(s > 0, s, 1.0)
    p_far = e_far / s
    p_near = e_near / s
    o_cmp = (mxu(p_far, vc_ref[0, 0, NEAR_CMP_BACK:NEAR_CMP_BACK + n_cmp_pad, :]) + mxu(p_near, v_near))

    cn = c0 + lax.broadcasted_iota(jnp.int32, (NEAR_CMP, n_sb), 0)
    nn = lax.broadcasted_iota(jnp.int32, (NEAR_CMP, n_sb), 1)
    blocks_per_sel = SEL_BLOCK // CMP_STRIDE
    imap_near = ((cn >= blocks_per_sel * nn - 1) & (cn <= blocks_per_sel * nn + blocks_per_sel - 1)).astype(f32)
    imp_rows = mxu(p_far, imap_ref[...]) + mxu(p_near, imap_near)
    imp = imp_rows[0:QB]
    for j in range(1, NSA_GROUP):
        imp = imp + imp_rows[j * QB:(j + 1) * QB]
    n_io = lax.broadcasted_iota(jnp.int32, (QB, n_sb), 1)
    n_f = n_io.astype(f32)
    r_q = lax.broadcasted_iota(jnp.int32, (QB, n_sb), 0)
    t_q = i * QB + r_q
    cur = SEL_PER_TILE * i + (r_q >= SEL_BLOCK).astype(jnp.int32)
    forced = (n_io == 0) | (n_io == cur) | (n_io == cur - 1)
    score = jnp.where(n_io * SEL_BLOCK <= t_q, jnp.where(forced, jnp.inf, imp), -jnp.inf)

    def pick_one(_, carry):
        score, sel = carry
        top = score.max(-1, keepdims=True)
        first = jnp.where(score == top, n_f, float(n_sb)).min(-1, keepdims=True)
        hit = n_f == first
        sel = jnp.where(hit & (top > -jnp.inf), 1.0, sel)
        return jnp.where(hit, -jnp.inf, score), sel

    _, sel = lax.fori_loop(0, min(N_SELECT, n_sb), pick_one, (score, jnp.zeros((QB, n_sb), f32)))

    r_t = lax.broadcasted_iota(jnp.int32, (R, QB), 0) & (QB - 1)
    c_t = lax.broadcasted_iota(jnp.int32, (R, QB), 1)

    def reset():
        m_scr[...] = jnp.full((R, 1), NEG, f32)
        l_scr[...] = jnp.zeros((R, 1), f32)
        acc_scr[...] = jnp.zeros_like(acc_scr)

    def update(lg, mk, v):
        z = jnp.where(mk, lg, NEG) if mk is not None else lg
        m_old = m_scr[...]
        m_new = jnp.maximum(m_old, z.max(-1, keepdims=True))
        p = jnp.exp(z - m_new)
        if mk is not None:
            p = jnp.where(mk, p, 0.0)
        alpha = jnp.exp(m_old - m_new)
        l_scr[...] = alpha * l_scr[...] + p.sum(-1, keepdims=True)
        acc_scr[...] = alpha * acc_scr[...] + mxu(p, v)
        m_scr[...] = m_new

    def result():
        l = l_scr[...]
        return acc_scr[...] / jnp.where(l > 0, l, 1.0)

    def tile(kt_ref, v_ref, kt):
        start = pl.multiple_of(kt * QB, QB)
        return mxu(q, kt_ref[0, 0, :, pl.ds(start, QB)]), v_ref[0, 0, pl.ds(start, QB), :]

    def sel_mask(kt):
        blk = SEL_PER_TILE * kt + (lax.broadcasted_iota(jnp.int32, (n_sb, QB), 1) >= SEL_BLOCK).astype(jnp.int32)
        expand = (lax.broadcasted_iota(jnp.int32, (n_sb, QB), 0) == blk).astype(f32)
        mq = mxu(sel, expand)
        return jnp.concatenate([mq] * NSA_GROUP, axis=0) > 0.5

    reset()

    def sel_far(kt, carry):
        lg, v = tile(kst_ref, vs_ref, kt)
        update(lg + far_bias, sel_mask(kt), v)
        return carry

    lax.fori_loop(0, jnp.maximum(i - 1, 0), sel_far, 0)

    @pl.when(i >= 1)
    def _():
        lg, v = tile(kst_ref, vs_ref, i - 1)
        update(lg + band_ref[0, 1], sel_mask(i - 1), v)

    lg, v = tile(kst_ref, vs_ref, i)
    update(lg + band_ref[0, 0], sel_mask(i) & (c_t <= r_t), v)
    o_sel = result()

    reset()

    @pl.when(i >= WIN_TILES)
    def _():
        lg, v = tile(kwt_ref, vw_ref, i - WIN_TILES)
        update(lg + far_bias, c_t > r_t, v)

    for back in range(WIN_TILES - 1, 1, -1):
        @pl.when(i >= back)
        def _(back=back):
            lg, v = tile(kwt_ref, vw_ref, i - back)
            update(lg + far_bias, None, v)

    @pl.when(i >= 1)
    def _():
        lg, v = tile(kwt_ref, vw_ref, i - 1)
        update(lg + band_ref[0, 1], None, v)

    lg, v = tile(kwt_ref, vw_ref, i)
    update(lg + band_ref[0, 0], c_t <= r_t, v)
    o_win = result()

    gt = gt_ref[0, 0, 0]
    o_ref[0, 0, 0] = gt[:, 0:1] * o_cmp + gt[:, 1:2] * o_sel + gt[:, 2:3] * o_win


def _nsa_prompt_attention(q, gates, kc, vc, ksel, vsel, kwin, vwin, rel_bias):
    f32 = jnp.float32
    B, L = q.shape[:2]
    G, J, dh, QB, R = NSA_KV, NSA_GROUP, NSA_DH, Q_BLOCK, NSA_ROWS
    assert L % QB == 0 and WINDOW % QB == 0 and QB % SEL_BLOCK == 0
    nq = L // QB
    n_cmp = kc.shape[1]
    n_cmp_pad = L // CMP_STRIDE
    n_sb = L // SEL_BLOCK
    assert n_cmp == n_cmp_pad - 1

    def rows(a):
        x = a.shape[-1]
        return a.reshape(B, nq, QB, G, J, x).transpose(0, 3, 1, 4, 2, 5).reshape(B, G, nq, R, x)

    q_r = rows(q).astype(MXU_DTYPE)
    gt_r = rows(gates)
    kc_p = jnp.pad(kc, ((0, 0), (NEAR_CMP_BACK, n_cmp_pad - n_cmp + NEAR_CMP - NEAR_CMP_BACK), (0, 0), (0, 0)))
    vc_p = jnp.pad(vc, ((0, 0), (NEAR_CMP_BACK, n_cmp_pad - n_cmp + NEAR_CMP - NEAR_CMP_BACK), (0, 0), (0, 0)))
    kc_p = kc_p.transpose(0, 2, 1, 3)
    vc_p = vc_p.transpose(0, 2, 1, 3)
    kct = kc_p[:, :, NEAR_CMP_BACK:NEAR_CMP_BACK + n_cmp_pad].transpose(0, 1, 3, 2).astype(MXU_DTYPE)
    kst = ksel.transpose(0, 2, 3, 1).astype(MXU_DTYPE)
    kwt = kwin.transpose(0, 2, 3, 1).astype(MXU_DTYPE)
    vs = vsel.transpose(0, 2, 1, 3).astype(MXU_DTYPE)
    vw = vwin.transpose(0, 2, 1, 3).astype(MXU_DTYPE)

    n_dist = MAX_DISTANCE
    by_dist = rel_bias[_t5_bucket(jnp.arange(n_dist))].astype(f32).reshape(n_dist, G, J)
    by_dist = by_dist.transpose(1, 2, 0)

    def table(dist):
        d = jnp.clip(dist, 0, n_dist - 1)
        return by_dist[:, :, d].reshape(G, R, dist.shape[1])

    r = jnp.arange(QB)[:, None]
    c = jnp.arange(QB)[None, :]
    band = jnp.stack([table(r - c), table(r - c + QB)], axis=1)
    u = jnp.arange(NEAR_CMP)[None, :]
    near = table(r - CMP_STRIDE * (u - NEAR_CMP_BACK) - (CMP_BLOCK - 1))
    far = jnp.repeat(by_dist[:, :, n_dist - 1], QB, axis=1).reshape(G, R, 1)
    imap = _importance_map(n_cmp_pad, n_sb)

    kv_spec = lambda shape: pl.BlockSpec((1, 1) + shape, lambda b, g, i: (b, g, 0, 0))
    out = pl.pallas_call(
        _nsa_prompt_body,
        grid=(B, G, nq),
        in_specs=[pl.BlockSpec((1, 1, 1, R, dh), lambda b, g, i: (b, g, i, 0, 0)),
                  pl.BlockSpec((1, 1, 1, R, 3), lambda b, g, i: (b, g, i, 0, 0)),
                  kv_spec((dh, n_cmp_pad)),
                  kv_spec((n_cmp_pad + NEAR_CMP, dh)),
                  kv_spec((n_cmp_pad + NEAR_CMP, dh)),
                  kv_spec((dh, L)), kv_spec((L, dh)), kv_spec((dh, L)), kv_spec((L, dh)),
                  pl.BlockSpec((1, 2, R, QB), lambda b, g, i: (g, 0, 0, 0)),
                  pl.BlockSpec((1, R, NEAR_CMP), lambda b, g, i: (g, 0, 0)),
                  pl.BlockSpec((1, R, 1), lambda b, g, i: (g, 0, 0)),
                  pl.BlockSpec((n_cmp_pad, n_sb), lambda b, g, i: (0, 0))],
        out_specs=pl.BlockSpec((1, 1, 1, R, dh), lambda b, g, i: (b, g, i, 0, 0)),
        out_shape=jax.ShapeDtypeStruct((B, G, nq, R, dh), f32),
        scratch_shapes=[pltpu.VMEM((R, 1), f32), pltpu.VMEM((R, 1), f32), pltpu.VMEM((R, dh), f32)],
        compiler_params=pltpu.CompilerParams(
            dimension_semantics=("parallel", "parallel", "arbitrary"),
            vmem_limit_bytes=VMEM_LIMIT_BYTES),
        name="nsa_prompt",
    )(q_r, gt_r, kct, kc_p, vc_p, kst, vs, kwt, vw, band, near, far, imap)
    return out.reshape(B, G, nq, J, QB, dh).transpose(0, 2, 4, 1, 3, 5).reshape(B, L, G * J * dh)


MASK_OFF = -1e30
FAR_TILES = 4
SEL_SHIFT = SEL_BLOCK.bit_length() - 1
QB_SHIFT = Q_BLOCK.bit_length() - 1
assert 1 << SEL_SHIFT == SEL_BLOCK and 1 << QB_SHIFT == Q_BLOCK


def _nsa_prompt_body_t(q_ref, gt_ref, kc_ref, vct_ref, vc_ref, ks_ref, vst_ref, kw_ref, vwt_ref,
                       band_ref, near_ref, far_ref, winb_ref, imapt_ref, o_ref, m_scr, l_scr, acc_scr, z_scr):
    f32 = jnp.float32
    R, QB = NSA_ROWS, Q_BLOCK
    i = pl.program_id(2)
    n_cmp_pad = vct_ref.shape[-1]
    n_sb = imapt_ref.shape[0]
    qt = q_ref[0, 0, 0]
    far_bias = far_ref[0]
    r_lane = lax.broadcasted_iota(jnp.int32, (1, R), 1) & (QB - 1)
    t_lane = i * QB + r_lane

    def mxu(a, b):
        return jnp.dot(a.astype(MXU_DTYPE), b.astype(MXU_DTYPE), preferred_element_type=f32)

    def recip(x):
        r0 = 1.0 / x
        return r0 * (2.0 - x * r0)

    c0 = CMP_PER_QBLOCK * i - NEAR_CMP_BACK
    kc_far = kc_ref[0, 0, NEAR_CMP_BACK:NEAR_CMP_BACK + n_cmp_pad, :]
    lg_far = mxu(kc_far, qt) + far_bias
    mk_far = lax.broadcasted_iota(jnp.int32, (n_cmp_pad, R), 0) < c0
    near_start = pl.multiple_of(CMP_PER_QBLOCK * i, 8)
    k_near = kc_ref[0, 0, pl.ds(near_start, NEAR_CMP), :]
    v_near = vc_ref[0, 0, pl.ds(near_start, NEAR_CMP), :]
    lg_near = mxu(k_near, qt) + near_ref[0]
    c_near = c0 + lax.broadcasted_iota(jnp.int32, (NEAR_CMP, R), 0)
    mk_near = (c_near >= 0) & (c_near * CMP_STRIDE + (CMP_BLOCK - 1) <= t_lane)
    z_far = jnp.where(mk_far, lg_far, NEG)
    z_near = jnp.where(mk_near, lg_near, NEG)
    mx = jnp.maximum(z_far.max(0, keepdims=True), z_near.max(0, keepdims=True))
    e_far = jnp.where(mk_far, jnp.exp(z_far - mx), 0.0)
    e_near = jnp.where(mk_near, jnp.exp(z_near - mx), 0.0)
    s = e_far.sum(0, keepdims=True) + e_near.sum(0, keepdims=True)
    inv = recip(jnp.where(s > 0, s, 1.0))
    p_far = (e_far * inv).astype(MXU_DTYPE)
    p_near = (e_near * inv).astype(MXU_DTYPE)
    o_cmp = mxu(vct_ref[0, 0], p_far) + mxu(v_near.T, p_near)

    nn = lax.broadcasted_iota(jnp.int32, (n_sb, NEAR_CMP), 0)
    cn = c0 + lax.broadcasted_iota(jnp.int32, (n_sb, NEAR_CMP), 1)
    per_sel = SEL_BLOCK // CMP_STRIDE
    imapt_near = ((cn >= per_sel * nn - 1) & (cn <= per_sel * nn + per_sel - 1)).astype(f32)
    imp_rows = mxu(imapt_ref[...], p_far) + mxu(imapt_near, p_near)
    imp = imp_rows[:, 0:QB]
    for j in range(1, NSA_GROUP):
        imp = imp + imp_rows[:, j * QB:(j + 1) * QB]
    n_io = lax.broadcasted_iota(jnp.int32, (n_sb, QB), 0)
    n_f = n_io.astype(f32)
    r_q = lax.broadcasted_iota(jnp.int32, (n_sb, QB), 1)
    t_q = i * QB + r_q
    cur = SEL_PER_TILE * i + (r_q >= SEL_BLOCK).astype(jnp.int32)
    forced = (n_io == 0) | (n_io == cur) | (n_io == cur - 1)
    score = jnp.where(n_io * SEL_BLOCK <= t_q, jnp.where(forced, jnp.inf, imp), -jnp.inf)

    def pick_one(_, carry):
        score, sel = carry
        top = score.max(0, keepdims=True)
        first = jnp.where(score == top, n_f, float(n_sb)).min(0, keepdims=True)
        hit = n_f == first
        sel = jnp.where(hit & (top > -jnp.inf), 1.0, sel)
        return jnp.where(hit, -jnp.inf, score), sel

    _, sel = lax.fori_loop(0, min(N_SELECT, n_sb), pick_one, (score, jnp.zeros((n_sb, QB), f32)))
    sel = sel.astype(MXU_DTYPE)

    c_t = lax.broadcasted_iota(jnp.int32, (QB, R), 0)
    r_t = lax.broadcasted_iota(jnp.int32, (QB, R), 1) & (QB - 1)
    causal_off = jnp.where(c_t <= r_t, 0.0, MASK_OFF)

    def put(start, rows, z):
        z_scr[pl.ds(start, rows), :] = z
        m_scr[...] = jnp.maximum(m_scr[...], z.max(0, keepdims=True))

    def sel_off(kt):
        blk = SEL_PER_TILE * kt + (lax.broadcasted_iota(jnp.int32, (QB, n_sb), 0) >= SEL_BLOCK).astype(jnp.int32)
        expand = (lax.broadcasted_iota(jnp.int32, (QB, n_sb), 1) == blk).astype(MXU_DTYPE)
        off = (mxu(expand, sel) - 1.0) * (-MASK_OFF)
        return jnp.concatenate([off] * NSA_GROUP, axis=1)

    m_scr[...] = jnp.full((1, R), NEG, f32)
    KF = FAR_TILES * QB
    n_chunks = i // FAR_TILES + 1
    key_f = lax.broadcasted_iota(jnp.int32, (KF, n_sb), 0)
    n_f2 = lax.broadcasted_iota(jnp.int32, (KF, n_sb), 1)
    tile_f = lax.broadcasted_iota(jnp.int32, (KF, 1), 0) >> QB_SHIFT

    def sel_far(it, carry):
        start = pl.multiple_of(it * KF, KF)
        lg = mxu(ks_ref[0, 0, pl.ds(start, KF), :], qt)
        expand = (n_f2 == it * (KF // SEL_BLOCK) + (key_f >> SEL_SHIFT)).astype(MXU_DTYPE)
        off = (mxu(expand, sel) - 1.0) * (-MASK_OFF)
        off = off + jnp.where(it * FAR_TILES + tile_f < i - 1, 0.0, MASK_OFF)
        put(start, KF, lg + far_bias + jnp.concatenate([off] * NSA_GROUP, axis=1))
        return carry

    lax.fori_loop(0, n_chunks, sel_far, 0)

    @pl.when(i >= 1)
    def _():
        start = pl.multiple_of((i - 1) * QB, QB)
        put(start, QB, mxu(ks_ref[0, 0, pl.ds(start, QB), :], qt) + band_ref[0, 1] + sel_off(i - 1))

    diag = pl.multiple_of(i * QB, QB)
    put(diag, QB, mxu(ks_ref[0, 0, pl.ds(diag, QB), :], qt) + band_ref[0, 0] + (sel_off(i) + causal_off))
    m_sel = m_scr[...]
    l_scr[...] = jnp.zeros((1, R), f32)

    def sel_exp(it, carry):
        start = pl.multiple_of(it * KF, KF)
        e = jnp.exp(z_scr[pl.ds(start, KF), :] - m_sel)
        z_scr[pl.ds(start, KF), :] = e
        l_scr[...] = l_scr[...] + e.sum(0, keepdims=True)
        return carry

    lax.fori_loop(0, n_chunks, sel_exp, 0)
    inv_sel = recip(l_scr[...])
    acc_scr[...] = jnp.zeros_like(acc_scr)

    def sel_pv(it, carry):
        start = pl.multiple_of(it * KF, KF)
        p = (z_scr[pl.ds(start, KF), :] * inv_sel).astype(MXU_DTYPE)
        acc_scr[...] = acc_scr[...] + mxu(vst_ref[0, 0, :, pl.ds(start, KF)], p)
        return carry

    lax.fori_loop(0, n_chunks, sel_pv, 0)
    o_sel = acc_scr[...]

    start_d = pl.multiple_of((i + WIN_TILES) * QB, QB)
    z_d = mxu(kw_ref[0, 0, pl.ds(start_d, QB), :], qt) + band_ref[0, 0] + causal_off
    start_w = pl.multiple_of(i * QB, QB)
    before_start = (lax.broadcasted_iota(jnp.int32, (WINDOW, 1), 0) >> QB_SHIFT) < WIN_TILES - i
    z_w = mxu(kw_ref[0, 0, pl.ds(start_w, WINDOW), :], qt) + winb_ref[0] + jnp.where(before_start, MASK_OFF, 0.0)
    m_win = jnp.maximum(z_d.max(0, keepdims=True), z_w.max(0, keepdims=True))
    e_d = jnp.exp(z_d - m_win)
    e_w = jnp.exp(z_w - m_win)
    inv_win = recip(e_d.sum(0, keepdims=True) + e_w.sum(0, keepdims=True))
    o_win = (mxu(vwt_ref[0, 0, :, pl.ds(start_d, QB)], (e_d * inv_win).astype(MXU_DTYPE))
             + mxu(vwt_ref[0, 0, :, pl.ds(start_w, WINDOW)], (e_w * inv_win).astype(MXU_DTYPE)))
    gt = gt_ref[0, 0, 0]
    o_ref[0, 0, 0] = gt[0:1] * o_cmp + gt[1:2] * o_sel + gt[2:3] * o_win


def _nsa_prompt_attention_t(q, gates, kc, vc, ksel, vsel, kwin, vwin, rel_bias):
    f32 = jnp.float32
    B, L = q.shape[:2]
    G, J, dh, QB, R = NSA_KV, NSA_GROUP, NSA_DH, Q_BLOCK, NSA_ROWS
    assert L % QB == 0 and WINDOW % QB == 0 and QB % SEL_BLOCK == 0
    nq = L // QB
    n_cmp = kc.shape[1]
    n_cmp_pad = L // CMP_STRIDE
    n_sb = L // SEL_BLOCK
    assert n_cmp == n_cmp_pad - 1

    def cols(a):
        x = a.shape[-1]
        return a.reshape(B, nq, QB, G, J, x).transpose(0, 3, 1, 5, 4, 2).reshape(B, G, nq, x, R)

    q_c = cols(q).astype(MXU_DTYPE)
    gt_c = cols(gates)
    pad = ((0, 0), (NEAR_CMP_BACK, n_cmp_pad - n_cmp + NEAR_CMP - NEAR_CMP_BACK), (0, 0), (0, 0))
    kc_p = jnp.pad(kc, pad).transpose(0, 2, 1, 3)
    vc_p = jnp.pad(vc, pad).transpose(0, 2, 1, 3)
    vct = vc_p[:, :, NEAR_CMP_BACK:NEAR_CMP_BACK + n_cmp_pad].transpose(0, 1, 3, 2).astype(MXU_DTYPE)
    ks = ksel.transpose(0, 2, 1, 3).astype(MXU_DTYPE)
    vst = vsel.transpose(0, 2, 3, 1).astype(MXU_DTYPE)
    kw = jnp.pad(kwin.transpose(0, 2, 1, 3), ((0, 0), (0, 0), (WINDOW, 0), (0, 0))).astype(MXU_DTYPE)
    vwt = jnp.pad(vwin.transpose(0, 2, 3, 1), ((0, 0), (0, 0), (0, 0), (WINDOW, 0))).astype(MXU_DTYPE)

    n_dist = MAX_DISTANCE
    by_dist = rel_bias[_t5_bucket(jnp.arange(n_dist))].astype(f32).reshape(n_dist, G, J)
    by_dist = by_dist.transpose(1, 2, 0)

    def table(dist):
        d = jnp.clip(dist, 0, n_dist - 1)
        return by_dist[:, :, d].transpose(0, 2, 1, 3).reshape(G, dist.shape[0], R)

    c = jnp.arange(QB)[:, None]
    r = jnp.arange(QB)[None, :]
    band = jnp.stack([table(r - c), table(r - c + QB)], axis=1)
    u = jnp.arange(NEAR_CMP)[:, None]
    near = table(r - CMP_STRIDE * (u - NEAR_CMP_BACK) - (CMP_BLOCK - 1))
    far = jnp.repeat(by_dist[:, :, n_dist - 1], QB, axis=1).reshape(G, 1, R)
    assert nq % FAR_TILES == 0
    edge = jnp.tile(jnp.where(c > r, 0.0, MASK_OFF), (1, J))[None] + far
    winb = jnp.concatenate([edge] + [jnp.broadcast_to(far, (G, QB, R))] * (WIN_TILES - 2) + [band[:, 1]], axis=1)
    imapt = _importance_map(n_cmp_pad, n_sb).T

    kv_spec = lambda shape: pl.BlockSpec((1, 1) + shape, lambda b, g, i: (b, g, 0, 0))
    out = pl.pallas_call(
        _nsa_prompt_body_t,
        grid=(B, G, nq),
        in_specs=[pl.BlockSpec((1, 1, 1, dh, R), lambda b, g, i: (b, g, i, 0, 0)),
                  pl.BlockSpec((1, 1, 1, 3, R), lambda b, g, i: (b, g, i, 0, 0)),
                  kv_spec((n_cmp_pad + NEAR_CMP, dh)),
                  kv_spec((dh, n_cmp_pad)),
                  kv_spec((n_cmp_pad + NEAR_CMP, dh)),
                  kv_spec((L, dh)), kv_spec((dh, L)), kv_spec((L + WINDOW, dh)), kv_spec((dh, L + WINDOW)),
                  pl.BlockSpec((1, 2, QB, R), lambda b, g, i: (g, 0, 0, 0)),
                  pl.BlockSpec((1, NEAR_CMP, R), lambda b, g, i: (g, 0, 0)),
                  pl.BlockSpec((1, 1, R), lambda b, g, i: (g, 0, 0)),
                  pl.BlockSpec((1, WINDOW, R), lambda b, g, i: (g, 0, 0)),
                  pl.BlockSpec((n_sb, n_cmp_pad), lambda b, g, i: (0, 0))],
        out_specs=pl.BlockSpec((1, 1, 1, dh, R), lambda b, g, i: (b, g, i, 0, 0)),
        out_shape=jax.ShapeDtypeStruct((B, G, nq, dh, R), f32),
        scratch_shapes=[pltpu.VMEM((1, R), f32), pltpu.VMEM((1, R), f32), pltpu.VMEM((dh, R), f32),
                        pltpu.VMEM((L, R), f32)],
        compiler_params=pltpu.CompilerParams(
            dimension_semantics=("parallel", "parallel", "arbitrary"),
            vmem_limit_bytes=VMEM_LIMIT_BYTES),
        name="nsa_prompt",
    )(q_c, gt_c, kc_p, vct, vc_p, ks, vst, kw, vwt, band, near, far, winb, imapt)
    return out.reshape(B, G, nq, dh, J, QB).transpose(0, 2, 5, 1, 4, 3).reshape(B, L, G * J * dh)


PAGE_SIZE = 128
ROW_WIDTH = N_CACHE_STREAMS * NSA_KV * NSA_DH
CMP_WIDTH = 2 * NSA_KV * NSA_DH
CHUNKS_PER_PAGE = PAGE_SIZE // CMP_STRIDE
MAX_PAGES_PER_STEP = 16


def _cmp_partial_body(pt_ref, *refs):
    del pt_ref
    page_refs, w_ref, o_ref = refs[:-2], refs[-2], refs[-1]
    x = jnp.concatenate([r[0] for r in page_refs], axis=0)
    acc = None
    for s in range(CMP_STRIDE):
        part = jnp.dot(x[:, s * ROW_WIDTH:s * ROW_WIDTH + CMP_WIDTH].astype(MXU_DTYPE), w_ref[s],
                       preferred_element_type=jnp.float32)
        acc = part if acc is None else acc + part
    o_ref[0] = acc


def _compress_partials(cache_nsa, layer, page_table, w1):
    f32 = jnp.float32
    depth, n_pool = cache_nsa.shape[:2]
    B, n_pages = page_table.shape
    G, dh = NSA_KV, NSA_DH
    lanes = CMP_STRIDE * ROW_WIDTH
    view = cache_nsa.reshape(depth * n_pool, CHUNKS_PER_PAGE, lanes)
    pg = math.gcd(n_pages, MAX_PAGES_PER_STEP)
    w = w1.reshape(2, 2, CMP_STRIDE, dh, CMP_HID)
    wb = jnp.einsum('nzsdh,nm,gk->sngdzmkh', w, jnp.eye(2, dtype=f32), jnp.eye(G, dtype=f32))
    n_out = 2 * 2 * G * CMP_HID
    wb = wb.reshape(CMP_STRIDE, CMP_WIDTH, n_out).astype(MXU_DTYPE)

    def page_map(b, j, pt, r):
        return (layer * n_pool + pt[b, j * pg + r], 0, 0)

    grid_spec = pltpu.PrefetchScalarGridSpec(
        num_scalar_prefetch=1,
        grid=(B, n_pages // pg),
        in_specs=[pl.BlockSpec((1, CHUNKS_PER_PAGE, lanes), functools.partial(page_map, r=r)) for r in range(pg)]
        + [pl.BlockSpec((CMP_STRIDE, CMP_WIDTH, n_out), lambda b, j, pt: (0, 0, 0))],
        out_specs=pl.BlockSpec((1, pg * CHUNKS_PER_PAGE, n_out), lambda b, j, pt: (b, j, 0)),
    )
    out = pl.pallas_call(
        _cmp_partial_body,
        grid_spec=grid_spec,
        out_shape=jax.ShapeDtypeStruct((B, n_pages * CHUNKS_PER_PAGE, n_out), f32),
        compiler_params=pltpu.CompilerParams(dimension_semantics=("parallel", "arbitrary"),
                                             vmem_limit_bytes=VMEM_LIMIT_BYTES),
        name="compress_partials",
    )(page_table, *([view] * pg), wb)
    return out.reshape(B, n_pages * CHUNKS_PER_PAGE, 2, 2, G, CMP_HID)


def _hdot(a, b):
    return jnp.dot(a, b, preferred_element_type=jnp.float32, precision=lax.Precision.HIGHEST)


def _hdot_nt(a, b):
    return lax.dot_general(a, b, (((1,), (1,)), ((), ())), preferred_element_type=jnp.float32,
                           precision=lax.Precision.HIGHEST)


def _bdot(a, b):
    return jnp.dot(a.astype(MXU_DTYPE), b.astype(MXU_DTYPE), preferred_element_type=jnp.float32)


def _dot3(a, b):
    f32 = jnp.float32
    a_hi = a.astype(MXU_DTYPE)
    b_hi = b.astype(MXU_DTYPE)
    a_lo = (a - a_hi.astype(f32)).astype(MXU_DTYPE)
    b_lo = (b - b_hi.astype(f32)).astype(MXU_DTYPE)
    dot = functools.partial(jnp.dot, preferred_element_type=f32)
    return dot(a_hi, b_hi) + (dot(a_hi, b_lo) + dot(a_lo, b_hi))


def _bdot_nt(a, b):
    return lax.dot_general(a.astype(MXU_DTYPE), b.astype(MXU_DTYPE), (((1,), (1,)), ((), ())),
                           preferred_element_type=jnp.float32)


ROUTER_OUT_LANES = 128


def _router_body(x_ref, w_ref, b_ref, o_ref, *, precise):
    f32 = jnp.float32
    if precise:
        logits = _hdot(x_ref[...], w_ref[...])
    else:
        logits = jnp.dot(x_ref[...].astype(jnp.bfloat16), w_ref[...].astype(jnp.bfloat16),
                         preferred_element_type=f32)
    scores = jax.nn.sigmoid(logits)
    biased = scores + b_ref[...]
    tm = scores.shape[0]
    lane = lax.broadcasted_iota(jnp.int32, (tm, N_EXPERTS), 1).astype(f32)
    e_io = lax.broadcasted_iota(jnp.int32, (tm, N_EXPERTS), 1)
    grp = sum((e_io >= gi * EXPERTS_PER_GROUP).astype(jnp.int32) for gi in range(1, N_EXPERT_GROUPS))

    def top2(vals):
        m1 = vals.max(-1, keepdims=True)
        i1 = jnp.where(vals == m1, lane, float(N_EXPERTS)).min(-1, keepdims=True)
        rest = jnp.where(lane == i1, -jnp.inf, vals)
        m2 = rest.max(-1, keepdims=True)
        i2 = jnp.where(rest == m2, lane, float(N_EXPERTS)).min(-1, keepdims=True)
        return m1, i1, m2, i2

    best = None
    gsel = None
    for gi in range(N_EXPERT_GROUPS):
        m1, _, m2, _ = top2(jnp.where(grp == gi, biased, -jnp.inf))
        gsum = m1 + m2
        if best is None:
            best, gsel = gsum, jnp.zeros_like(gsum, dtype=jnp.int32)
        else:
            better = gsum > best
            best = jnp.where(better, gsum, best)
            gsel = jnp.where(better, gi, gsel)
    _, i1, _, i2 = top2(jnp.where(grp == gsel, biased, -jnp.inf))
    s1 = jnp.where(lane == i1, scores, 0.0).sum(-1, keepdims=True)
    s2 = jnp.where(lane == i2, scores, 0.0).sum(-1, keepdims=True)
    tot = s1 + s2
    col = lax.broadcasted_iota(jnp.int32, (tm, ROUTER_OUT_LANES), 1)
    o_ref[...] = jnp.where(col == 0, i1, jnp.where(col == 1, i2, jnp.where(col == 2, s1 / tot, s2 / tot)))


def _router(x, w_router, router_bias, precise):
    N, D = x.shape
    tm = min(MM_ROW_TILE, N)
    out = pl.pallas_call(
        functools.partial(_router_body, precise=precise),
        grid=(pl.cdiv(N, tm),),
        in_specs=[pl.BlockSpec((tm, D), lambda i: (i, 0)),
                  pl.BlockSpec((D, N_EXPERTS), lambda i: (0, 0)),
                  pl.BlockSpec((1, N_EXPERTS), lambda i: (0, 0))],
        out_specs=pl.BlockSpec((tm, ROUTER_OUT_LANES), lambda i: (i, 0)),
        out_shape=jax.ShapeDtypeStruct((N, ROUTER_OUT_LANES), jnp.float32),
        compiler_params=pltpu.CompilerParams(dimension_semantics=("parallel",),
                                             vmem_limit_bytes=VMEM_LIMIT_BYTES),
        name="router",
    )(x, w_router, router_bias.reshape(1, N_EXPERTS))
    return out[:, 0:2].astype(jnp.int32), out[:, 2:4]


def _tri(c, strict=False, upper=False):
    r = lax.broadcasted_iota(jnp.int32, (c, c), 0)
    s = lax.broadcasted_iota(jnp.int32, (c, c), 1)
    if upper:
        r, s = s, r
    return (r > s) if strict else (r >= s)


def _dn_body(u_ref, gcol_ref, grow_ref, beta_ref, gate_ref, nw_ref, s0_ref, o_ref, s_ref):
    f32 = jnp.float32
    nb, c = u_ref.shape[0], u_ref.shape[1]
    dk, dv = DN_DK, DN_DV

    @pl.when(pl.program_id(0) == 0)
    def _():
        s_ref[...] = s0_ref[...]

    incl = _tri(c)
    strict = _tri(c, strict=True)
    eye = (lax.broadcasted_iota(jnp.int32, (c, c), 0) == lax.broadcasted_iota(jnp.int32, (c, c), 1)).astype(f32)
    nw = nw_ref[...]
    for b in range(nb):
        g_col = gcol_ref[b, 0]
        g_row = grow_ref[b, 0]
        beta_all = beta_ref[b, 0]
        for h in range(DN_HEADS):
            q = u_ref[b, :, h * dk:(h + 1) * dk]
            k = u_ref[b, :, DN_HEADS * dk + h * dk:DN_HEADS * dk + (h + 1) * dk]
            v = u_ref[b, :, 2 * DN_HEADS * dk + h * dv:2 * DN_HEADS * dk + (h + 1) * dv]
            q = q * lax.rsqrt(jnp.sum(q * q, -1, keepdims=True) + EPS) * dk ** -0.5
            k = k * lax.rsqrt(jnp.sum(k * k, -1, keepdims=True) + EPS)
            gc = g_col[:, h:h + 1]
            gr = g_row[h:h + 1, :]
            beta = beta_all[:, h:h + 1]
            dec = jnp.where(incl, jnp.exp(jnp.where(incl, gc - gr, 0.0)), 0.0)
            kb = k * beta
            a = jnp.where(strict, _bdot_nt(kb, k) * dec, 0.0)
            p = -a
            tm = eye + p
            n_sq = max(1, math.ceil(math.log2(c))) - 1
            for _ in range(n_sq):
                p = _dot3(p, p)
                tm = tm + _dot3(tm, p)
            eg = jnp.exp(gc)
            u = _bdot(tm, v * beta)
            w = _bdot(tm, kb * eg)
            s = s_ref[b, h]
            v_new = u - _bdot(w, s)
            attn = _bdot_nt(q, k) * dec
            o = _bdot(q * eg, s) + _bdot(attn, v_new)
            g_last = gc[c - 1:c, :]
            s_ref[b, h] = s * jnp.exp(g_last) + _bdot((k * jnp.exp(g_last - gc)).T, v_new)
            o = o * lax.rsqrt(jnp.mean(o * o, -1, keepdims=True) + EPS) * nw
            o_ref[b, :, h * dv:(h + 1) * dv] = o * jax.nn.silu(gate_ref[b, :, h * dv:(h + 1) * dv])


def _dn_scan(u, g, beta, gate, norm_w, s0):
    f32 = jnp.float32
    B, L, _ = u.shape
    c = min(DN_CHUNK, L)
    assert L % c == 0
    n = L // c
    H = DN_HEADS
    g4 = jnp.cumsum(g.reshape(B, n, c, H), axis=2)
    return pl.pallas_call(
        _dn_body,
        grid=(n,),
        in_specs=[pl.BlockSpec((B, c, DN_CONV_DIM), lambda i: (0, i, 0)),
                  pl.BlockSpec((B, 1, c, H), lambda i: (0, i, 0, 0)),
                  pl.BlockSpec((B, 1, H, c), lambda i: (0, i, 0, 0)),
                  pl.BlockSpec((B, 1, c, H), lambda i: (0, i, 0, 0)),
                  pl.BlockSpec((B, c, H * DN_DV), lambda i: (0, i, 0)),
                  pl.BlockSpec((1, DN_DV), lambda i: (0, 0)),
                  pl.BlockSpec((B, H, DN_DK, DN_DV), lambda i: (0, 0, 0, 0))],
        out_specs=[pl.BlockSpec((B, c, H * DN_DV), lambda i: (0, i, 0)),
                   pl.BlockSpec((B, H, DN_DK, DN_DV), lambda i: (0, 0, 0, 0))],
        out_shape=[jax.ShapeDtypeStruct((B, L, H * DN_DV), f32),
                   jax.ShapeDtypeStruct((B, H, DN_DK, DN_DV), f32)],
        compiler_params=pltpu.CompilerParams(dimension_semantics=("arbitrary",),
                                             vmem_limit_bytes=VMEM_LIMIT_BYTES),
        name="deltanet_scan",
    )(u, g4, g4.transpose(0, 1, 3, 2), beta.reshape(B, n, c, H), gate, norm_w.reshape(1, DN_DV), s0)


HG_SUB = 16


def _hg_body(q_ref, f_ref, i_ref, gate_ref, lb_ref, nw_ref, s0_ref, o_ref, s_ref):
    f32 = jnp.float32
    nb, c = q_ref.shape[0], q_ref.shape[1]
    dk, dv = HG_DK, HG_DV
    sub = min(HG_SUB, c)

    @pl.when(pl.program_id(0) == 0)
    def _():
        s_ref[...] = s0_ref[...]

    lower = _tri(c).astype(f32)
    t_sub = lax.broadcasted_iota(jnp.int32, (sub, 1), 0)
    nw = nw_ref[...]
    for b in range(nb):
        for h in range(HG_HEADS):
            sl = slice(h * dk, (h + 1) * dk)
            q = q_ref[b, :, sl] * dk ** -0.5
            lb = lb_ref[:, sl]
            f = lb + (1.0 - lb) * jax.nn.sigmoid(f_ref[b, :, sl])
            k = 1.0 - f
            v = i_ref[b, :, h * dv:(h + 1) * dv]
            gcum = _hdot(lower, jnp.log(f))
            st = s_ref[b, h]
            o_inter = _bdot_nt(q * jnp.exp(gcum), st)
            v_r = v.astype(MXU_DTYPE).astype(f32)
            for blk in range(c // sub):
                r0 = blk * sub
                q_i = q[r0:r0 + sub]
                g_i = gcum[r0:r0 + sub]
                o_i = o_inter[r0:r0 + sub]
                if blk > 0:
                    g_b = gcum[r0 - 1:r0]
                    a = _hdot_nt(q_i * jnp.exp(g_i - g_b), k[:r0] * jnp.exp(g_b - gcum[:r0]))
                    o_i = o_i + _bdot(a, v[:r0])
                for s in range(sub):
                    seen = t_sub >= s
                    term = q_i * k[r0 + s:r0 + s + 1] * jnp.exp(jnp.where(seen, g_i - gcum[r0 + s:r0 + s + 1], 0.0))
                    a_col = jnp.where(seen, term.sum(-1, keepdims=True), 0.0)
                    o_i = o_i + a_col.astype(MXU_DTYPE).astype(f32) * v_r[r0 + s:r0 + s + 1]
                o_i = o_i * lax.rsqrt(jnp.mean(o_i * o_i, -1, keepdims=True) + EPS) * nw
                o_ref[b, r0:r0 + sub, h * dv:(h + 1) * dv] = o_i * jax.nn.silu(gate_ref[b, r0:r0 + sub, h * dv:(h + 1) * dv])
            g_last = gcum[c - 1:c]
            s_ref[b, h] = st * jnp.exp(g_last) + _bdot(v.T, k * jnp.exp(g_last - gcum))


def _hg_scan(q_raw, f_raw, i_raw, gate, lb, norm_w, s0):
    f32 = jnp.float32
    B, L, _ = q_raw.shape
    c = min(HG_CHUNK, L)
    assert L % c == 0
    n = L // c
    H = HG_HEADS
    tok = lambda w: pl.BlockSpec((B, c, w), lambda i: (0, i, 0))
    o, st = pl.pallas_call(
        _hg_body,
        grid=(n,),
        in_specs=[tok(H * HG_DK), tok(H * HG_DK), tok(H * HG_DV), tok(H * HG_DV),
                  pl.BlockSpec((1, H * HG_DK), lambda i: (0, 0)),
                  pl.BlockSpec((1, HG_DV), lambda i: (0, 0)),
                  pl.BlockSpec((B, H, HG_DV, HG_DK), lambda i: (0, 0, 0, 0))],
        out_specs=[tok(H * HG_DV), pl.BlockSpec((B, H, HG_DV, HG_DK), lambda i: (0, 0, 0, 0))],
        out_shape=[jax.ShapeDtypeStruct((B, L, H * HG_DV), f32),
                   jax.ShapeDtypeStruct((B, H, HG_DV, HG_DK), f32)],
        compiler_params=pltpu.CompilerParams(dimension_semantics=("arbitrary",),
                                             vmem_limit_bytes=VMEM_LIMIT_BYTES),
        name="hgrn2_scan",
    )(q_raw, f_raw, i_raw, gate, lb.reshape(1, H * HG_DK), norm_w.reshape(1, HG_DV), s0.transpose(0, 1, 3, 2))
    return o, st.transpose(0, 1, 3, 2)


def _split(z, sizes):
    return jnp.split(z, np.cumsum(sizes)[:-1].tolist(), axis=-1)


def _layer_norm(x, g, b):
    mu = x.mean(-1, keepdims=True)
    var = jnp.square(x - mu).mean(-1, keepdims=True)
    return (x - mu) * lax.rsqrt(var + EPS) * g + b


def _masked_softmax(logits, mask):
    z = jnp.where(mask, logits.astype(jnp.float32), NEG)
    e = jnp.exp(z - z.max(-1, keepdims=True)) * mask
    s = e.sum(-1, keepdims=True)
    return e / jnp.where(s > 0, s, 1.0)


def _l2norm(a):
    return a * lax.rsqrt(jnp.sum(a * a, -1, keepdims=True) + EPS)


def _gated_rmsnorm(o, gate, w):
    B, L, H, dv = o.shape
    o = o * lax.rsqrt(jnp.mean(o * o, -1, keepdims=True) + EPS) * w
    return o.reshape(B, L, H * dv) * jax.nn.silu(gate)


def _t5_bucket(dist):
    n = jnp.maximum(dist, 0)
    exact = N_BUCKETS // 2
    rel = jnp.log(jnp.maximum(n, 1).astype(jnp.float32) / exact) / math.log(MAX_DISTANCE / exact)
    large = jnp.minimum(exact + (rel * (N_BUCKETS - exact)).astype(jnp.int32), N_BUCKETS - 1)
    return jnp.where(n < exact, n, large)


def _head_bias(dist, rel_bias):
    Q, K = dist.shape
    bias = rel_bias[_t5_bucket(dist)]
    return bias.reshape(Q, K, NSA_KV, NSA_GROUP).transpose(2, 3, 0, 1).astype(jnp.float32)


def _compress(rows, w1, w2, pe):
    B, T, G, dh = rows.shape
    n_chunk = T // CMP_STRIDE
    r = rows[:, :n_chunk * CMP_STRIDE].reshape(B, n_chunk, CMP_STRIDE, G, dh)
    w1h = w1.reshape(2, CMP_STRIDE, dh, CMP_HID)
    lo = jnp.einsum('bcsgd,sdh->bcgh', r, w1h[0])
    hi = jnp.einsum('bcsgd,sdh->bcgh', r, w1h[1])
    h = lo[:, :-1] + hi[:, 1:] + pe.reshape(-1) @ w1
    return jax.nn.silu(h) @ w2


def _importance_map(n_cmp, n_sb):
    c0 = jnp.arange(n_cmp) * CMP_STRIDE
    c1 = c0 + CMP_BLOCK - 1
    b0 = jnp.arange(n_sb) * SEL_BLOCK
    b1 = b0 + SEL_BLOCK - 1
    return ((c0[:, None] <= b1[None, :]) & (c1[:, None] >= b0[None, :])).astype(jnp.float32)


def _nsa_core(q, gates, t, kc, vc, cmp_end, imp_map, n_sb, fetch_sel, kw, vw, w_pos, rel_bias):
    f32 = jnp.float32
    B, Q = q.shape[:2]
    lg = jnp.einsum('bqgjd,bcgd->bgjqc', q, kc).astype(f32) + _head_bias(t[:, None] - cmp_end[None, :], rel_bias)
    p_cmp = _masked_softmax(lg, cmp_end[None, :] <= t[:, None])
    o_cmp = jnp.einsum('bgjqc,bcgd->bqgjd', p_cmp, vc)
    n_top = min(N_SELECT, n_sb)
    imp = jnp.einsum('bgjqc,cn->bgqn', p_cmp, imp_map)
    blk = jnp.arange(n_sb)[None, :]
    cur = (t // SEL_BLOCK)[:, None]
    forced = (blk == 0) | (blk == cur) | (blk == cur - 1)
    score = jnp.where(blk * SEL_BLOCK <= t[:, None], jnp.where(forced, jnp.inf, imp), -jnp.inf)
    top_s, top_i = lax.top_k(score, n_top)
    n_keys = n_top * SEL_BLOCK
    ks, vs = (a.reshape(B, NSA_KV, Q, n_keys, NSA_DH) for a in fetch_sel(top_i))
    s_pos = (top_i[..., None] * SEL_BLOCK + jnp.arange(SEL_BLOCK)).reshape(B, NSA_KV, Q, n_keys)
    s_ok = jnp.repeat(top_s > -jnp.inf, SEL_BLOCK, axis=-1) & (s_pos <= t[:, None])
    table = rel_bias.reshape(N_BUCKETS, NSA_KV, NSA_GROUP).transpose(1, 2, 0)
    bucket = _t5_bucket(t[:, None] - s_pos)
    onehot = (bucket[..., None] == jnp.arange(N_BUCKETS)).astype(f32)
    s_bias = jnp.einsum('bgqmn,gjn->bgjqm', onehot, table.astype(f32), precision=lax.Precision.HIGHEST)
    lg = jnp.einsum('bqgjd,bgqmd->bgjqm', q, ks).astype(f32) + s_bias
    p_sel = _masked_softmax(lg, s_ok[:, :, None])
    o_sel = jnp.einsum('bgjqm,bgqmd->bqgjd', p_sel, vs)
    w_ok = (w_pos[None, :] <= t[:, None]) & (w_pos[None, :] > t[:, None] - WINDOW) & (w_pos[None, :] >= 0)
    lg = jnp.einsum('bqgjd,bkgd->bgjqk', q, kw).astype(f32) + _head_bias(t[:, None] - w_pos[None, :], rel_bias)
    p_win = _masked_softmax(lg, w_ok)
    o_win = jnp.einsum('bgjqk,bkgd->bqgjd', p_win, vw)
    o = gates[..., 0:1] * o_cmp + gates[..., 1:2] * o_sel + gates[..., 2:3] * o_win
    return o.reshape(B, Q, NSA_HEADS * NSA_DH)


def _nsa_mixer(q_raw, kv_raw, gate_raw, past, w_cmp1, w_cmp2, pe_cmp, rel_bias):
    B, L, _ = q_raw.shape
    q = (q_raw * NSA_DH ** -0.5).reshape(B, L, NSA_KV, NSA_GROUP, NSA_DH)
    gates = jax.nn.sigmoid(gate_raw).reshape(B, L, NSA_KV, NSA_GROUP, 3)
    kv = kv_raw.reshape(B, L, N_KV_STREAMS, NSA_KV, NSA_DH)
    new_rows, new_win = kv[:, :, :N_CACHE_STREAMS], kv[:, :, N_CACHE_STREAMS:]
    if past is None:
        kc = _compress(new_rows[:, :, 0], w_cmp1[0], w_cmp2[0], pe_cmp[0])
        vc = _compress(new_rows[:, :, 1], w_cmp1[1], w_cmp2[1], pe_cmp[1])
        o = _nsa_prompt_attention_t(q, gates, kc, vc, new_rows[:, :, 2], new_rows[:, :, 3],
                                    new_win[:, :, 0], new_win[:, :, 1], rel_bias)
        return o, new_rows, new_win[:, -min(WINDOW, L):]

    cache_nsa, layer, page_table, win_buf = past
    assert L == 1 and cache_nsa.shape[2] == PAGE_SIZE
    n_pool = cache_nsa.shape[1]
    n_pages = page_table.shape[1]
    pos0 = n_pages * PAGE_SIZE
    part = _compress_partials(cache_nsa, layer, page_table, w_cmp1)

    def finish(n):
        h = part[:, :-1, 0, n] + part[:, 1:, 1, n] + pe_cmp[n].reshape(-1) @ w_cmp1[n]
        return jax.nn.silu(h) @ w_cmp2[n]

    kc, vc = finish(0), finish(1)
    n_cmp = kc.shape[1]
    cmp_end = jnp.arange(n_cmp) * CMP_STRIDE + CMP_BLOCK - 1
    n_past_blk = pos0 // SEL_BLOCK
    n_sb = n_past_blk + 1
    blk_per_page = PAGE_SIZE // SEL_BLOCK
    blocks = cache_nsa.reshape(cache_nsa.shape[0] * n_pool * blk_per_page, SEL_BLOCK * ROW_WIDTH)
    first = (jnp.arange(SEL_BLOCK) == 0)[:, None]

    def fetch_sel(top_i):
        n_top = top_i.shape[-1]
        idx = jnp.minimum(top_i, n_past_blk - 1)
        page = jnp.take_along_axis(jnp.broadcast_to(page_table[:, None, None, :], (B, NSA_KV, 1, n_pages)),
                                   idx // blk_per_page, axis=-1)
        flat = (layer * n_pool + page) * blk_per_page + idx % blk_per_page
        got = jnp.take(blocks, flat.reshape(-1), axis=0)
        got = got.reshape(B, NSA_KV, 1, n_top, SEL_BLOCK, N_CACHE_STREAMS, NSA_KV, NSA_DH)
        is_new = (top_i == n_past_blk)[..., None, None]
        out = []
        for stream in (2, 3):
            cached = jnp.stack([got[:, g, :, :, :, stream, g] for g in range(NSA_KV)], axis=1)
            fresh = new_rows[:, 0, stream][:, :, None, None, None, :] * first
            out.append(jnp.where(is_new, fresh, cached))
        return out

    wrows = jnp.concatenate([win_buf, new_win], 1)
    n_buf = wrows.shape[1] - L
    o = _nsa_core(q, gates, pos0 + jnp.arange(L), kc, vc, cmp_end, _importance_map(n_cmp, n_sb), n_sb, fetch_sel,
                  wrows[:, :, 0], wrows[:, :, 1], pos0 - n_buf + jnp.arange(n_buf + L), rel_bias)
    return o, new_rows, wrows[:, -min(WINDOW, pos0 + L):]


def _chunk(a, c):
    B, L = a.shape[:2]
    n = -(-L // c)
    a = jnp.pad(a, [(0, 0), (0, n * c - L)] + [(0, 0)] * (a.ndim - 2))
    a = a.reshape((B, n, c) + a.shape[2:])
    return jnp.moveaxis(a, (1, 2), (0, 3))


def _unchunk(o, L):
    n, B, H, c, d = o.shape
    return jnp.moveaxis(o, (0, 3), (1, 2)).reshape(B, n * c, H, d)[:, :L]


def _gated_delta(q, k, v, beta, g, s0):
    f32 = jnp.float32
    L = k.shape[1]
    c = min(DN_CHUNK, L)
    qs, ks, vs = (_chunk(a, c) for a in (q, k, v))
    bs, gs = _chunk(beta, c), _chunk(g, c)
    incl = jnp.tril(jnp.ones((c, c), bool))
    strict = jnp.tril(jnp.ones((c, c), bool), -1)
    eye = jnp.eye(c, dtype=f32)

    def step(S, xs):
        qc, kc, vc, bc, gc = xs
        G = jnp.cumsum(gc, -1)
        dec = jnp.where(incl, jnp.exp(jnp.where(incl, G[..., :, None] - G[..., None, :], 0.0)), 0.0)
        kb = kc * bc[..., None]
        A = jnp.where(strict, jnp.einsum('bhtk,bhsk->bhts', kb, kc) * dec, 0.0)
        Tm = lax.linalg.triangular_solve(eye + A, jnp.broadcast_to(eye, A.shape), left_side=True, lower=True)
        u = Tm @ (vc * bc[..., None])
        w = Tm @ (kb * jnp.exp(G)[..., None])
        v_new = u - w @ S
        attn = jnp.einsum('bhtk,bhsk->bhts', qc, kc) * dec
        o = (qc * jnp.exp(G)[..., None]) @ S + attn @ v_new
        S = S * jnp.exp(G[..., -1])[..., None, None] + jnp.einsum(
            'bhtk,bhtv->bhkv', kc * jnp.exp(G[..., -1:] - G)[..., None], v_new)
        return S, o

    S, o = lax.scan(step, s0, (qs, ks, vs, bs, gs))
    return _unchunk(o, L), S


def _gla_scan(q, k, v, logf, s0):
    L = k.shape[1]
    c = min(HG_CHUNK, L)
    qs, ks, vs, fs = (_chunk(a, c) for a in (q, k, v, logf))
    incl = jnp.tril(jnp.ones((c, c), bool))[:, :, None]

    def step(S, xs):
        qc, kc, vc, fc = xs
        G = jnp.cumsum(fc, axis=-2)
        diff = G[:, :, :, None, :] - G[:, :, None, :, :]
        dec = jnp.where(incl, jnp.exp(jnp.where(incl, diff, 0.0)), 0.0)
        attn = jnp.einsum('bhtk,bhsk,bhtsk->bhts', qc, kc, dec)
        o = jnp.einsum('bhtk,bhkv->bhtv', qc * jnp.exp(G), S) + jnp.einsum('bhts,bhsv->bhtv', attn, vc)
        S = S * jnp.exp(G[:, :, -1])[..., None] + jnp.einsum(
            'bhsk,bhsv->bhkv', kc * jnp.exp(G[:, :, -1:] - G), vc)
        return S, o

    S, o = lax.scan(step, s0, (qs, ks, vs, fs))
    return _unchunk(o, L), S


def _short_conv(u, buf, w):
    L = u.shape[1]
    xc = jnp.concatenate([buf, u], 1)
    y = sum(xc[:, j:j + L] * w[j] for j in range(DN_CONV))
    return jax.nn.silu(y), xc[:, -(DN_CONV - 1):]


def _deltanet_mixer(qkv_raw, a_raw, b_raw, gate_raw, conv_buf, s0, conv_w, a_log, dt_bias, norm_w):
    B, L, _ = qkv_raw.shape
    if conv_buf is None:
        conv_buf = jnp.zeros((B, DN_CONV - 1, DN_CONV_DIM), qkv_raw.dtype)
    if s0 is None:
        s0 = jnp.zeros((B, DN_HEADS, DN_DK, DN_DV), qkv_raw.dtype)
    u, new_buf = _short_conv(qkv_raw, conv_buf, conv_w)
    if L % DN_CHUNK == 0:
        beta = jax.nn.sigmoid(b_raw)
        g = -jnp.exp(a_log) * jax.nn.softplus(a_raw + dt_bias)
        o, s_new = _dn_scan(u, g, beta, gate_raw, norm_w, s0)
        return o, new_buf, s_new
    q, k, v = _split(u, (DN_HEADS * DN_DK, DN_HEADS * DN_DK, DN_HEADS * DN_DV))
    q = _l2norm(q.reshape(B, L, DN_HEADS, DN_DK)) * DN_DK ** -0.5
    k = _l2norm(k.reshape(B, L, DN_HEADS, DN_DK))
    v = v.reshape(B, L, DN_HEADS, DN_DV)
    beta = jax.nn.sigmoid(b_raw)
    g = -jnp.exp(a_log) * jax.nn.softplus(a_raw + dt_bias)
    o, s_new = _gated_delta(q, k, v, beta, g, s0)
    return _gated_rmsnorm(o, gate_raw, norm_w), new_buf, s_new


def _hgrn2_mixer(q_raw, f_raw, i_raw, gate_raw, s0, lb, norm_w):
    B, L, _ = q_raw.shape
    if s0 is None:
        s0 = jnp.zeros((B, HG_HEADS, HG_DK, HG_DV), q_raw.dtype)
    if L % HG_CHUNK == 0:
        return _hg_scan(q_raw, f_raw, i_raw, gate_raw, lb, norm_w, s0)
    q = q_raw.reshape(B, L, HG_HEADS, HG_DK) * HG_DK ** -0.5
    f = lb + (1.0 - lb) * jax.nn.sigmoid(f_raw)
    logf = jnp.log(f).reshape(B, L, HG_HEADS, HG_DK)
    k = (1.0 - f).reshape(B, L, HG_HEADS, HG_DK)
    v = i_raw.reshape(B, L, HG_HEADS, HG_DV)
    o, s_new = _gla_scan(q, k, v, logf, s0)
    return _gated_rmsnorm(o, gate_raw, norm_w), s_new


def _moe(x, w_router, router_bias, wg, wu, wd, precise):
    f32 = jnp.float32
    N, D = x.shape
    eidx, gw = _router(x, w_router, router_bias, precise)
    n_assign = N * TOP_K
    e_flat = eidx.reshape(-1)
    onehot = (e_flat[:, None] == jnp.arange(N_EXPERTS)[None, :]).astype(jnp.int32)
    running = jnp.cumsum(onehot, axis=0)
    counts = running[-1]
    rank = jnp.take_along_axis(running, e_flat[:, None], axis=1)[:, 0] - 1
    padded = (counts + MOE_ROW_BLOCK - 1) // MOE_ROW_BLOCK * MOE_ROW_BLOCK
    pad_end = jnp.cumsum(padded)
    dest = ((pad_end - padded)[e_flat] + rank).astype(jnp.int32)
    n_blocks = -(-n_assign // MOE_ROW_BLOCK) + N_EXPERTS
    n_rows = n_blocks * MOE_ROW_BLOCK
    tok = jnp.full((n_rows,), N, jnp.int32).at[dest].set(jnp.arange(n_assign, dtype=jnp.int32) // TOP_K)
    wt = jnp.zeros((n_rows,), f32).at[dest].set(gw.reshape(-1))
    blk_exp = jnp.minimum(jnp.searchsorted(pad_end, jnp.arange(n_blocks) * MOE_ROW_BLOCK, side='right'),
                          N_EXPERTS - 1).astype(jnp.int32)
    x_pad = jnp.concatenate([x, jnp.zeros((1, D), x.dtype)])
    out = _expert_ffn(x_pad[tok], wt, blk_exp, wg, wu, wd, precise)
    slot = dest.reshape(N, TOP_K)
    return out[slot[:, 0]] + out[slot[:, 1]]


def _trunk_layer(x, ple, lb, past, precise, w_in, w_cmp1, w_cmp2, pe_cmp, dn_conv_w, dn_a_log, dn_dt_bias,
                 dn_norm_w, hg_norm_w, w_branch, w_out, ln1_g, ln1_b, ln2_g, ln2_b, w_router,
                 router_bias, w_exp_gate, w_exp_up, w_exp_down, w_ple, w_ple_gate, rel_bias):
    nsa_past, conv_buf, dn_state, hg_state = past
    B, L, D = x.shape
    x2 = x.reshape(B * L, D)
    mm = functools.partial(_mm, precise=precise)
    z = mm(x2, w_in).reshape(B, L, -1)
    (nsa_q, nsa_kv, nsa_g, dn_qkv, dn_a, dn_b, dn_g,
     hg_q, hg_f, hg_i, hg_g, merge) = _split(z, IN_SPLITS)
    o_a, rows_new, win_new = _nsa_mixer(nsa_q, nsa_kv, nsa_g, nsa_past, w_cmp1, w_cmp2, pe_cmp, rel_bias)
    o_b, conv_new, dn_new = _deltanet_mixer(dn_qkv, dn_a, dn_b, dn_g, conv_buf, dn_state, dn_conv_w,
                                            dn_a_log, dn_dt_bias, dn_norm_w)
    o_c, hg_new = _hgrn2_mixer(hg_q, hg_f, hg_i, hg_g, hg_state, lb, hg_norm_w)
    g = jax.nn.sigmoid(merge).reshape(B * L, 3, D)
    m = (g[:, 0] * mm(o_a.reshape(B * L, -1), w_branch[0]) + g[:, 1] * mm(o_b.reshape(B * L, -1), w_branch[1])
         + g[:, 2] * mm(o_c.reshape(B * L, -1), w_branch[2]))
    x2 = _layer_norm(DEEPNORM_ALPHA * x2 + mm(m, w_out), ln1_g, ln1_b)
    ffn = _moe(x2, w_router, router_bias, w_exp_gate, w_exp_up, w_exp_down, precise)
    ple_term = mm(ple.reshape(B * L, -1), w_ple) * jax.nn.sigmoid(mm(x2, w_ple_gate))
    x2 = _layer_norm(DEEPNORM_ALPHA * x2 + ffn + ple_term, ln2_g, ln2_b)
    return x2.reshape(B, L, D), (rows_new, win_new, conv_new, dn_new, hg_new)


def kernel(x_prompt, x_sample, cache_nsa, state_win, state_dn_conv, state_dn, state_hg, page_table,
           p_prompt, p_sample, w_in, w_cmp1, w_cmp2, pe_cmp, dn_conv_w, dn_a_log, dn_dt_bias, dn_norm_w,
           hg_lb_logits, hg_norm_w, w_branch, w_out, ln1_g, ln1_b, ln2_g, ln2_b, w_router, router_bias,
           w_exp_gate, w_exp_up, w_exp_down, w_ple, w_ple_gate, rel_bias):
    lb = jnp.cumsum(jax.nn.softmax(hg_lb_logits, axis=0), axis=0)
    lb = lb - lb[0:1]
    n_dec = page_table.shape[0]
    y_prompt, y_sample = x_prompt, x_sample
    st_p, st_s = [], []
    for l in range(DEPTH):
        lw = (w_in[l], w_cmp1[l], w_cmp2[l], pe_cmp[l], dn_conv_w[l], dn_a_log[l], dn_dt_bias[l],
              dn_norm_w[l], hg_norm_w[l], w_branch[l], w_out[l], ln1_g[l], ln1_b[l], ln2_g[l], ln2_b[l],
              w_router, router_bias, w_exp_gate[l], w_exp_up[l], w_exp_down[l], w_ple[l], w_ple_gate[l],
              rel_bias)
        y_prompt, s = _trunk_layer(y_prompt, p_prompt[l], lb[l], (None, None, None, None), False, *lw)
        st_p.append(s)
        y_sample, s = _trunk_layer(y_sample, p_sample[l], lb[l],
                                   ((cache_nsa, l, page_table, state_win[l]), state_dn_conv[l], state_dn[l],
                                    state_hg[l]),
                                   False, *lw)
        st_s.append(s)
    outs = [y_prompt, y_sample]
    for i in range(5):
        outs.append(jnp.stack([s[i] for s in st_p]))
        outs.append(jnp.stack([s[i] for s in st_s]))
    return tuple(outs)
```

```python
import functools
import math

import jax
import jax.numpy as jnp
import numpy as np
from jax import lax
from jax.experimental import pallas as pl
from jax.experimental.pallas import tpu as pltpu

D_MODEL = 1024
DEPTH = 4
NSA_HEADS = 8
NSA_KV = 2
NSA_GROUP = NSA_HEADS // NSA_KV
NSA_DH = 64
CMP_STRIDE = 16
CMP_BLOCK = 2 * CMP_STRIDE
CMP_HID = 128
SEL_BLOCK = 64
N_SELECT = 16
WINDOW = 512
Q_BLOCK = 128
N_KV_STREAMS = 6
N_CACHE_STREAMS = 4
N_BUCKETS = 32
MAX_DISTANCE = 128
DN_HEADS = 4
DN_DK = 128
DN_DV = 128
DN_CONV = 4
DN_CHUNK = 64
DN_CONV_DIM = DN_HEADS * (2 * DN_DK + DN_DV)
HG_HEADS = 4
HG_DK = 128
HG_DV = 128
HG_CHUNK = 64
N_EXPERTS = 16
N_EXPERT_GROUPS = 4
EXPERTS_PER_GROUP = N_EXPERTS // N_EXPERT_GROUPS
TOP_K = 2
D_EXPERT = 512
MOE_ROW_BLOCK = 128
DEEPNORM_ALPHA = (2 * DEPTH) ** 0.25
EPS = 1e-6
NEG = -1e30
IN_SPLITS = (NSA_HEADS * NSA_DH, N_KV_STREAMS * NSA_KV * NSA_DH, 3 * NSA_HEADS,
             DN_CONV_DIM, DN_HEADS, DN_HEADS, DN_HEADS * DN_DV,
             HG_HEADS * HG_DK, HG_HEADS * HG_DK, HG_HEADS * HG_DV, HG_HEADS * HG_DV,
             3 * D_MODEL)

VMEM_LIMIT_BYTES = 48 * 1024 * 1024
MM_ROW_TILE = 1024
MM_COL_TILE = 512


def _mm_body(x_ref, w_ref, o_ref, *, precise):
    if precise:
        o_ref[...] = jnp.dot(x_ref[...], w_ref[...], preferred_element_type=jnp.float32,
                             precision=lax.Precision.HIGHEST)
    else:
        o_ref[...] = jnp.dot(x_ref[...].astype(jnp.bfloat16), w_ref[...].astype(jnp.bfloat16),
                             preferred_element_type=jnp.float32)


def _mm(x, w, precise=False):
    M, K = x.shape
    N = w.shape[1]
    tm = min(MM_ROW_TILE, M)
    tn = min(MM_COL_TILE, N)
    return pl.pallas_call(
        functools.partial(_mm_body, precise=precise),
        grid=(pl.cdiv(M, tm), pl.cdiv(N, tn)),
        in_specs=[pl.BlockSpec((tm, K), lambda i, j: (i, 0)),
                  pl.BlockSpec((K, tn), lambda i, j: (0, j))],
        out_specs=pl.BlockSpec((tm, tn), lambda i, j: (i, j)),
        out_shape=jax.ShapeDtypeStruct((M, N), jnp.float32),
        compiler_params=pltpu.CompilerParams(
            dimension_semantics=("parallel", "parallel"),
            vmem_limit_bytes=VMEM_LIMIT_BYTES),
        name="mm",
    )(x, w)


def _expert_body(blk_exp_ref, h_ref, wt_ref, wg_ref, wu_ref, wd_ref, o_ref, *, precise):
    del blk_exp_ref
    if precise:
        dot = functools.partial(jnp.dot, preferred_element_type=jnp.float32,
                                precision=lax.Precision.HIGHEST)
        h = h_ref[...]
        a = jax.nn.silu(dot(h, wg_ref[0])) * dot(h, wu_ref[0])
        o_ref[...] = dot(a, wd_ref[0]) * wt_ref[...]
    else:
        bf = jnp.bfloat16
        dot = functools.partial(jnp.dot, preferred_element_type=jnp.float32)
        h = h_ref[...].astype(bf)
        a = jax.nn.silu(dot(h, wg_ref[0].astype(bf))) * dot(h, wu_ref[0].astype(bf))
        o_ref[...] = dot(a.astype(bf), wd_ref[0].astype(bf)) * wt_ref[...]


def _expert_ffn(h, wt, blk_exp, wg, wu, wd, precise):
    n_rows, D = h.shape
    n_blocks = n_rows // MOE_ROW_BLOCK
    grid_spec = pltpu.PrefetchScalarGridSpec(
        num_scalar_prefetch=1,
        grid=(n_blocks,),
        in_specs=[pl.BlockSpec((MOE_ROW_BLOCK, D), lambda i, be: (i, 0)),
                  pl.BlockSpec((MOE_ROW_BLOCK, 1), lambda i, be: (i, 0)),
                  pl.BlockSpec((1, D, D_EXPERT), lambda i, be: (be[i], 0, 0)),
                  pl.BlockSpec((1, D, D_EXPERT), lambda i, be: (be[i], 0, 0)),
                  pl.BlockSpec((1, D_EXPERT, D), lambda i, be: (be[i], 0, 0))],
        out_specs=pl.BlockSpec((MOE_ROW_BLOCK, D), lambda i, be: (i, 0)),
    )
    return pl.pallas_call(
        functools.partial(_expert_body, precise=precise),
        grid_spec=grid_spec,
        out_shape=jax.ShapeDtypeStruct((n_rows, D), jnp.float32),
        compiler_params=pltpu.CompilerParams(
            dimension_semantics=("arbitrary",),
            vmem_limit_bytes=VMEM_LIMIT_BYTES),
        name="expert_ffn",
    )(blk_exp, h, wt.reshape(n_rows, 1), wg, wu, wd)


NSA_ROWS = NSA_GROUP * Q_BLOCK
NEAR_CMP = 32
NEAR_CMP_BACK = 16
CMP_PER_QBLOCK = Q_BLOCK // CMP_STRIDE
SEL_PER_TILE = Q_BLOCK // SEL_BLOCK
WIN_TILES = WINDOW // Q_BLOCK
MXU_DTYPE = jnp.bfloat16


def _nsa_prompt_body(q_ref, gt_ref, kct_ref, kc_ref, vc_ref, kst_ref, vs_ref, kwt_ref, vw_ref,
                     band_ref, near_ref, far_ref, imap_ref, o_ref, m_scr, l_scr, acc_scr):
    f32 = jnp.float32
    R, QB = NSA_ROWS, Q_BLOCK
    i = pl.program_id(2)
    n_cmp_pad = kct_ref.shape[-1]
    n_sb = imap_ref.shape[-1]
    q = q_ref[0, 0, 0]
    far_bias = far_ref[0]
    r_row = lax.broadcasted_iota(jnp.int32, (R, 1), 0) & (QB - 1)
    t_row = i * QB + r_row

    def mxu(a, b):
        return jnp.dot(a.astype(MXU_DTYPE), b.astype(MXU_DTYPE), preferred_element_type=f32)

    def mxu_nt(a, b):
        return lax.dot_general(a.astype(MXU_DTYPE), b.astype(MXU_DTYPE), (((1,), (1,)), ((), ())),
                               preferred_element_type=f32)

    c0 = CMP_PER_QBLOCK * i - NEAR_CMP_BACK
    lg_far = mxu(q, kct_ref[0, 0]) + far_bias
    mk_far = lax.broadcasted_iota(jnp.int32, (R, n_cmp_pad), 1) < c0
    near_start = pl.multiple_of(CMP_PER_QBLOCK * i, 8)
    k_near = kc_ref[0, 0, pl.ds(near_start, NEAR_CMP), :]
    v_near = vc_ref[0, 0, pl.ds(near_start, NEAR_CMP), :]
    lg_near = mxu_nt(q, k_near) + near_ref[0]
    c_near = c0 + lax.broadcasted_iota(jnp.int32, (R, NEAR_CMP), 1)
    mk_near = (c_near >= 0) & (c_near * CMP_STRIDE + (CMP_BLOCK - 1) <= t_row)
    z_far = jnp.where(mk_far, lg_far, NEG)
    z_near = jnp.where(mk_near, lg_near, NEG)
    mx = jnp.maximum(z_far.max(-1, keepdims=True), z_near.max(-1, keepdims=True))
    e_far = jnp.where(mk_far, jnp.exp(z_far - mx), 0.0)
    e_near = jnp.where(mk_near, jnp.exp(z_near - mx), 0.0)
    s = e_far.sum(-1, keepdims=True) + e_near.sum(-1, keepdims=True)
    s = jnp.where(s > 0, s, 1.0)
    p_far = e_far / s
    p_near = e_near / s
    o_cmp = (mxu(p_far, vc_ref[0, 0, NEAR_CMP_BACK:NEAR_CMP_BACK + n_cmp_pad, :]) + mxu(p_near, v_near))

    cn = c0 + lax.broadcasted_iota(jnp.int32, (NEAR_CMP, n_sb), 0)
    nn = lax.broadcasted_iota(jnp.int32, (NEAR_CMP, n_sb), 1)
    blocks_per_sel = SEL_BLOCK // CMP_STRIDE
    imap_near = ((cn >= blocks_per_sel * nn - 1) & (cn <= blocks_per_sel * nn + blocks_per_sel - 1)).astype(f32)
    imp_rows = mxu(p_far, imap_ref[...]) + mxu(p_near, imap_near)
    imp = imp_rows[0:QB]
    for j in range(1, NSA_GROUP):
        imp = imp + imp_rows[j * QB:(j + 1) * QB]
    n_io = lax.broadcasted_iota(jnp.int32, (QB, n_sb), 1)
    n_f = n_io.astype(f32)
    r_q = lax.broadcasted_iota(jnp.int32, (QB, n_sb), 0)
    t_q = i * QB + r_q
    cur = SEL_PER_TILE * i + (r_q >= SEL_BLOCK).astype(jnp.int32)
    forced = (n_io == 0) | (n_io == cur) | (n_io == cur - 1)
    score = jnp.where(n_io * SEL_BLOCK <= t_q, jnp.where(forced, jnp.inf, imp), -jnp.inf)

    def pick_one(_, carry):
        score, sel = carry
        top = score.max(-1, keepdims=True)
        first = jnp.where(score == top, n_f, float(n_sb)).min(-1, keepdims=True)
        hit = n_f == first
        sel = jnp.where(hit & (top > -jnp.inf), 1.0, sel)
        return jnp.where(hit, -jnp.inf, score), sel

    _, sel = lax.fori_loop(0, min(N_SELECT, n_sb), pick_one, (score, jnp.zeros((QB, n_sb), f32)))

    r_t = lax.broadcasted_iota(jnp.int32, (R, QB), 0) & (QB - 1)
    c_t = lax.broadcasted_iota(jnp.int32, (R, QB), 1)

    def reset():
        m_scr[...] = jnp.full((R, 1), NEG, f32)
        l_scr[...] = jnp.zeros((R, 1), f32)
        acc_scr[...] = jnp.zeros_like(acc_scr)

    def update(lg, mk, v):
        z = jnp.where(mk, lg, NEG) if mk is not None else lg
        m_old = m_scr[...]
        m_new = jnp.maximum(m_old, z.max(-1, keepdims=True))
        p = jnp.exp(z - m_new)
        if mk is not None:
            p = jnp.where(mk, p, 0.0)
        alpha = jnp.exp(m_old - m_new)
        l_scr[...] = alpha * l_scr[...] + p.sum(-1, keepdims=True)
        acc_scr[...] = alpha * acc_scr[...] + mxu(p, v)
        m_scr[...] = m_new

    def result():
        l = l_scr[...]
        return acc_scr[...] / jnp.where(l > 0, l, 1.0)

    def tile(kt_ref, v_ref, kt):
        start = pl.multiple_of(kt * QB, QB)
        return mxu(q, kt_ref[0, 0, :, pl.ds(start, QB)]), v_ref[0, 0, pl.ds(start, QB), :]

    def sel_mask(kt):
        blk = SEL_PER_TILE * kt + (lax.broadcasted_iota(jnp.int32, (n_sb, QB), 1) >= SEL_BLOCK).astype(jnp.int32)
        expand = (lax.broadcasted_iota(jnp.int32, (n_sb, QB), 0) == blk).astype(f32)
        mq = mxu(sel, expand)
        return jnp.concatenate([mq] * NSA_GROUP, axis=0) > 0.5

    reset()

    def sel_far(kt, carry):
        lg, v = tile(kst_ref, vs_ref, kt)
        update(lg + far_bias, sel_mask(kt), v)
        return carry

    lax.fori_loop(0, jnp.maximum(i - 1, 0), sel_far, 0)

    @pl.when(i >= 1)
    def _():
        lg, v = tile(kst_ref, vs_ref, i - 1)
        update(lg + band_ref[0, 1], sel_mask(i - 1), v)

    lg, v = tile(kst_ref, vs_ref, i)
    update(lg + band_ref[0, 0], sel_mask(i) & (c_t <= r_t), v)
    o_sel = result()

    reset()

    @pl.when(i >= WIN_TILES)
    def _():
        lg, v = tile(kwt_ref, vw_ref, i - WIN_TILES)
        update(lg + far_bias, c_t > r_t, v)

    for back in range(WIN_TILES - 1, 1, -1):
        @pl.when(i >= back)
        def _(back=back):
            lg, v = tile(kwt_ref, vw_ref, i - back)
            update(lg + far_bias, None, v)

    @pl.when(i >= 1)
    def _():
        lg, v = tile(kwt_ref, vw_ref, i - 1)
        update(lg + band_ref[0, 1], None, v)

    lg, v = tile(kwt_ref, vw_ref, i)
    update(lg + band_ref[0, 0], c_t <= r_t, v)
    o_win = result()

    gt = gt_ref[0, 0, 0]
    o_ref[0, 0, 0] = gt[:, 0:1] * o_cmp + gt[:, 1:2] * o_sel + gt[:, 2:3] * o_win


def _nsa_prompt_attention(q, gates, kc, vc, ksel, vsel, kwin, vwin, rel_bias):
    f32 = jnp.float32
    B, L = q.shape[:2]
    G, J, dh, QB, R = NSA_KV, NSA_GROUP, NSA_DH, Q_BLOCK, NSA_ROWS
    assert L % QB == 0 and WINDOW % QB == 0 and QB % SEL_BLOCK == 0
    nq = L // QB
    n_cmp = kc.shape[1]
    n_cmp_pad = L // CMP_STRIDE
    n_sb = L // SEL_BLOCK
    assert n_cmp == n_cmp_pad - 1

    def rows(a):
        x = a.shape[-1]
        return a.reshape(B, nq, QB, G, J, x).transpose(0, 3, 1, 4, 2, 5).reshape(B, G, nq, R, x)

    q_r = rows(q).astype(MXU_DTYPE)
    gt_r = rows(gates)
    kc_p = jnp.pad(kc, ((0, 0), (NEAR_CMP_BACK, n_cmp_pad - n_cmp + NEAR_CMP - NEAR_CMP_BACK), (0, 0), (0, 0)))
    vc_p = jnp.pad(vc, ((0, 0), (NEAR_CMP_BACK, n_cmp_pad - n_cmp + NEAR_CMP - NEAR_CMP_BACK), (0, 0), (0, 0)))
    kc_p = kc_p.transpose(0, 2, 1, 3)
    vc_p = vc_p.transpose(0, 2, 1, 3)
    kct = kc_p[:, :, NEAR_CMP_BACK:NEAR_CMP_BACK + n_cmp_pad].transpose(0, 1, 3, 2).astype(MXU_DTYPE)
    kst = ksel.transpose(0, 2, 3, 1).astype(MXU_DTYPE)
    kwt = kwin.transpose(0, 2, 3, 1).astype(MXU_DTYPE)
    vs = vsel.transpose(0, 2, 1, 3).astype(MXU_DTYPE)
    vw = vwin.transpose(0, 2, 1, 3).astype(MXU_DTYPE)

    n_dist = MAX_DISTANCE
    by_dist = rel_bias[_t5_bucket(jnp.arange(n_dist))].astype(f32).reshape(n_dist, G, J)
    by_dist = by_dist.transpose(1, 2, 0)

    def table(dist):
        d = jnp.clip(dist, 0, n_dist - 1)
        return by_dist[:, :, d].reshape(G, R, dist.shape[1])

    r = jnp.arange(QB)[:, None]
    c = jnp.arange(QB)[None, :]
    band = jnp.stack([table(r - c), table(r - c + QB)], axis=1)
    u = jnp.arange(NEAR_CMP)[None, :]
    near = table(r - CMP_STRIDE * (u - NEAR_CMP_BACK) - (CMP_BLOCK - 1))
    far = jnp.repeat(by_dist[:, :, n_dist - 1], QB, axis=1).reshape(G, R, 1)
    imap = _importance_map(n_cmp_pad, n_sb)

    kv_spec = lambda shape: pl.BlockSpec((1, 1) + shape, lambda b, g, i: (b, g, 0, 0))
    out = pl.pallas_call(
        _nsa_prompt_body,
        grid=(B, G, nq),
        in_specs=[pl.BlockSpec((1, 1, 1, R, dh), lambda b, g, i: (b, g, i, 0, 0)),
                  pl.BlockSpec((1, 1, 1, R, 3), lambda b, g, i: (b, g, i, 0, 0)),
                  kv_spec((dh, n_cmp_pad)),
                  kv_spec((n_cmp_pad + NEAR_CMP, dh)),
                  kv_spec((n_cmp_pad + NEAR_CMP, dh)),
                  kv_spec((dh, L)), kv_spec((L, dh)), kv_spec((dh, L)), kv_spec((L, dh)),
                  pl.BlockSpec((1, 2, R, QB), lambda b, g, i: (g, 0, 0, 0)),
                  pl.BlockSpec((1, R, NEAR_CMP), lambda b, g, i: (g, 0, 0)),
                  pl.BlockSpec((1, R, 1), lambda b, g, i: (g, 0, 0)),
                  pl.BlockSpec((n_cmp_pad, n_sb), lambda b, g, i: (0, 0))],
        out_specs=pl.BlockSpec((1, 1, 1, R, dh), lambda b, g, i: (b, g, i, 0, 0)),
        out_shape=jax.ShapeDtypeStruct((B, G, nq, R, dh), f32),
        scratch_shapes=[pltpu.VMEM((R, 1), f32), pltpu.VMEM((R, 1), f32), pltpu.VMEM((R, dh), f32)],
        compiler_params=pltpu.CompilerParams(
            dimension_semantics=("parallel", "parallel", "arbitrary"),
            vmem_limit_bytes=VMEM_LIMIT_BYTES),
        name="nsa_prompt",
    )(q_r, gt_r, kct, kc_p, vc_p, kst, vs, kwt, vw, band, near, far, imap)
    return out.reshape(B, G, nq, J, QB, dh).transpose(0, 2, 4, 1, 3, 5).reshape(B, L, G * J * dh)


MASK_OFF = -1e30
FAR_TILES = 4
SEL_SHIFT = SEL_BLOCK.bit_length() - 1
QB_SHIFT = Q_BLOCK.bit_length() - 1
assert 1 << SEL_SHIFT == SEL_BLOCK and 1 << QB_SHIFT == Q_BLOCK


def _nsa_prompt_body_t(q_ref, gt_ref, kc_ref, vct_ref, vc_ref, ks_ref, vst_ref, kw_ref, vwt_ref,
                       band_ref, near_ref, far_ref, winb_ref, imapt_ref, o_ref, m_scr, l_scr, acc_scr, z_scr):
    f32 = jnp.float32
    R, QB = NSA_ROWS, Q_BLOCK
    i = pl.program_id(2)
    n_cmp_pad = vct_ref.shape[-1]
    n_sb = imapt_ref.shape[0]
    qt = q_ref[0, 0, 0]
    far_bias = far_ref[0]
    r_lane = lax.broadcasted_iota(jnp.int32, (1, R), 1) & (QB - 1)
    t_lane = i * QB + r_lane

    def mxu(a, b):
        return jnp.dot(a.astype(MXU_DTYPE), b.astype(MXU_DTYPE), preferred_element_type=f32)

    def recip(x):
        r0 = 1.0 / x
        return r0 * (2.0 - x * r0)

    c0 = CMP_PER_QBLOCK * i - NEAR_CMP_BACK
    kc_far = kc_ref[0, 0, NEAR_CMP_BACK:NEAR_CMP_BACK + n_cmp_pad, :]
    lg_far = mxu(kc_far, qt) + far_bias
    mk_far = lax.broadcasted_iota(jnp.int32, (n_cmp_pad, R), 0) < c0
    near_start = pl.multiple_of(CMP_PER_QBLOCK * i, 8)
    k_near = kc_ref[0, 0, pl.ds(near_start, NEAR_CMP), :]
    v_near = vc_ref[0, 0, pl.ds(near_start, NEAR_CMP), :]
    lg_near = mxu(k_near, qt) + near_ref[0]
    c_near = c0 + lax.broadcasted_iota(jnp.int32, (NEAR_CMP, R), 0)
    mk_near = (c_near >= 0) & (c_near * CMP_STRIDE + (CMP_BLOCK - 1) <= t_lane)
    z_far = jnp.where(mk_far, lg_far, NEG)
    z_near = jnp.where(mk_near, lg_near, NEG)
    mx = jnp.maximum(z_far.max(0, keepdims=True), z_near.max(0, keepdims=True))
    e_far = jnp.where(mk_far, jnp.exp(z_far - mx), 0.0)
    e_near = jnp.where(mk_near, jnp.exp(z_near - mx), 0.0)
    s = e_far.sum(0, keepdims=True) + e_near.sum(0, keepdims=True)
    inv = recip(jnp.where(s > 0, s, 1.0))
    p_far = (e_far * inv).astype(MXU_DTYPE)
    p_near = (e_near * inv).astype(MXU_DTYPE)
    o_cmp = mxu(vct_ref[0, 0], p_far) + mxu(v_near.T, p_near)

    nn = lax.broadcasted_iota(jnp.int32, (n_sb, NEAR_CMP), 0)
    cn = c0 + lax.broadcasted_iota(jnp.int32, (n_sb, NEAR_CMP), 1)
    per_sel = SEL_BLOCK // CMP_STRIDE
    imapt_near = ((cn >= per_sel * nn - 1) & (cn <= per_sel * nn + per_sel - 1)).astype(f32)
    imp_rows = mxu(imapt_ref[...], p_far) + mxu(imapt_near, p_near)
    imp = imp_rows[:, 0:QB]
    for j in range(1, NSA_GROUP):
        imp = imp + imp_rows[:, j * QB:(j + 1) * QB]
    n_io = lax.broadcasted_iota(jnp.int32, (n_sb, QB), 0)
    n_f = n_io.astype(f32)
    r_q = lax.broadcasted_iota(jnp.int32, (n_sb, QB), 1)
    t_q = i * QB + r_q
    cur = SEL_PER_TILE * i + (r_q >= SEL_BLOCK).astype(jnp.int32)
    forced = (n_io == 0) | (n_io == cur) | (n_io == cur - 1)
    score = jnp.where(n_io * SEL_BLOCK <= t_q, jnp.where(forced, jnp.inf, imp), -jnp.inf)

    def pick_one(_, carry):
        score, sel = carry
        top = score.max(0, keepdims=True)
        first = jnp.where(score == top, n_f, float(n_sb)).min(0, keepdims=True)
        hit = n_f == first
        sel = jnp.where(hit & (top > -jnp.inf), 1.0, sel)
        return jnp.where(hit, -jnp.inf, score), sel

    _, sel = lax.fori_loop(0, min(N_SELECT, n_sb), pick_one, (score, jnp.zeros((n_sb, QB), f32)))
    sel = sel.astype(MXU_DTYPE)

    c_t = lax.broadcasted_iota(jnp.int32, (QB, R), 0)
    r_t = lax.broadcasted_iota(jnp.int32, (QB, R), 1) & (QB - 1)
    causal_off = jnp.where(c_t <= r_t, 0.0, MASK_OFF)

    def put(start, rows, z):
        z_scr[pl.ds(start, rows), :] = z
        m_scr[...] = jnp.maximum(m_scr[...], z.max(0, keepdims=True))

    def sel_off(kt):
        blk = SEL_PER_TILE * kt + (lax.broadcasted_iota(jnp.int32, (QB, n_sb), 0) >= SEL_BLOCK).astype(jnp.int32)
        expand = (lax.broadcasted_iota(jnp.int32, (QB, n_sb), 1) == blk).astype(MXU_DTYPE)
        off = (mxu(expand, sel) - 1.0) * (-MASK_OFF)
        return jnp.concatenate([off] * NSA_GROUP, axis=1)

    m_scr[...] = jnp.full((1, R), NEG, f32)
    KF = FAR_TILES * QB
    n_chunks = i // FAR_TILES + 1
    key_f = lax.broadcasted_iota(jnp.int32, (KF, n_sb), 0)
    n_f2 = lax.broadcasted_iota(jnp.int32, (KF, n_sb), 1)
    tile_f = lax.broadcasted_iota(jnp.int32, (KF, 1), 0) >> QB_SHIFT

    def sel_far(it, carry):
        start = pl.multiple_of(it * KF, KF)
        lg = mxu(ks_ref[0, 0, pl.ds(start, KF), :], qt)
        expand = (n_f2 == it * (KF // SEL_BLOCK) + (key_f >> SEL_SHIFT)).astype(MXU_DTYPE)
        off = (mxu(expand, sel) - 1.0) * (-MASK_OFF)
        off = off + jnp.where(it * FAR_TILES + tile_f < i - 1, 0.0, MASK_OFF)
        put(start, KF, lg + far_bias + jnp.concatenate([off] * NSA_GROUP, axis=1))
        return carry

    lax.fori_loop(0, n_chunks, sel_far, 0)

    @pl.when(i >= 1)
    def _():
        start = pl.multiple_of((i - 1) * QB, QB)
        put(start, QB, mxu(ks_ref[0, 0, pl.ds(start, QB), :], qt) + band_ref[0, 1] + sel_off(i - 1))

    diag = pl.multiple_of(i * QB, QB)
    put(diag, QB, mxu(ks_ref[0, 0, pl.ds(diag, QB), :], qt) + band_ref[0, 0] + (sel_off(i) + causal_off))
    m_sel = m_scr[...]
    l_scr[...] = jnp.zeros((1, R), f32)

    def sel_exp(it, carry):
        start = pl.multiple_of(it * KF, KF)
        e = jnp.exp(z_scr[pl.ds(start, KF), :] - m_sel)
        z_scr[pl.ds(start, KF), :] = e
        l_scr[...] = l_scr[...] + e.sum(0, keepdims=True)
        return carry

    lax.fori_loop(0, n_chunks, sel_exp, 0)
    inv_sel = recip(l_scr[...])
    acc_scr[...] = jnp.zeros_like(acc_scr)

    def sel_pv(it, carry):
        start = pl.multiple_of(it * KF, KF)
        p = (z_scr[pl.ds(start, KF), :] * inv_sel).astype(MXU_DTYPE)
        acc_scr[...] = acc_scr[...] + mxu(vst_ref[0, 0, :, pl.ds(start, KF)], p)
        return carry

    lax.fori_loop(0, n_chunks, sel_pv, 0)
    o_sel = acc_scr[...]

    start_d = pl.multiple_of((i + WIN_TILES) * QB, QB)
    z_d = mxu(kw_ref[0, 0, pl.ds(start_d, QB), :], qt) + band_ref[0, 0] + causal_off
    start_w = pl.multiple_of(i * QB, QB)
    before_start = (lax.broadcasted_iota(jnp.int32, (WINDOW, 1), 0) >> QB_SHIFT) < WIN_TILES - i
    z_w = mxu(kw_ref[0, 0, pl.ds(start_w, WINDOW), :], qt) + winb_ref[0] + jnp.where(before_start, MASK_OFF, 0.0)
    m_win = jnp.maximum(z_d.max(0, keepdims=True), z_w.max(0, keepdims=True))
    e_d = jnp.exp(z_d - m_win)
    e_w = jnp.exp(z_w - m_win)
    inv_win = recip(e_d.sum(0, keepdims=True) + e_w.sum(0, keepdims=True))
    o_win = (mxu(vwt_ref[0, 0, :, pl.ds(start_d, QB)], (e_d * inv_win).astype(MXU_DTYPE))
             + mxu(vwt_ref[0, 0, :, pl.ds(start_w, WINDOW)], (e_w * inv_win).astype(MXU_DTYPE)))
    gt = gt_ref[0, 0, 0]
    o_ref[0, 0, 0] = gt[0:1] * o_cmp + gt[1:2] * o_sel + gt[2:3] * o_win


def _nsa_prompt_attention_t(q, gates, kc, vc, ksel, vsel, kwin, vwin, rel_bias):
    f32 = jnp.float32
    B, L = q.shape[:2]
    G, J, dh, QB, R = NSA_KV, NSA_GROUP, NSA_DH, Q_BLOCK, NSA_ROWS
    assert L % QB == 0 and WINDOW % QB == 0 and QB % SEL_BLOCK == 0
    nq = L // QB
    n_cmp = kc.shape[1]
    n_cmp_pad = L // CMP_STRIDE
    n_sb = L // SEL_BLOCK
    assert n_cmp == n_cmp_pad - 1

    def cols(a):
        x = a.shape[-1]
        return a.reshape(B, nq, QB, G, J, x).transpose(0, 3, 1, 5, 4, 2).reshape(B, G, nq, x, R)

    q_c = cols(q).astype(MXU_DTYPE)
    gt_c = cols(gates)
    pad = ((0, 0), (NEAR_CMP_BACK, n_cmp_pad - n_cmp + NEAR_CMP - NEAR_CMP_BACK), (0, 0), (0, 0))
    kc_p = jnp.pad(kc, pad).transpose(0, 2, 1, 3)
    vc_p = jnp.pad(vc, pad).transpose(0, 2, 1, 3)
    vct = vc_p[:, :, NEAR_CMP_BACK:NEAR_CMP_BACK + n_cmp_pad].transpose(0, 1, 3, 2).astype(MXU_DTYPE)
    ks = ksel.transpose(0, 2, 1, 3).astype(MXU_DTYPE)
    vst = vsel.transpose(0, 2, 3, 1).astype(MXU_DTYPE)
    kw = jnp.pad(kwin.transpose(0, 2, 1, 3), ((0, 0), (0, 0), (WINDOW, 0), (0, 0))).astype(MXU_DTYPE)
    vwt = jnp.pad(vwin.transpose(0, 2, 3, 1), ((0, 0), (0, 0), (0, 0), (WINDOW, 0))).astype(MXU_DTYPE)

    n_dist = MAX_DISTANCE
    by_dist = rel_bias[_t5_bucket(jnp.arange(n_dist))].astype(f32).reshape(n_dist, G, J)
    by_dist = by_dist.transpose(1, 2, 0)

    def table(dist):
        d = jnp.clip(dist, 0, n_dist - 1)
        return by_dist[:, :, d].transpose(0, 2, 1, 3).reshape(G, dist.shape[0], R)

    c = jnp.arange(QB)[:, None]
    r = jnp.arange(QB)[None, :]
    band = jnp.stack([table(r - c), table(r - c + QB)], axis=1)
    u = jnp.arange(NEAR_CMP)[:, None]
    near = table(r - CMP_STRIDE * (u - NEAR_CMP_BACK) - (CMP_BLOCK - 1))
    far = jnp.repeat(by_dist[:, :, n_dist - 1], QB, axis=1).reshape(G, 1, R)
    assert nq % FAR_TILES == 0
    edge = jnp.tile(jnp.where(c > r, 0.0, MASK_OFF), (1, J))[None] + far
    winb = jnp.concatenate([edge] + [jnp.broadcast_to(far, (G, QB, R))] * (WIN_TILES - 2) + [band[:, 1]], axis=1)
    imapt = _importance_map(n_cmp_pad, n_sb).T

    kv_spec = lambda shape: pl.BlockSpec((1, 1) + shape, lambda b, g, i: (b, g, 0, 0))
    out = pl.pallas_call(
        _nsa_prompt_body_t,
        grid=(B, G, nq),
        in_specs=[pl.BlockSpec((1, 1, 1, dh, R), lambda b, g, i: (b, g, i, 0, 0)),
                  pl.BlockSpec((1, 1, 1, 3, R), lambda b, g, i: (b, g, i, 0, 0)),
                  kv_spec((n_cmp_pad + NEAR_CMP, dh)),
                  kv_spec((dh, n_cmp_pad)),
                  kv_spec((n_cmp_pad + NEAR_CMP, dh)),
                  kv_spec((L, dh)), kv_spec((dh, L)), kv_spec((L + WINDOW, dh)), kv_spec((dh, L + WINDOW)),
                  pl.BlockSpec((1, 2, QB, R), lambda b, g, i: (g, 0, 0, 0)),
                  pl.BlockSpec((1, NEAR_CMP, R), lambda b, g, i: (g, 0, 0)),
                  pl.BlockSpec((1, 1, R), lambda b, g, i: (g, 0, 0)),
                  pl.BlockSpec((1, WINDOW, R), lambda b, g, i: (g, 0, 0)),
                  pl.BlockSpec((n_sb, n_cmp_pad), lambda b, g, i: (0, 0))],
        out_specs=pl.BlockSpec((1, 1, 1, dh, R), lambda b, g, i: (b, g, i, 0, 0)),
        out_shape=jax.ShapeDtypeStruct((B, G, nq, dh, R), f32),
        scratch_shapes=[pltpu.VMEM((1, R), f32), pltpu.VMEM((1, R), f32), pltpu.VMEM((dh, R), f32),
                        pltpu.VMEM((L, R), f32)],
        compiler_params=pltpu.CompilerParams(
            dimension_semantics=("parallel", "parallel", "arbitrary"),
            vmem_limit_bytes=VMEM_LIMIT_BYTES),
        name="nsa_prompt",
    )(q_c, gt_c, kc_p, vct, vc_p, ks, vst, kw, vwt, band, near, far, winb, imapt)
    return out.reshape(B, G, nq, dh, J, QB).transpose(0, 2, 5, 1, 4, 3).reshape(B, L, G * J * dh)


PAGE_SIZE = 128
ROW_WIDTH = N_CACHE_STREAMS * NSA_KV * NSA_DH
CMP_WIDTH = 2 * NSA_KV * NSA_DH
CHUNKS_PER_PAGE = PAGE_SIZE // CMP_STRIDE
PAGES_PER_STEP = 16


def _cmp_partial_body(x_ref, w_ref, o_ref):
    pg = x_ref.shape[0]
    chunk_lanes = CMP_STRIDE * ROW_WIDTH
    acc = None
    for s in range(CMP_STRIDE):
        lo = s * ROW_WIDTH
        lhs = jnp.concatenate([x_ref[:, c * chunk_lanes + lo:c * chunk_lanes + lo + CMP_WIDTH]
                               for c in range(CHUNKS_PER_PAGE)], axis=0)
        part = jnp.dot(lhs.astype(MXU_DTYPE), w_ref[s], preferred_element_type=jnp.float32)
        acc = part if acc is None else acc + part
    for c in range(CHUNKS_PER_PAGE):
        o_ref[0, c] = acc[c * pg:(c + 1) * pg]


def _compress_partials(pages, w1):
    f32 = jnp.float32
    n = pages.shape[0]
    G, dh = NSA_KV, NSA_DH
    pg = math.gcd(n, PAGES_PER_STEP)
    w = w1.reshape(2, 2, CMP_STRIDE, dh, CMP_HID)
    wb = jnp.einsum('nzsdh,nm,gk->sngdzmkh', w, jnp.eye(2, dtype=f32), jnp.eye(G, dtype=f32))
    n_out = 2 * 2 * G * CMP_HID
    wb = wb.reshape(CMP_STRIDE, CMP_WIDTH, n_out).astype(MXU_DTYPE)
    out = pl.pallas_call(
        _cmp_partial_body,
        grid=(n // pg,),
        in_specs=[pl.BlockSpec((pg, PAGE_SIZE * ROW_WIDTH), lambda i: (i, 0)),
                  pl.BlockSpec((CMP_STRIDE, CMP_WIDTH, n_out), lambda i: (0, 0, 0))],
        out_specs=pl.BlockSpec((1, CHUNKS_PER_PAGE, pg, n_out), lambda i: (i, 0, 0, 0)),
        out_shape=jax.ShapeDtypeStruct((n // pg, CHUNKS_PER_PAGE, pg, n_out), f32),
        compiler_params=pltpu.CompilerParams(dimension_semantics=("parallel",),
                                             vmem_limit_bytes=VMEM_LIMIT_BYTES),
        name="compress_partials",
    )(pages, wb)
    return out.transpose(0, 2, 1, 3).reshape(n * CHUNKS_PER_PAGE, 2, 2, G, CMP_HID)


def _hdot(a, b):
    return jnp.dot(a, b, preferred_element_type=jnp.float32, precision=lax.Precision.HIGHEST)


def _hdot_nt(a, b):
    return lax.dot_general(a, b, (((1,), (1,)), ((), ())), preferred_element_type=jnp.float32,
                           precision=lax.Precision.HIGHEST)


def _bdot(a, b):
    return jnp.dot(a.astype(MXU_DTYPE), b.astype(MXU_DTYPE), preferred_element_type=jnp.float32)


def _dot3(a, b):
    f32 = jnp.float32
    a_hi = a.astype(MXU_DTYPE)
    b_hi = b.astype(MXU_DTYPE)
    a_lo = (a - a_hi.astype(f32)).astype(MXU_DTYPE)
    b_lo = (b - b_hi.astype(f32)).astype(MXU_DTYPE)
    dot = functools.partial(jnp.dot, preferred_element_type=f32)
    return dot(a_hi, b_hi) + (dot(a_hi, b_lo) + dot(a_lo, b_hi))


def _bdot_nt(a, b):
    return lax.dot_general(a.astype(MXU_DTYPE), b.astype(MXU_DTYPE), (((1,), (1,)), ((), ())),
                           preferred_element_type=jnp.float32)


ROUTER_OUT_LANES = 128


def _router_body(x_ref, w_ref, b_ref, o_ref, *, precise):
    f32 = jnp.float32
    if precise:
        logits = _hdot(x_ref[...], w_ref[...])
    else:
        logits = jnp.dot(x_ref[...].astype(jnp.bfloat16), w_ref[...].astype(jnp.bfloat16),
                         preferred_element_type=f32)
    scores = jax.nn.sigmoid(logits)
    biased = scores + b_ref[...]
    tm = scores.shape[0]
    lane = lax.broadcasted_iota(jnp.int32, (tm, N_EXPERTS), 1).astype(f32)
    e_io = lax.broadcasted_iota(jnp.int32, (tm, N_EXPERTS), 1)
    grp = sum((e_io >= gi * EXPERTS_PER_GROUP).astype(jnp.int32) for gi in range(1, N_EXPERT_GROUPS))

    def top2(vals):
        m1 = vals.max(-1, keepdims=True)
        i1 = jnp.where(vals == m1, lane, float(N_EXPERTS)).min(-1, keepdims=True)
        rest = jnp.where(lane == i1, -jnp.inf, vals)
        m2 = rest.max(-1, keepdims=True)
        i2 = jnp.where(rest == m2, lane, float(N_EXPERTS)).min(-1, keepdims=True)
        return m1, i1, m2, i2

    best = None
    gsel = None
    for gi in range(N_EXPERT_GROUPS):
        m1, _, m2, _ = top2(jnp.where(grp == gi, biased, -jnp.inf))
        gsum = m1 + m2
        if best is None:
            best, gsel = gsum, jnp.zeros_like(gsum, dtype=jnp.int32)
        else:
            better = gsum > best
            best = jnp.where(better, gsum, best)
            gsel = jnp.where(better, gi, gsel)
    _, i1, _, i2 = top2(jnp.where(grp == gsel, biased, -jnp.inf))
    s1 = jnp.where(lane == i1, scores, 0.0).sum(-1, keepdims=True)
    s2 = jnp.where(lane == i2, scores, 0.0).sum(-1, keepdims=True)
    tot = s1 + s2
    col = lax.broadcasted_iota(jnp.int32, (tm, ROUTER_OUT_LANES), 1)
    o_ref[...] = jnp.where(col == 0, i1, jnp.where(col == 1, i2, jnp.where(col == 2, s1 / tot, s2 / tot)))


def _router(x, w_router, router_bias, precise):
    N, D = x.shape
    tm = min(MM_ROW_TILE, N)
    out = pl.pallas_call(
        functools.partial(_router_body, precise=precise),
        grid=(pl.cdiv(N, tm),),
        in_specs=[pl.BlockSpec((tm, D), lambda i: (i, 0)),
                  pl.BlockSpec((D, N_EXPERTS), lambda i: (0, 0)),
                  pl.BlockSpec((1, N_EXPERTS), lambda i: (0, 0))],
        out_specs=pl.BlockSpec((tm, ROUTER_OUT_LANES), lambda i: (i, 0)),
        out_shape=jax.ShapeDtypeStruct((N, ROUTER_OUT_LANES), jnp.float32),
        compiler_params=pltpu.CompilerParams(dimension_semantics=("parallel",),
                                             vmem_limit_bytes=VMEM_LIMIT_BYTES),
        name="router",
    )(x, w_router, router_bias.reshape(1, N_EXPERTS))
    return out[:, 0:2].astype(jnp.int32), out[:, 2:4]


def _tri(c, strict=False, upper=False):
    r = lax.broadcasted_iota(jnp.int32, (c, c), 0)
    s = lax.broadcasted_iota(jnp.int32, (c, c), 1)
    if upper:
        r, s = s, r
    return (r > s) if strict else (r >= s)


def _dn_body(u_ref, gcol_ref, grow_ref, beta_ref, gate_ref, nw_ref, s0_ref, o_ref, s_ref):
    f32 = jnp.float32
    nb, c = u_ref.shape[0], u_ref.shape[1]
    dk, dv = DN_DK, DN_DV

    @pl.when(pl.program_id(0) == 0)
    def _():
        s_ref[...] = s0_ref[...]

    incl = _tri(c)
    strict = _tri(c, strict=True)
    eye = (lax.broadcasted_iota(jnp.int32, (c, c), 0) == lax.broadcasted_iota(jnp.int32, (c, c), 1)).astype(f32)
    nw = nw_ref[...]
    for b in range(nb):
        g_col = gcol_ref[b, 0]
        g_row = grow_ref[b, 0]
        beta_all = beta_ref[b, 0]
        for h in range(DN_HEADS):
            q = u_ref[b, :, h * dk:(h + 1) * dk]
            k = u_ref[b, :, DN_HEADS * dk + h * dk:DN_HEADS * dk + (h + 1) * dk]
            v = u_ref[b, :, 2 * DN_HEADS * dk + h * dv:2 * DN_HEADS * dk + (h + 1) * dv]
            q = q * lax.rsqrt(jnp.sum(q * q, -1, keepdims=True) + EPS) * dk ** -0.5
            k = k * lax.rsqrt(jnp.sum(k * k, -1, keepdims=True) + EPS)
            gc = g_col[:, h:h + 1]
            gr = g_row[h:h + 1, :]
            beta = beta_all[:, h:h + 1]
            dec = jnp.where(incl, jnp.exp(jnp.where(incl, gc - gr, 0.0)), 0.0)
            kb = k * beta
            a = jnp.where(strict, _bdot_nt(kb, k) * dec, 0.0)
            p = -a
            tm = eye + p
            n_sq = max(1, math.ceil(math.log2(c))) - 1
            for _ in range(n_sq):
                p = _dot3(p, p)
                tm = tm + _dot3(tm, p)
            eg = jnp.exp(gc)
            u = _bdot(tm, v * beta)
            w = _bdot(tm, kb * eg)
            s = s_ref[b, h]
            v_new = u - _bdot(w, s)
            attn = _bdot_nt(q, k) * dec
            o = _bdot(q * eg, s) + _bdot(attn, v_new)
            g_last = gc[c - 1:c, :]
            s_ref[b, h] = s * jnp.exp(g_last) + _bdot((k * jnp.exp(g_last - gc)).T, v_new)
            o = o * lax.rsqrt(jnp.mean(o * o, -1, keepdims=True) + EPS) * nw
            o_ref[b, :, h * dv:(h + 1) * dv] = o * jax.nn.silu(gate_ref[b, :, h * dv:(h + 1) * dv])


def _dn_scan(u, g, beta, gate, norm_w, s0):
    f32 = jnp.float32
    B, L, _ = u.shape
    c = min(DN_CHUNK, L)
    assert L % c == 0
    n = L // c
    H = DN_HEADS
    g4 = jnp.cumsum(g.reshape(B, n, c, H), axis=2)
    return pl.pallas_call(
        _dn_body,
        grid=(n,),
        in_specs=[pl.BlockSpec((B, c, DN_CONV_DIM), lambda i: (0, i, 0)),
                  pl.BlockSpec((B, 1, c, H), lambda i: (0, i, 0, 0)),
                  pl.BlockSpec((B, 1, H, c), lambda i: (0, i, 0, 0)),
                  pl.BlockSpec((B, 1, c, H), lambda i: (0, i, 0, 0)),
                  pl.BlockSpec((B, c, H * DN_DV), lambda i: (0, i, 0)),
                  pl.BlockSpec((1, DN_DV), lambda i: (0, 0)),
                  pl.BlockSpec((B, H, DN_DK, DN_DV), lambda i: (0, 0, 0, 0))],
        out_specs=[pl.BlockSpec((B, c, H * DN_DV), lambda i: (0, i, 0)),
                   pl.BlockSpec((B, H, DN_DK, DN_DV), lambda i: (0, 0, 0, 0))],
        out_shape=[jax.ShapeDtypeStruct((B, L, H * DN_DV), f32),
                   jax.ShapeDtypeStruct((B, H, DN_DK, DN_DV), f32)],
        compiler_params=pltpu.CompilerParams(dimension_semantics=("arbitrary",),
                                             vmem_limit_bytes=VMEM_LIMIT_BYTES),
        name="deltanet_scan",
    )(u, g4, g4.transpose(0, 1, 3, 2), beta.reshape(B, n, c, H), gate, norm_w.reshape(1, DN_DV), s0)


HG_SUB = 16


def _hg_body(q_ref, f_ref, i_ref, gate_ref, lb_ref, nw_ref, s0_ref, o_ref, s_ref):
    f32 = jnp.float32
    nb, c = q_ref.shape[0], q_ref.shape[1]
    dk, dv = HG_DK, HG_DV
    sub = min(HG_SUB, c)

    @pl.when(pl.program_id(0) == 0)
    def _():
        s_ref[...] = s0_ref[...]

    lower = _tri(c).astype(f32)
    t_sub = lax.broadcasted_iota(jnp.int32, (sub, 1), 0)
    nw = nw_ref[...]
    for b in range(nb):
        for h in range(HG_HEADS):
            sl = slice(h * dk, (h + 1) * dk)
            q = q_ref[b, :, sl] * dk ** -0.5
            lb = lb_ref[:, sl]
            f = lb + (1.0 - lb) * jax.nn.sigmoid(f_ref[b, :, sl])
            k = 1.0 - f
            v = i_ref[b, :, h * dv:(h + 1) * dv]
            gcum = _hdot(lower, jnp.log(f))
            st = s_ref[b, h]
            o_inter = _bdot_nt(q * jnp.exp(gcum), st)
            v_r = v.astype(MXU_DTYPE).astype(f32)
            for blk in range(c // sub):
                r0 = blk * sub
                q_i = q[r0:r0 + sub]
                g_i = gcum[r0:r0 + sub]
                o_i = o_inter[r0:r0 + sub]
                if blk > 0:
                    g_b = gcum[r0 - 1:r0]
                    a = _hdot_nt(q_i * jnp.exp(g_i - g_b), k[:r0] * jnp.exp(g_b - gcum[:r0]))
                    o_i = o_i + _bdot(a, v[:r0])
                for s in range(sub):
                    seen = t_sub >= s
                    term = q_i * k[r0 + s:r0 + s + 1] * jnp.exp(jnp.where(seen, g_i - gcum[r0 + s:r0 + s + 1], 0.0))
                    a_col = jnp.where(seen, term.sum(-1, keepdims=True), 0.0)
                    o_i = o_i + a_col.astype(MXU_DTYPE).astype(f32) * v_r[r0 + s:r0 + s + 1]
                o_i = o_i * lax.rsqrt(jnp.mean(o_i * o_i, -1, keepdims=True) + EPS) * nw
                o_ref[b, r0:r0 + sub, h * dv:(h + 1) * dv] = o_i * jax.nn.silu(gate_ref[b, r0:r0 + sub, h * dv:(h + 1) * dv])
            g_last = gcum[c - 1:c]
            s_ref[b, h] = st * jnp.exp(g_last) + _bdot(v.T, k * jnp.exp(g_last - gcum))


def _hg_scan(q_raw, f_raw, i_raw, gate, lb, norm_w, s0):
    f32 = jnp.float32
    B, L, _ = q_raw.shape
    c = min(HG_CHUNK, L)
    assert L % c == 0
    n = L // c
    H = HG_HEADS
    tok = lambda w: pl.BlockSpec((B, c, w), lambda i: (0, i, 0))
    o, st = pl.pallas_call(
        _hg_body,
        grid=(n,),
        in_specs=[tok(H * HG_DK), tok(H * HG_DK), tok(H * HG_DV), tok(H * HG_DV),
                  pl.BlockSpec((1, H * HG_DK), lambda i: (0, 0)),
                  pl.BlockSpec((1, HG_DV), lambda i: (0, 0)),
                  pl.BlockSpec((B, H, HG_DV, HG_DK), lambda i: (0, 0, 0, 0))],
        out_specs=[tok(H * HG_DV), pl.BlockSpec((B, H, HG_DV, HG_DK), lambda i: (0, 0, 0, 0))],
        out_shape=[jax.ShapeDtypeStruct((B, L, H * HG_DV), f32),
                   jax.ShapeDtypeStruct((B, H, HG_DV, HG_DK), f32)],
        compiler_params=pltpu.CompilerParams(dimension_semantics=("arbitrary",),
                                             vmem_limit_bytes=VMEM_LIMIT_BYTES),
        name="hgrn2_scan",
    )(q_raw, f_raw, i_raw, gate, lb.reshape(1, H * HG_DK), norm_w.reshape(1, HG_DV), s0.transpose(0, 1, 3, 2))
    return o, st.transpose(0, 1, 3, 2)


def _split(z, sizes):
    return jnp.split(z, np.cumsum(sizes)[:-1].tolist(), axis=-1)


def _layer_norm(x, g, b):
    mu = x.mean(-1, keepdims=True)
    var = jnp.square(x - mu).mean(-1, keepdims=True)
    return (x - mu) * lax.rsqrt(var + EPS) * g + b


def _masked_softmax(logits, mask):
    z = jnp.where(mask, logits.astype(jnp.float32), NEG)
    e = jnp.exp(z - z.max(-1, keepdims=True)) * mask
    s = e.sum(-1, keepdims=True)
    return e / jnp.where(s > 0, s, 1.0)


def _l2norm(a):
    return a * lax.rsqrt(jnp.sum(a * a, -1, keepdims=True) + EPS)


def _gated_rmsnorm(o, gate, w):
    B, L, H, dv = o.shape
    o = o * lax.rsqrt(jnp.mean(o * o, -1, keepdims=True) + EPS) * w
    return o.reshape(B, L, H * dv) * jax.nn.silu(gate)


def _t5_bucket(dist):
    n = jnp.maximum(dist, 0)
    exact = N_BUCKETS // 2
    rel = jnp.log(jnp.maximum(n, 1).astype(jnp.float32) / exact) / math.log(MAX_DISTANCE / exact)
    large = jnp.minimum(exact + (rel * (N_BUCKETS - exact)).astype(jnp.int32), N_BUCKETS - 1)
    return jnp.where(n < exact, n, large)


def _head_bias(dist, rel_bias):
    Q, K = dist.shape
    bias = rel_bias[_t5_bucket(dist)]
    return bias.reshape(Q, K, NSA_KV, NSA_GROUP).transpose(2, 3, 0, 1).astype(jnp.float32)


def _compress(rows, w1, w2, pe):
    B, T, G, dh = rows.shape
    n_chunk = T // CMP_STRIDE
    r = rows[:, :n_chunk * CMP_STRIDE].reshape(B, n_chunk, CMP_STRIDE, G, dh)
    w1h = w1.reshape(2, CMP_STRIDE, dh, CMP_HID)
    lo = jnp.einsum('bcsgd,sdh->bcgh', r, w1h[0])
    hi = jnp.einsum('bcsgd,sdh->bcgh', r, w1h[1])
    h = lo[:, :-1] + hi[:, 1:] + pe.reshape(-1) @ w1
    return jax.nn.silu(h) @ w2


def _importance_map(n_cmp, n_sb):
    c0 = jnp.arange(n_cmp) * CMP_STRIDE
    c1 = c0 + CMP_BLOCK - 1
    b0 = jnp.arange(n_sb) * SEL_BLOCK
    b1 = b0 + SEL_BLOCK - 1
    return ((c0[:, None] <= b1[None, :]) & (c1[:, None] >= b0[None, :])).astype(jnp.float32)


def _nsa_core(q, gates, t, kc, vc, cmp_end, imp_map, n_sb, fetch_sel, kw, vw, w_pos, rel_bias):
    f32 = jnp.float32
    B, Q = q.shape[:2]
    lg = jnp.einsum('bqgjd,bcgd->bgjqc', q, kc).astype(f32) + _head_bias(t[:, None] - cmp_end[None, :], rel_bias)
    p_cmp = _masked_softmax(lg, cmp_end[None, :] <= t[:, None])
    o_cmp = jnp.einsum('bgjqc,bcgd->bqgjd', p_cmp, vc)
    n_top = min(N_SELECT, n_sb)
    imp = jnp.einsum('bgjqc,cn->bgqn', p_cmp, imp_map)
    blk = jnp.arange(n_sb)[None, :]
    cur = (t // SEL_BLOCK)[:, None]
    forced = (blk == 0) | (blk == cur) | (blk == cur - 1)
    score = jnp.where(blk * SEL_BLOCK <= t[:, None], jnp.where(forced, jnp.inf, imp), -jnp.inf)
    top_s, top_i = lax.top_k(score, n_top)
    n_keys = n_top * SEL_BLOCK
    ks, vs = (a.reshape(B, NSA_KV, Q, n_keys, NSA_DH) for a in fetch_sel(top_i))
    s_pos = (top_i[..., None] * SEL_BLOCK + jnp.arange(SEL_BLOCK)).reshape(B, NSA_KV, Q, n_keys)
    s_ok = jnp.repeat(top_s > -jnp.inf, SEL_BLOCK, axis=-1) & (s_pos <= t[:, None])
    table = rel_bias.reshape(N_BUCKETS, NSA_KV, NSA_GROUP).transpose(1, 2, 0)
    bucket = _t5_bucket(t[:, None] - s_pos)
    onehot = (bucket[..., None] == jnp.arange(N_BUCKETS)).astype(f32)
    s_bias = jnp.einsum('bgqmn,gjn->bgjqm', onehot, table.astype(f32), precision=lax.Precision.HIGHEST)
    lg = jnp.einsum('bqgjd,bgqmd->bgjqm', q, ks).astype(f32) + s_bias
    p_sel = _masked_softmax(lg, s_ok[:, :, None])
    o_sel = jnp.einsum('bgjqm,bgqmd->bqgjd', p_sel, vs)
    w_ok = (w_pos[None, :] <= t[:, None]) & (w_pos[None, :] > t[:, None] - WINDOW) & (w_pos[None, :] >= 0)
    lg = jnp.einsum('bqgjd,bkgd->bgjqk', q, kw).astype(f32) + _head_bias(t[:, None] - w_pos[None, :], rel_bias)
    p_win = _masked_softmax(lg, w_ok)
    o_win = jnp.einsum('bgjqk,bkgd->bqgjd', p_win, vw)
    o = gates[..., 0:1] * o_cmp + gates[..., 1:2] * o_sel + gates[..., 2:3] * o_win
    return o.reshape(B, Q, NSA_HEADS * NSA_DH)


def _nsa_mixer(q_raw, kv_raw, gate_raw, past, w_cmp1, w_cmp2, pe_cmp, rel_bias):
    B, L, _ = q_raw.shape
    q = (q_raw * NSA_DH ** -0.5).reshape(B, L, NSA_KV, NSA_GROUP, NSA_DH)
    gates = jax.nn.sigmoid(gate_raw).reshape(B, L, NSA_KV, NSA_GROUP, 3)
    kv = kv_raw.reshape(B, L, N_KV_STREAMS, NSA_KV, NSA_DH)
    new_rows, new_win = kv[:, :, :N_CACHE_STREAMS], kv[:, :, N_CACHE_STREAMS:]
    if past is None:
        kc = _compress(new_rows[:, :, 0], w_cmp1[0], w_cmp2[0], pe_cmp[0])
        vc = _compress(new_rows[:, :, 1], w_cmp1[1], w_cmp2[1], pe_cmp[1])
        o = _nsa_prompt_attention_t(q, gates, kc, vc, new_rows[:, :, 2], new_rows[:, :, 3],
                                    new_win[:, :, 0], new_win[:, :, 1], rel_bias)
        return o, new_rows, new_win[:, -min(WINDOW, L):]

    cache_nsa, layer, page_table, win_buf = past
    assert L == 1 and cache_nsa.shape[2] == PAGE_SIZE
    n_pool = cache_nsa.shape[1]
    n_pages = page_table.shape[1]
    pos0 = n_pages * PAGE_SIZE
    pages = jnp.take(cache_nsa[layer].reshape(n_pool, PAGE_SIZE * ROW_WIDTH), page_table.reshape(-1),
                     axis=0).astype(MXU_DTYPE)
    part = _compress_partials(pages, w_cmp1)
    part = part.reshape((B, n_pages * CHUNKS_PER_PAGE) + part.shape[1:])

    def finish(n):
        h = part[:, :-1, 0, n] + part[:, 1:, 1, n] + pe_cmp[n].reshape(-1) @ w_cmp1[n]
        return jax.nn.silu(h) @ w_cmp2[n]

    kc, vc = finish(0), finish(1)
    n_cmp = kc.shape[1]
    cmp_end = jnp.arange(n_cmp) * CMP_STRIDE + CMP_BLOCK - 1
    n_past_blk = pos0 // SEL_BLOCK
    n_sb = n_past_blk + 1
    blk_per_page = PAGE_SIZE // SEL_BLOCK
    first = (jnp.arange(SEL_BLOCK) == 0)[:, None]

    def fetch_sel(top_i):
        n_top = top_i.shape[-1]
        idx = jnp.minimum(top_i, n_past_blk - 1)
        row = jnp.arange(B)[:, None, None, None] * n_pages + idx // blk_per_page
        got = jnp.take(pages, row.reshape(-1), axis=0)
        got = got.reshape(B, NSA_KV, 1, n_top, blk_per_page, SEL_BLOCK, N_CACHE_STREAMS, NSA_KV, NSA_DH)
        half = (idx % blk_per_page)[..., None, None]
        is_new = (top_i == n_past_blk)[..., None, None]
        out = []
        for stream in (2, 3):
            cached = jnp.stack([got[:, g, :, :, :, :, stream, g] for g in range(NSA_KV)], axis=1)
            block = cached[:, :, :, :, 0]
            for h in range(1, blk_per_page):
                block = jnp.where(half == h, cached[:, :, :, :, h], block)
            fresh = (new_rows[:, 0, stream][:, :, None, None, None, :] * first).astype(block.dtype)
            out.append(jnp.where(is_new, fresh, block))
        return out

    wrows = jnp.concatenate([win_buf, new_win], 1)
    n_buf = wrows.shape[1] - L
    o = _nsa_core(q, gates, pos0 + jnp.arange(L), kc, vc, cmp_end, _importance_map(n_cmp, n_sb), n_sb, fetch_sel,
                  wrows[:, :, 0], wrows[:, :, 1], pos0 - n_buf + jnp.arange(n_buf + L), rel_bias)
    return o, new_rows, wrows[:, -min(WINDOW, pos0 + L):]


def _chunk(a, c):
    B, L = a.shape[:2]
    n = -(-L // c)
    a = jnp.pad(a, [(0, 0), (0, n * c - L)] + [(0, 0)] * (a.ndim - 2))
    a = a.reshape((B, n, c) + a.shape[2:])
    return jnp.moveaxis(a, (1, 2), (0, 3))


def _unchunk(o, L):
    n, B, H, c, d = o.shape
    return jnp.moveaxis(o, (0, 3), (1, 2)).reshape(B, n * c, H, d)[:, :L]


def _gated_delta(q, k, v, beta, g, s0):
    f32 = jnp.float32
    L = k.shape[1]
    c = min(DN_CHUNK, L)
    qs, ks, vs = (_chunk(a, c) for a in (q, k, v))
    bs, gs = _chunk(beta, c), _chunk(g, c)
    incl = jnp.tril(jnp.ones((c, c), bool))
    strict = jnp.tril(jnp.ones((c, c), bool), -1)
    eye = jnp.eye(c, dtype=f32)

    def step(S, xs):
        qc, kc, vc, bc, gc = xs
        G = jnp.cumsum(gc, -1)
        dec = jnp.where(incl, jnp.exp(jnp.where(incl, G[..., :, None] - G[..., None, :], 0.0)), 0.0)
        kb = kc * bc[..., None]
        A = jnp.where(strict, jnp.einsum('bhtk,bhsk->bhts', kb, kc) * dec, 0.0)
        Tm = lax.linalg.triangular_solve(eye + A, jnp.broadcast_to(eye, A.shape), left_side=True, lower=True)
        u = Tm @ (vc * bc[..., None])
        w = Tm @ (kb * jnp.exp(G)[..., None])
        v_new = u - w @ S
        attn = jnp.einsum('bhtk,bhsk->bhts', qc, kc) * dec
        o = (qc * jnp.exp(G)[..., None]) @ S + attn @ v_new
        S = S * jnp.exp(G[..., -1])[..., None, None] + jnp.einsum(
            'bhtk,bhtv->bhkv', kc * jnp.exp(G[..., -1:] - G)[..., None], v_new)
        return S, o

    S, o = lax.scan(step, s0, (qs, ks, vs, bs, gs))
    return _unchunk(o, L), S


def _gla_scan(q, k, v, logf, s0):
    L = k.shape[1]
    c = min(HG_CHUNK, L)
    qs, ks, vs, fs = (_chunk(a, c) for a in (q, k, v, logf))
    incl = jnp.tril(jnp.ones((c, c), bool))[:, :, None]

    def step(S, xs):
        qc, kc, vc, fc = xs
        G = jnp.cumsum(fc, axis=-2)
        diff = G[:, :, :, None, :] - G[:, :, None, :, :]
        dec = jnp.where(incl, jnp.exp(jnp.where(incl, diff, 0.0)), 0.0)
        attn = jnp.einsum('bhtk,bhsk,bhtsk->bhts', qc, kc, dec)
        o = jnp.einsum('bhtk,bhkv->bhtv', qc * jnp.exp(G), S) + jnp.einsum('bhts,bhsv->bhtv', attn, vc)
        S = S * jnp.exp(G[:, :, -1])[..., None] + jnp.einsum(
            'bhsk,bhsv->bhkv', kc * jnp.exp(G[:, :, -1:] - G), vc)
        return S, o

    S, o = lax.scan(step, s0, (qs, ks, vs, fs))
    return _unchunk(o, L), S


def _short_conv(u, buf, w):
    L = u.shape[1]
    xc = jnp.concatenate([buf, u], 1)
    y = sum(xc[:, j:j + L] * w[j] for j in range(DN_CONV))
    return jax.nn.silu(y), xc[:, -(DN_CONV - 1):]


def _deltanet_mixer(qkv_raw, a_raw, b_raw, gate_raw, conv_buf, s0, conv_w, a_log, dt_bias, norm_w):
    B, L, _ = qkv_raw.shape
    if conv_buf is None:
        conv_buf = jnp.zeros((B, DN_CONV - 1, DN_CONV_DIM), qkv_raw.dtype)
    if s0 is None:
        s0 = jnp.zeros((B, DN_HEADS, DN_DK, DN_DV), qkv_raw.dtype)
    u, new_buf = _short_conv(qkv_raw, conv_buf, conv_w)
    if L % DN_CHUNK == 0:
        beta = jax.nn.sigmoid(b_raw)
        g = -jnp.exp(a_log) * jax.nn.softplus(a_raw + dt_bias)
        o, s_new = _dn_scan(u, g, beta, gate_raw, norm_w, s0)
        return o, new_buf, s_new
    q, k, v = _split(u, (DN_HEADS * DN_DK, DN_HEADS * DN_DK, DN_HEADS * DN_DV))
    q = _l2norm(q.reshape(B, L, DN_HEADS, DN_DK)) * DN_DK ** -0.5
    k = _l2norm(k.reshape(B, L, DN_HEADS, DN_DK))
    v = v.reshape(B, L, DN_HEADS, DN_DV)
    beta = jax.nn.sigmoid(b_raw)
    g = -jnp.exp(a_log) * jax.nn.softplus(a_raw + dt_bias)
    o, s_new = _gated_delta(q, k, v, beta, g, s0)
    return _gated_rmsnorm(o, gate_raw, norm_w), new_buf, s_new


def _hgrn2_mixer(q_raw, f_raw, i_raw, gate_raw, s0, lb, norm_w):
    B, L, _ = q_raw.shape
    if s0 is None:
        s0 = jnp.zeros((B, HG_HEADS, HG_DK, HG_DV), q_raw.dtype)
    if L % HG_CHUNK == 0:
        return _hg_scan(q_raw, f_raw, i_raw, gate_raw, lb, norm_w, s0)
    q = q_raw.reshape(B, L, HG_HEADS, HG_DK) * HG_DK ** -0.5
    f = lb + (1.0 - lb) * jax.nn.sigmoid(f_raw)
    logf = jnp.log(f).reshape(B, L, HG_HEADS, HG_DK)
    k = (1.0 - f).reshape(B, L, HG_HEADS, HG_DK)
    v = i_raw.reshape(B, L, HG_HEADS, HG_DV)
    o, s_new = _gla_scan(q, k, v, logf, s0)
    return _gated_rmsnorm(o, gate_raw, norm_w), s_new


def _moe(x, w_router, router_bias, wg, wu, wd, precise):
    f32 = jnp.float32
    N, D = x.shape
    eidx, gw = _router(x, w_router, router_bias, precise)
    n_assign = N * TOP_K
    e_flat = eidx.reshape(-1)
    onehot = (e_flat[:, None] == jnp.arange(N_EXPERTS)[None, :]).astype(jnp.int32)
    running = jnp.cumsum(onehot, axis=0)
    counts = running[-1]
    rank = jnp.take_along_axis(running, e_flat[:, None], axis=1)[:, 0] - 1
    padded = (counts + MOE_ROW_BLOCK - 1) // MOE_ROW_BLOCK * MOE_ROW_BLOCK
    pad_end = jnp.cumsum(padded)
    dest = ((pad_end - padded)[e_flat] + rank).astype(jnp.int32)
    n_blocks = -(-n_assign // MOE_ROW_BLOCK) + N_EXPERTS
    n_rows = n_blocks * MOE_ROW_BLOCK
    tok = jnp.full((n_rows,), N, jnp.int32).at[dest].set(jnp.arange(n_assign, dtype=jnp.int32) // TOP_K)
    wt = jnp.zeros((n_rows,), f32).at[dest].set(gw.reshape(-1))
    blk_exp = jnp.minimum(jnp.searchsorted(pad_end, jnp.arange(n_blocks) * MOE_ROW_BLOCK, side='right'),
                          N_EXPERTS - 1).astype(jnp.int32)
    x_pad = jnp.concatenate([x, jnp.zeros((1, D), x.dtype)])
    out = _expert_ffn(x_pad[tok], wt, blk_exp, wg, wu, wd, precise)
    slot = dest.reshape(N, TOP_K)
    return out[slot[:, 0]] + out[slot[:, 1]]


def _trunk_layer(x, ple, lb, past, precise, w_in, w_cmp1, w_cmp2, pe_cmp, dn_conv_w, dn_a_log, dn_dt_bias,
                 dn_norm_w, hg_norm_w, w_branch, w_out, ln1_g, ln1_b, ln2_g, ln2_b, w_router,
                 router_bias, w_exp_gate, w_exp_up, w_exp_down, w_ple, w_ple_gate, rel_bias):
    nsa_past, conv_buf, dn_state, hg_state = past
    B, L, D = x.shape
    x2 = x.reshape(B * L, D)
    mm = functools.partial(_mm, precise=precise)
    z = mm(x2, w_in).reshape(B, L, -1)
    (nsa_q, nsa_kv, nsa_g, dn_qkv, dn_a, dn_b, dn_g,
     hg_q, hg_f, hg_i, hg_g, merge) = _split(z, IN_SPLITS)
    o_a, rows_new, win_new = _nsa_mixer(nsa_q, nsa_kv, nsa_g, nsa_past, w_cmp1, w_cmp2, pe_cmp, rel_bias)
    o_b, conv_new, dn_new = _deltanet_mixer(dn_qkv, dn_a, dn_b, dn_g, conv_buf, dn_state, dn_conv_w,
                                            dn_a_log, dn_dt_bias, dn_norm_w)
    o_c, hg_new = _hgrn2_mixer(hg_q, hg_f, hg_i, hg_g, hg_state, lb, hg_norm_w)
    g = jax.nn.sigmoid(merge).reshape(B * L, 3, D)
    m = (g[:, 0] * mm(o_a.reshape(B * L, -1), w_branch[0]) + g[:, 1] * mm(o_b.reshape(B * L, -1), w_branch[1])
         + g[:, 2] * mm(o_c.reshape(B * L, -1), w_branch[2]))
    x2 = _layer_norm(DEEPNORM_ALPHA * x2 + mm(m, w_out), ln1_g, ln1_b)
    ffn = _moe(x2, w_router, router_bias, w_exp_gate, w_exp_up, w_exp_down, precise)
    ple_term = mm(ple.reshape(B * L, -1), w_ple) * jax.nn.sigmoid(mm(x2, w_ple_gate))
    x2 = _layer_norm(DEEPNORM_ALPHA * x2 + ffn + ple_term, ln2_g, ln2_b)
    return x2.reshape(B, L, D), (rows_new, win_new, conv_new, dn_new, hg_new)


def kernel(x_prompt, x_sample, cache_nsa, state_win, state_dn_conv, state_dn, state_hg, page_table,
           p_prompt, p_sample, w_in, w_cmp1, w_cmp2, pe_cmp, dn_conv_w, dn_a_log, dn_dt_bias, dn_norm_w,
           hg_lb_logits, hg_norm_w, w_branch, w_out, ln1_g, ln1_b, ln2_g, ln2_b, w_router, router_bias,
           w_exp_gate, w_exp_up, w_exp_down, w_ple, w_ple_gate, rel_bias):
    lb = jnp.cumsum(jax.nn.softmax(hg_lb_logits, axis=0), axis=0)
    lb = lb - lb[0:1]
    n_dec = page_table.shape[0]
    y_prompt, y_sample = x_prompt, x_sample
    st_p, st_s = [], []
    for l in range(DEPTH):
        lw = (w_in[l], w_cmp1[l], w_cmp2[l], pe_cmp[l], dn_conv_w[l], dn_a_log[l], dn_dt_bias[l],
              dn_norm_w[l], hg_norm_w[l], w_branch[l], w_out[l], ln1_g[l], ln1_b[l], ln2_g[l], ln2_b[l],
              w_router, router_bias, w_exp_gate[l], w_exp_up[l], w_exp_down[l], w_ple[l], w_ple_gate[l],
              rel_bias)
        y_prompt, s = _trunk_layer(y_prompt, p_prompt[l], lb[l], (None, None, None, None), False, *lw)
        st_p.append(s)
        y_sample, s = _trunk_layer(y_sample, p_sample[l], lb[l],
                                   ((cache_nsa, l, page_table, state_win[l]), state_dn_conv[l], state_dn[l],
                                    state_hg[l]),
                                   False, *lw)
        st_s.append(s)
    outs = [y_prompt, y_sample]
    for i in range(5):
        outs.append(jnp.stack([s[i] for s in st_p]))
        outs.append(jnp.stack([s[i] for s in st_s]))
    return tuple(outs)
```

```python
import functools
import math

import jax
import jax.numpy as jnp
import numpy as np
from jax import lax
from jax.experimental import pallas as pl
from jax.experimental.pallas import tpu as pltpu

D_MODEL = 1024
DEPTH = 4
NSA_HEADS = 8
NSA_KV = 2
NSA_GROUP = NSA_HEADS // NSA_KV
NSA_DH = 64
CMP_STRIDE = 16
CMP_BLOCK = 2 * CMP_STRIDE
CMP_HID = 128
SEL_BLOCK = 64
N_SELECT = 16
WINDOW = 512
Q_BLOCK = 128
N_KV_STREAMS = 6
N_CACHE_STREAMS = 4
N_BUCKETS = 32
MAX_DISTANCE = 128
DN_HEADS = 4
DN_DK = 128
DN_DV = 128
DN_CONV = 4
DN_CHUNK = 64
DN_CONV_DIM = DN_HEADS * (2 * DN_DK + DN_DV)
HG_HEADS = 4
HG_DK = 128
HG_DV = 128
HG_CHUNK = 64
N_EXPERTS = 16
N_EXPERT_GROUPS = 4
EXPERTS_PER_GROUP = N_EXPERTS // N_EXPERT_GROUPS
TOP_K = 2
D_EXPERT = 512
MOE_ROW_BLOCK = 128
DEEPNORM_ALPHA = (2 * DEPTH) ** 0.25
EPS = 1e-6
NEG = -1e30
IN_SPLITS = (NSA_HEADS * NSA_DH, N_KV_STREAMS * NSA_KV * NSA_DH, 3 * NSA_HEADS,
             DN_CONV_DIM, DN_HEADS, DN_HEADS, DN_HEADS * DN_DV,
             HG_HEADS * HG_DK, HG_HEADS * HG_DK, HG_HEADS * HG_DV, HG_HEADS * HG_DV,
             3 * D_MODEL)

VMEM_LIMIT_BYTES = 48 * 1024 * 1024
MM_ROW_TILE = 1024
MM_COL_TILE = 512


def _mm_body(x_ref, w_ref, o_ref, *, precise):
    if precise:
        o_ref[...] = jnp.dot(x_ref[...], w_ref[...], preferred_element_type=jnp.float32,
                             precision=lax.Precision.HIGHEST)
    else:
        o_ref[...] = jnp.dot(x_ref[...].astype(jnp.bfloat16), w_ref[...].astype(jnp.bfloat16),
                             preferred_element_type=jnp.float32)


def _mm(x, w, precise=False):
    M, K = x.shape
    N = w.shape[1]
    tm = min(MM_ROW_TILE, M)
    tn = min(MM_COL_TILE, N)
    return pl.pallas_call(
        functools.partial(_mm_body, precise=precise),
        grid=(pl.cdiv(M, tm), pl.cdiv(N, tn)),
        in_specs=[pl.BlockSpec((tm, K), lambda i, j: (i, 0)),
                  pl.BlockSpec((K, tn), lambda i, j: (0, j))],
        out_specs=pl.BlockSpec((tm, tn), lambda i, j: (i, j)),
        out_shape=jax.ShapeDtypeStruct((M, N), jnp.float32),
        compiler_params=pltpu.CompilerParams(
            dimension_semantics=("parallel", "parallel"),
            vmem_limit_bytes=VMEM_LIMIT_BYTES),
        name="mm",
    )(x, w)


def _expert_body(blk_exp_ref, h_ref, wt_ref, wg_ref, wu_ref, wd_ref, o_ref, *, precise):
    del blk_exp_ref
    if precise:
        dot = functools.partial(jnp.dot, preferred_element_type=jnp.float32,
                                precision=lax.Precision.HIGHEST)
        h = h_ref[...]
        a = jax.nn.silu(dot(h, wg_ref[0])) * dot(h, wu_ref[0])
        o_ref[...] = dot(a, wd_ref[0]) * wt_ref[...]
    else:
        bf = jnp.bfloat16
        dot = functools.partial(jnp.dot, preferred_element_type=jnp.float32)
        h = h_ref[...].astype(bf)
        a = jax.nn.silu(dot(h, wg_ref[0].astype(bf))) * dot(h, wu_ref[0].astype(bf))
        o_ref[...] = dot(a.astype(bf), wd_ref[0].astype(bf)) * wt_ref[...]


def _expert_ffn(h, wt, blk_exp, wg, wu, wd, precise):
    n_rows, D = h.shape
    n_blocks = n_rows // MOE_ROW_BLOCK
    grid_spec = pltpu.PrefetchScalarGridSpec(
        num_scalar_prefetch=1,
        grid=(n_blocks,),
        in_specs=[pl.BlockSpec((MOE_ROW_BLOCK, D), lambda i, be: (i, 0)),
                  pl.BlockSpec((MOE_ROW_BLOCK, 1), lambda i, be: (i, 0)),
                  pl.BlockSpec((1, D, D_EXPERT), lambda i, be: (be[i], 0, 0)),
                  pl.BlockSpec((1, D, D_EXPERT), lambda i, be: (be[i], 0, 0)),
                  pl.BlockSpec((1, D_EXPERT, D), lambda i, be: (be[i], 0, 0))],
        out_specs=pl.BlockSpec((MOE_ROW_BLOCK, D), lambda i, be: (i, 0)),
    )
    return pl.pallas_call(
        functools.partial(_expert_body, precise=precise),
        grid_spec=grid_spec,
        out_shape=jax.ShapeDtypeStruct((n_rows, D), jnp.float32),
        compiler_params=pltpu.CompilerParams(
            dimension_semantics=("arbitrary",),
            vmem_limit_bytes=VMEM_LIMIT_BYTES),
        name="expert_ffn",
    )(blk_exp, h, wt.reshape(n_rows, 1), wg, wu, wd)


NSA_ROWS = NSA_GROUP * Q_BLOCK
NEAR_CMP = 32
NEAR_CMP_BACK = 16
CMP_PER_QBLOCK = Q_BLOCK // CMP_STRIDE
SEL_PER_TILE = Q_BLOCK // SEL_BLOCK
WIN_TILES = WINDOW // Q_BLOCK
MXU_DTYPE = jnp.bfloat16


MASK_OFF = -1e30
FAR_TILES = 8
SEL_SHIFT = SEL_BLOCK.bit_length() - 1
QB_SHIFT = Q_BLOCK.bit_length() - 1
assert 1 << SEL_SHIFT == SEL_BLOCK and 1 << QB_SHIFT == Q_BLOCK


def _nsa_prompt_body(q_ref, gt_ref, kc_ref, vct_ref, vc_ref, ks_ref, vst_ref, kw_ref, vwt_ref,
                       band_ref, near_ref, far_ref, winb_ref, imapt_ref, o_ref, m_scr, l_scr, acc_scr, z_scr):
    f32 = jnp.float32
    R, QB = NSA_ROWS, Q_BLOCK
    i = pl.program_id(2)
    n_cmp_pad = vct_ref.shape[-1]
    n_sb = imapt_ref.shape[0]
    qt = q_ref[0, 0, 0]
    far_bias = far_ref[0]
    r_lane = lax.broadcasted_iota(jnp.int32, (1, R), 1) & (QB - 1)
    t_lane = i * QB + r_lane

    def mxu(a, b):
        return jnp.dot(a.astype(MXU_DTYPE), b.astype(MXU_DTYPE), preferred_element_type=f32)

    def recip(x):
        r0 = 1.0 / x
        return r0 * (2.0 - x * r0)

    c0 = CMP_PER_QBLOCK * i - NEAR_CMP_BACK
    kc_far = kc_ref[0, 0, NEAR_CMP_BACK:NEAR_CMP_BACK + n_cmp_pad, :]
    lg_far = mxu(kc_far, qt) + far_bias
    mk_far = lax.broadcasted_iota(jnp.int32, (n_cmp_pad, R), 0) < c0
    near_start = pl.multiple_of(CMP_PER_QBLOCK * i, 8)
    k_near = kc_ref[0, 0, pl.ds(near_start, NEAR_CMP), :]
    v_near = vc_ref[0, 0, pl.ds(near_start, NEAR_CMP), :]
    lg_near = mxu(k_near, qt) + near_ref[0]
    c_near = c0 + lax.broadcasted_iota(jnp.int32, (NEAR_CMP, R), 0)
    mk_near = (c_near >= 0) & (c_near * CMP_STRIDE + (CMP_BLOCK - 1) <= t_lane)
    z_far = jnp.where(mk_far, lg_far, NEG)
    z_near = jnp.where(mk_near, lg_near, NEG)
    mx = jnp.maximum(z_far.max(0, keepdims=True), z_near.max(0, keepdims=True))
    e_far = jnp.where(mk_far, jnp.exp(z_far - mx), 0.0)
    e_near = jnp.where(mk_near, jnp.exp(z_near - mx), 0.0)
    s = e_far.sum(0, keepdims=True) + e_near.sum(0, keepdims=True)
    inv = recip(jnp.where(s > 0, s, 1.0))
    p_far = (e_far * inv).astype(MXU_DTYPE)
    p_near = (e_near * inv).astype(MXU_DTYPE)
    o_cmp = mxu(vct_ref[0, 0], p_far) + mxu(v_near.T, p_near)

    nn = lax.broadcasted_iota(jnp.int32, (n_sb, NEAR_CMP), 0)
    cn = c0 + lax.broadcasted_iota(jnp.int32, (n_sb, NEAR_CMP), 1)
    per_sel = SEL_BLOCK // CMP_STRIDE
    imapt_near = ((cn >= per_sel * nn - 1) & (cn <= per_sel * nn + per_sel - 1)).astype(f32)
    imp_rows = mxu(imapt_ref[...], p_far) + mxu(imapt_near, p_near)
    imp = imp_rows[:, 0:QB]
    for j in range(1, NSA_GROUP):
        imp = imp + imp_rows[:, j * QB:(j + 1) * QB]
    n_io = lax.broadcasted_iota(jnp.int32, (n_sb, QB), 0)
    n_f = n_io.astype(f32)
    r_q = lax.broadcasted_iota(jnp.int32, (n_sb, QB), 1)
    t_q = i * QB + r_q
    cur = SEL_PER_TILE * i + (r_q >= SEL_BLOCK).astype(jnp.int32)
    forced = (n_io == 0) | (n_io == cur) | (n_io == cur - 1)
    score = jnp.where(n_io * SEL_BLOCK <= t_q, jnp.where(forced, jnp.inf, imp), -jnp.inf)

    def pick_one(_, carry):
        score, sel = carry
        top = score.max(0, keepdims=True)
        first = jnp.where(score == top, n_f, float(n_sb)).min(0, keepdims=True)
        hit = n_f == first
        sel = jnp.where(hit & (top > -jnp.inf), 1.0, sel)
        return jnp.where(hit, -jnp.inf, score), sel

    _, sel = lax.fori_loop(0, min(N_SELECT, n_sb), pick_one, (score, jnp.zeros((n_sb, QB), f32)))
    sel = sel.astype(MXU_DTYPE)

    c_t = lax.broadcasted_iota(jnp.int32, (QB, R), 0)
    r_t = lax.broadcasted_iota(jnp.int32, (QB, R), 1) & (QB - 1)
    causal_off = jnp.where(c_t <= r_t, 0.0, MASK_OFF)

    def put(start, rows, z):
        z_scr[pl.ds(start, rows), :] = z
        m_scr[...] = jnp.maximum(m_scr[...], z.max(0, keepdims=True))

    def sel_off(kt):
        blk = SEL_PER_TILE * kt + (lax.broadcasted_iota(jnp.int32, (QB, n_sb), 0) >= SEL_BLOCK).astype(jnp.int32)
        expand = (lax.broadcasted_iota(jnp.int32, (QB, n_sb), 1) == blk).astype(MXU_DTYPE)
        off = (mxu(expand, sel) - 1.0) * (-MASK_OFF)
        return jnp.concatenate([off] * NSA_GROUP, axis=1)

    m_scr[...] = jnp.full((1, R), NEG, f32)
    KF = FAR_TILES * QB
    n_chunks = i // FAR_TILES + 1
    key_f = lax.broadcasted_iota(jnp.int32, (KF, n_sb), 0)
    n_f2 = lax.broadcasted_iota(jnp.int32, (KF, n_sb), 1)
    tile_f = lax.broadcasted_iota(jnp.int32, (KF, 1), 0) >> QB_SHIFT

    def sel_far(it, carry):
        start = pl.multiple_of(it * KF, KF)
        lg = mxu(ks_ref[0, 0, pl.ds(start, KF), :], qt)
        expand = (n_f2 == it * (KF // SEL_BLOCK) + (key_f >> SEL_SHIFT)).astype(MXU_DTYPE)
        off = (mxu(expand, sel) - 1.0) * (-MASK_OFF)
        off = off + jnp.where(it * FAR_TILES + tile_f < i - 1, 0.0, MASK_OFF)
        put(start, KF, lg + far_bias + jnp.concatenate([off] * NSA_GROUP, axis=1))
        return carry

    lax.fori_loop(0, n_chunks, sel_far, 0)

    @pl.when(i >= 1)
    def _():
        start = pl.multiple_of((i - 1) * QB, QB)
        put(start, QB, mxu(ks_ref[0, 0, pl.ds(start, QB), :], qt) + band_ref[0, 1] + sel_off(i - 1))

    diag = pl.multiple_of(i * QB, QB)
    put(diag, QB, mxu(ks_ref[0, 0, pl.ds(diag, QB), :], qt) + band_ref[0, 0] + (sel_off(i) + causal_off))
    m_sel = m_scr[...]
    l_scr[...] = jnp.zeros((1, R), f32)

    def sel_exp(it, carry):
        start = pl.multiple_of(it * KF, KF)
        e = jnp.exp(z_scr[pl.ds(start, KF), :] - m_sel)
        z_scr[pl.ds(start, KF), :] = e
        l_scr[...] = l_scr[...] + e.sum(0, keepdims=True)
        return carry

    lax.fori_loop(0, n_chunks, sel_exp, 0)
    inv_sel = recip(l_scr[...])
    acc_scr[...] = jnp.zeros_like(acc_scr)

    def sel_pv(it, carry):
        start = pl.multiple_of(it * KF, KF)
        p = (z_scr[pl.ds(start, KF), :] * inv_sel).astype(MXU_DTYPE)
        acc_scr[...] = acc_scr[...] + mxu(vst_ref[0, 0, :, pl.ds(start, KF)], p)
        return carry

    lax.fori_loop(0, n_chunks, sel_pv, 0)
    o_sel = acc_scr[...]

    start_d = pl.multiple_of((i + WIN_TILES) * QB, QB)
    z_d = mxu(kw_ref[0, 0, pl.ds(start_d, QB), :], qt) + band_ref[0, 0] + causal_off
    start_w = pl.multiple_of(i * QB, QB)
    before_start = (lax.broadcasted_iota(jnp.int32, (WINDOW, 1), 0) >> QB_SHIFT) < WIN_TILES - i
    z_w = mxu(kw_ref[0, 0, pl.ds(start_w, WINDOW), :], qt) + winb_ref[0] + jnp.where(before_start, MASK_OFF, 0.0)
    m_win = jnp.maximum(z_d.max(0, keepdims=True), z_w.max(0, keepdims=True))
    e_d = jnp.exp(z_d - m_win)
    e_w = jnp.exp(z_w - m_win)
    inv_win = recip(e_d.sum(0, keepdims=True) + e_w.sum(0, keepdims=True))
    o_win = (mxu(vwt_ref[0, 0, :, pl.ds(start_d, QB)], (e_d * inv_win).astype(MXU_DTYPE))
             + mxu(vwt_ref[0, 0, :, pl.ds(start_w, WINDOW)], (e_w * inv_win).astype(MXU_DTYPE)))
    gt = gt_ref[0, 0, 0]
    o_ref[0, 0, 0] = gt[0:1] * o_cmp + gt[1:2] * o_sel + gt[2:3] * o_win


def _nsa_prompt_attention(q, gates, kc, vc, ksel, vsel, kwin, vwin, rel_bias):
    f32 = jnp.float32
    B, L = q.shape[:2]
    G, J, dh, QB, R = NSA_KV, NSA_GROUP, NSA_DH, Q_BLOCK, NSA_ROWS
    assert L % QB == 0 and WINDOW % QB == 0 and QB % SEL_BLOCK == 0
    nq = L // QB
    n_cmp = kc.shape[1]
    n_cmp_pad = L // CMP_STRIDE
    n_sb = L // SEL_BLOCK
    assert n_cmp == n_cmp_pad - 1

    def cols(a):
        x = a.shape[-1]
        return a.reshape(B, nq, QB, G, J, x).transpose(0, 3, 1, 5, 4, 2).reshape(B, G, nq, x, R)

    q_c = cols(q).astype(MXU_DTYPE)
    gt_c = cols(gates)
    pad = ((0, 0), (NEAR_CMP_BACK, n_cmp_pad - n_cmp + NEAR_CMP - NEAR_CMP_BACK), (0, 0), (0, 0))
    kc_p = jnp.pad(kc, pad).transpose(0, 2, 1, 3)
    vc_p = jnp.pad(vc, pad).transpose(0, 2, 1, 3)
    vct = vc_p[:, :, NEAR_CMP_BACK:NEAR_CMP_BACK + n_cmp_pad].transpose(0, 1, 3, 2).astype(MXU_DTYPE)
    ks = ksel.transpose(0, 2, 1, 3).astype(MXU_DTYPE)
    vst = vsel.transpose(0, 2, 3, 1).astype(MXU_DTYPE)
    kw = jnp.pad(kwin.transpose(0, 2, 1, 3), ((0, 0), (0, 0), (WINDOW, 0), (0, 0))).astype(MXU_DTYPE)
    vwt = jnp.pad(vwin.transpose(0, 2, 3, 1), ((0, 0), (0, 0), (0, 0), (WINDOW, 0))).astype(MXU_DTYPE)

    n_dist = MAX_DISTANCE
    by_dist = rel_bias[_t5_bucket(jnp.arange(n_dist))].astype(f32).reshape(n_dist, G, J)
    by_dist = by_dist.transpose(1, 2, 0)

    def table(dist):
        d = jnp.clip(dist, 0, n_dist - 1)
        return by_dist[:, :, d].transpose(0, 2, 1, 3).reshape(G, dist.shape[0], R)

    c = jnp.arange(QB)[:, None]
    r = jnp.arange(QB)[None, :]
    band = jnp.stack([table(r - c), table(r - c + QB)], axis=1)
    u = jnp.arange(NEAR_CMP)[:, None]
    near = table(r - CMP_STRIDE * (u - NEAR_CMP_BACK) - (CMP_BLOCK - 1))
    far = jnp.repeat(by_dist[:, :, n_dist - 1], QB, axis=1).reshape(G, 1, R)
    assert nq % FAR_TILES == 0
    edge = jnp.tile(jnp.where(c > r, 0.0, MASK_OFF), (1, J))[None] + far
    winb = jnp.concatenate([edge] + [jnp.broadcast_to(far, (G, QB, R))] * (WIN_TILES - 2) + [band[:, 1]], axis=1)
    imapt = _importance_map(n_cmp_pad, n_sb).T

    kv_spec = lambda shape: pl.BlockSpec((1, 1) + shape, lambda b, g, i: (b, g, 0, 0))
    out = pl.pallas_call(
        _nsa_prompt_body,
        grid=(B, G, nq),
        in_specs=[pl.BlockSpec((1, 1, 1, dh, R), lambda b, g, i: (b, g, i, 0, 0)),
                  pl.BlockSpec((1, 1, 1, 3, R), lambda b, g, i: (b, g, i, 0, 0)),
                  kv_spec((n_cmp_pad + NEAR_CMP, dh)),
                  kv_spec((dh, n_cmp_pad)),
                  kv_spec((n_cmp_pad + NEAR_CMP, dh)),
                  kv_spec((L, dh)), kv_spec((dh, L)), kv_spec((L + WINDOW, dh)), kv_spec((dh, L + WINDOW)),
                  pl.BlockSpec((1, 2, QB, R), lambda b, g, i: (g, 0, 0, 0)),
                  pl.BlockSpec((1, NEAR_CMP, R), lambda b, g, i: (g, 0, 0)),
                  pl.BlockSpec((1, 1, R), lambda b, g, i: (g, 0, 0)),
                  pl.BlockSpec((1, WINDOW, R), lambda b, g, i: (g, 0, 0)),
                  pl.BlockSpec((n_sb, n_cmp_pad), lambda b, g, i: (0, 0))],
        out_specs=pl.BlockSpec((1, 1, 1, dh, R), lambda b, g, i: (b, g, i, 0, 0)),
        out_shape=jax.ShapeDtypeStruct((B, G, nq, dh, R), f32),
        scratch_shapes=[pltpu.VMEM((1, R), f32), pltpu.VMEM((1, R), f32), pltpu.VMEM((dh, R), f32),
                        pltpu.VMEM((L, R), f32)],
        compiler_params=pltpu.CompilerParams(
            dimension_semantics=("parallel", "parallel", "arbitrary"),
            vmem_limit_bytes=VMEM_LIMIT_BYTES),
        name="nsa_prompt",
    )(q_c, gt_c, kc_p, vct, vc_p, ks, vst, kw, vwt, band, near, far, winb, imapt)
    return out.reshape(B, G, nq, dh, J, QB).transpose(0, 2, 5, 1, 4, 3).reshape(B, L, G * J * dh)


PAGE_SIZE = 128
ROW_WIDTH = N_CACHE_STREAMS * NSA_KV * NSA_DH
CMP_WIDTH = 2 * NSA_KV * NSA_DH
CHUNKS_PER_PAGE = PAGE_SIZE // CMP_STRIDE
PAGES_PER_STEP = 16


def _cmp_partial_body(x_ref, w_ref, o_ref):
    pg = x_ref.shape[0]
    chunk_lanes = CMP_STRIDE * ROW_WIDTH
    acc = None
    for s in range(CMP_STRIDE):
        lo = s * ROW_WIDTH
        lhs = jnp.concatenate([x_ref[:, c * chunk_lanes + lo:c * chunk_lanes + lo + CMP_WIDTH]
                               for c in range(CHUNKS_PER_PAGE)], axis=0)
        part = jnp.dot(lhs.astype(MXU_DTYPE), w_ref[s], preferred_element_type=jnp.float32)
        acc = part if acc is None else acc + part
    for c in range(CHUNKS_PER_PAGE):
        o_ref[0, c] = acc[c * pg:(c + 1) * pg]


def _compress_partials(pages, w1):
    f32 = jnp.float32
    n = pages.shape[0]
    G, dh = NSA_KV, NSA_DH
    pg = math.gcd(n, PAGES_PER_STEP)
    w = w1.reshape(2, 2, CMP_STRIDE, dh, CMP_HID)
    wb = jnp.einsum('nzsdh,nm,gk->sngdzmkh', w, jnp.eye(2, dtype=f32), jnp.eye(G, dtype=f32))
    n_out = 2 * 2 * G * CMP_HID
    wb = wb.reshape(CMP_STRIDE, CMP_WIDTH, n_out).astype(MXU_DTYPE)
    out = pl.pallas_call(
        _cmp_partial_body,
        grid=(n // pg,),
        in_specs=[pl.BlockSpec((pg, PAGE_SIZE * ROW_WIDTH), lambda i: (i, 0)),
                  pl.BlockSpec((CMP_STRIDE, CMP_WIDTH, n_out), lambda i: (0, 0, 0))],
        out_specs=pl.BlockSpec((1, CHUNKS_PER_PAGE, pg, n_out), lambda i: (i, 0, 0, 0)),
        out_shape=jax.ShapeDtypeStruct((n // pg, CHUNKS_PER_PAGE, pg, n_out), f32),
        compiler_params=pltpu.CompilerParams(dimension_semantics=("parallel",),
                                             vmem_limit_bytes=VMEM_LIMIT_BYTES),
        name="compress_partials",
    )(pages, wb)
    return out.transpose(0, 2, 1, 3).reshape(n * CHUNKS_PER_PAGE, 2, 2, G, CMP_HID)


def _hdot(a, b):
    return jnp.dot(a, b, preferred_element_type=jnp.float32, precision=lax.Precision.HIGHEST)


def _hdot_nt(a, b):
    return lax.dot_general(a, b, (((1,), (1,)), ((), ())), preferred_element_type=jnp.float32,
                           precision=lax.Precision.HIGHEST)


def _bdot(a, b):
    return jnp.dot(a.astype(MXU_DTYPE), b.astype(MXU_DTYPE), preferred_element_type=jnp.float32)


def _dot3(a, b):
    f32 = jnp.float32
    a_hi = a.astype(MXU_DTYPE)
    b_hi = b.astype(MXU_DTYPE)
    a_lo = (a - a_hi.astype(f32)).astype(MXU_DTYPE)
    b_lo = (b - b_hi.astype(f32)).astype(MXU_DTYPE)
    dot = functools.partial(jnp.dot, preferred_element_type=f32)
    return dot(a_hi, b_hi) + (dot(a_hi, b_lo) + dot(a_lo, b_hi))


def _bdot_nt(a, b):
    return lax.dot_general(a.astype(MXU_DTYPE), b.astype(MXU_DTYPE), (((1,), (1,)), ((), ())),
                           preferred_element_type=jnp.float32)


ROUTER_OUT_LANES = 128


def _router_body(x_ref, w_ref, b_ref, o_ref, *, precise):
    f32 = jnp.float32
    if precise:
        logits = _hdot(x_ref[...], w_ref[...])
    else:
        logits = jnp.dot(x_ref[...].astype(jnp.bfloat16), w_ref[...].astype(jnp.bfloat16),
                         preferred_element_type=f32)
    scores = jax.nn.sigmoid(logits)
    biased = scores + b_ref[...]
    tm = scores.shape[0]
    lane = lax.broadcasted_iota(jnp.int32, (tm, N_EXPERTS), 1).astype(f32)
    e_io = lax.broadcasted_iota(jnp.int32, (tm, N_EXPERTS), 1)
    grp = sum((e_io >= gi * EXPERTS_PER_GROUP).astype(jnp.int32) for gi in range(1, N_EXPERT_GROUPS))

    def top2(vals):
        m1 = vals.max(-1, keepdims=True)
        i1 = jnp.where(vals == m1, lane, float(N_EXPERTS)).min(-1, keepdims=True)
        rest = jnp.where(lane == i1, -jnp.inf, vals)
        m2 = rest.max(-1, keepdims=True)
        i2 = jnp.where(rest == m2, lane, float(N_EXPERTS)).min(-1, keepdims=True)
        return m1, i1, m2, i2

    best = None
    gsel = None
    for gi in range(N_EXPERT_GROUPS):
        m1, _, m2, _ = top2(jnp.where(grp == gi, biased, -jnp.inf))
        gsum = m1 + m2
        if best is None:
            best, gsel = gsum, jnp.zeros_like(gsum, dtype=jnp.int32)
        else:
            better = gsum > best
            best = jnp.where(better, gsum, best)
            gsel = jnp.where(better, gi, gsel)
    _, i1, _, i2 = top2(jnp.where(grp == gsel, biased, -jnp.inf))
    s1 = jnp.where(lane == i1, scores, 0.0).sum(-1, keepdims=True)
    s2 = jnp.where(lane == i2, scores, 0.0).sum(-1, keepdims=True)
    tot = s1 + s2
    col = lax.broadcasted_iota(jnp.int32, (tm, ROUTER_OUT_LANES), 1)
    o_ref[...] = jnp.where(col == 0, i1, jnp.where(col == 1, i2, jnp.where(col == 2, s1 / tot, s2 / tot)))


def _router(x, w_router, router_bias, precise):
    N, D = x.shape
    tm = min(MM_ROW_TILE, N)
    out = pl.pallas_call(
        functools.partial(_router_body, precise=precise),
        grid=(pl.cdiv(N, tm),),
        in_specs=[pl.BlockSpec((tm, D), lambda i: (i, 0)),
                  pl.BlockSpec((D, N_EXPERTS), lambda i: (0, 0)),
                  pl.BlockSpec((1, N_EXPERTS), lambda i: (0, 0))],
        out_specs=pl.BlockSpec((tm, ROUTER_OUT_LANES), lambda i: (i, 0)),
        out_shape=jax.ShapeDtypeStruct((N, ROUTER_OUT_LANES), jnp.float32),
        compiler_params=pltpu.CompilerParams(dimension_semantics=("parallel",),
                                             vmem_limit_bytes=VMEM_LIMIT_BYTES),
        name="router",
    )(x, w_router, router_bias.reshape(1, N_EXPERTS))
    return out[:, 0:2].astype(jnp.int32), out[:, 2:4]


def _tri(c, strict=False, upper=False):
    r = lax.broadcasted_iota(jnp.int32, (c, c), 0)
    s = lax.broadcasted_iota(jnp.int32, (c, c), 1)
    if upper:
        r, s = s, r
    return (r > s) if strict else (r >= s)


def _dn_body(u_ref, gcol_ref, grow_ref, beta_ref, gate_ref, nw_ref, s0_ref, o_ref, s_ref):
    f32 = jnp.float32
    nb, c = u_ref.shape[0], u_ref.shape[1]
    dk, dv = DN_DK, DN_DV

    @pl.when(pl.program_id(0) == 0)
    def _():
        s_ref[...] = s0_ref[...]

    incl = _tri(c)
    strict = _tri(c, strict=True)
    eye = (lax.broadcasted_iota(jnp.int32, (c, c), 0) == lax.broadcasted_iota(jnp.int32, (c, c), 1)).astype(f32)
    nw = nw_ref[...]
    pairs = [(b, h) for b in range(nb) for h in range(DN_HEADS)]
    each = lambda f, *cols: [f(*args) for args in zip(*cols)]
    states = [s_ref[b, h] for b, h in pairs]
    gates = [gate_ref[b, :, h * dv:(h + 1) * dv] for b, h in pairs]
    q = [u_ref[b, :, h * dk:(h + 1) * dk] for b, h in pairs]
    k = [u_ref[b, :, DN_HEADS * dk + h * dk:DN_HEADS * dk + (h + 1) * dk] for b, h in pairs]
    v = [u_ref[b, :, 2 * DN_HEADS * dk + h * dv:2 * DN_HEADS * dk + (h + 1) * dv] for b, h in pairs]
    gc = [gcol_ref[b, 0][:, h:h + 1] for b, h in pairs]
    gr = [grow_ref[b, 0][h:h + 1, :] for b, h in pairs]
    beta = [beta_ref[b, 0][:, h:h + 1] for b, h in pairs]
    q = each(lambda x: x * lax.rsqrt(jnp.sum(x * x, -1, keepdims=True) + EPS) * dk ** -0.5, q)
    k = each(lambda x: x * lax.rsqrt(jnp.sum(x * x, -1, keepdims=True) + EPS), k)
    dec = each(lambda a, b: jnp.where(incl, jnp.exp(jnp.where(incl, a - b, 0.0)), 0.0), gc, gr)
    kb = each(lambda a, b: a * b, k, beta)
    a = each(lambda x, y, d: jnp.where(strict, _bdot_nt(x, y) * d, 0.0), kb, k, dec)
    p = each(lambda x: -x, a)
    tm = each(lambda x: eye + x, p)
    for _ in range(max(1, math.ceil(math.log2(c))) - 1):
        p = each(lambda x: _dot3(x, x), p)
        tm = each(lambda t, x: t + _dot3(t, x), tm, p)
    eg = each(jnp.exp, gc)
    u = each(lambda t, x, bt: _bdot(t, x * bt), tm, v, beta)
    w = each(lambda t, x, e: _bdot(t, x * e), tm, kb, eg)
    v_new = each(lambda x, y, s: x - _bdot(y, s), u, w, states)
    attn = each(lambda x, y, d: _bdot_nt(x, y) * d, q, k, dec)
    o = each(lambda x, e, s, at, vn: _bdot(x * e, s) + _bdot(at, vn), q, eg, states, attn, v_new)
    g_last = each(lambda x: x[c - 1:c, :], gc)
    new_states = each(lambda s, gl, x, g, vn: s * jnp.exp(gl) + _bdot((x * jnp.exp(gl - g)).T, vn),
                      states, g_last, k, gc, v_new)
    o = each(lambda x, g: x * lax.rsqrt(jnp.mean(x * x, -1, keepdims=True) + EPS) * nw * jax.nn.silu(g), o, gates)
    for (b, h), s_new, o_bh in zip(pairs, new_states, o):
        s_ref[b, h] = s_new
        o_ref[b, :, h * dv:(h + 1) * dv] = o_bh


def _dn_scan(u, g, beta, gate, norm_w, s0):
    f32 = jnp.float32
    B, L, _ = u.shape
    c = min(DN_CHUNK, L)
    assert L % c == 0
    n = L // c
    H = DN_HEADS
    g4 = jnp.cumsum(g.reshape(B, n, c, H), axis=2)
    return pl.pallas_call(
        _dn_body,
        grid=(n,),
        in_specs=[pl.BlockSpec((B, c, DN_CONV_DIM), lambda i: (0, i, 0)),
                  pl.BlockSpec((B, 1, c, H), lambda i: (0, i, 0, 0)),
                  pl.BlockSpec((B, 1, H, c), lambda i: (0, i, 0, 0)),
                  pl.BlockSpec((B, 1, c, H), lambda i: (0, i, 0, 0)),
                  pl.BlockSpec((B, c, H * DN_DV), lambda i: (0, i, 0)),
                  pl.BlockSpec((1, DN_DV), lambda i: (0, 0)),
                  pl.BlockSpec((B, H, DN_DK, DN_DV), lambda i: (0, 0, 0, 0))],
        out_specs=[pl.BlockSpec((B, c, H * DN_DV), lambda i: (0, i, 0)),
                   pl.BlockSpec((B, H, DN_DK, DN_DV), lambda i: (0, 0, 0, 0))],
        out_shape=[jax.ShapeDtypeStruct((B, L, H * DN_DV), f32),
                   jax.ShapeDtypeStruct((B, H, DN_DK, DN_DV), f32)],
        compiler_params=pltpu.CompilerParams(dimension_semantics=("arbitrary",),
                                             vmem_limit_bytes=VMEM_LIMIT_BYTES),
        name="deltanet_scan",
    )(u, g4, g4.transpose(0, 1, 3, 2), beta.reshape(B, n, c, H), gate, norm_w.reshape(1, DN_DV), s0)


HG_SUB = 16


def _hg_body(q_ref, f_ref, i_ref, gate_ref, lb_ref, nw_ref, s0_ref, o_ref, s_ref):
    f32 = jnp.float32
    nb, c = q_ref.shape[0], q_ref.shape[1]
    dk, dv = HG_DK, HG_DV
    sub = min(HG_SUB, c)

    @pl.when(pl.program_id(0) == 0)
    def _():
        s_ref[...] = s0_ref[...]

    lower = _tri(c).astype(f32)
    t_sub = lax.broadcasted_iota(jnp.int32, (sub, 1), 0)
    nw = nw_ref[...]
    pairs = [(b, h) for b in range(nb) for h in range(HG_HEADS)]
    each = lambda fn, *cols: [fn(*args) for args in zip(*cols)]
    st = [s_ref[b, h] for b, h in pairs]
    gates = [gate_ref[b, :, h * dv:(h + 1) * dv] for b, h in pairs]
    q = [q_ref[b, :, h * dk:(h + 1) * dk] * dk ** -0.5 for b, h in pairs]
    lbs = [lb_ref[:, h * dk:(h + 1) * dk] for b, h in pairs]
    f = [lb + (1.0 - lb) * jax.nn.sigmoid(f_ref[b, :, h * dk:(h + 1) * dk]) for (b, h), lb in zip(pairs, lbs)]
    k = each(lambda x: 1.0 - x, f)
    v = [i_ref[b, :, h * dv:(h + 1) * dv] for b, h in pairs]
    gcum = each(lambda x: _hdot(lower, jnp.log(x)), f)
    o_inter = each(lambda x, g, s: _bdot_nt(x * jnp.exp(g), s), q, gcum, st)
    v_r = each(lambda x: x.astype(MXU_DTYPE).astype(f32), v)
    out_blocks = []
    for blk in range(c // sub):
        r0 = blk * sub
        q_i = each(lambda x: x[r0:r0 + sub], q)
        g_i = each(lambda x: x[r0:r0 + sub], gcum)
        o_i = each(lambda x: x[r0:r0 + sub], o_inter)
        if blk > 0:
            def earlier(qi, gi, g, kk, vv, oi):
                g_b = g[r0 - 1:r0]
                a = _hdot_nt(qi * jnp.exp(gi - g_b), kk[:r0] * jnp.exp(g_b - g[:r0]))
                return oi + _bdot(a, vv[:r0])
            o_i = each(earlier, q_i, g_i, gcum, k, v, o_i)
        for s in range(sub):
            seen = t_sub >= s

            def diag(qi, gi, g, kk, vr, oi):
                term = qi * kk[r0 + s:r0 + s + 1] * jnp.exp(jnp.where(seen, gi - g[r0 + s:r0 + s + 1], 0.0))
                a_col = jnp.where(seen, term.sum(-1, keepdims=True), 0.0)
                return oi + a_col.astype(MXU_DTYPE).astype(f32) * vr[r0 + s:r0 + s + 1]
            o_i = each(diag, q_i, g_i, gcum, k, v_r, o_i)
        out_blocks.append(each(
            lambda oi, g: oi * lax.rsqrt(jnp.mean(oi * oi, -1, keepdims=True) + EPS) * nw * jax.nn.silu(g[r0:r0 + sub]),
            o_i, gates))
    g_last = each(lambda g: g[c - 1:c], gcum)
    new_st = each(lambda s, gl, vv, kk, g: s * jnp.exp(gl) + _bdot(vv.T, kk * jnp.exp(gl - g)), st, g_last, v, k, gcum)
    for n, (b, h) in enumerate(pairs):
        s_ref[b, h] = new_st[n]
        for blk in range(c // sub):
            o_ref[b, blk * sub:(blk + 1) * sub, h * dv:(h + 1) * dv] = out_blocks[blk][n]


def _hg_scan(q_raw, f_raw, i_raw, gate, lb, norm_w, s0):
    f32 = jnp.float32
    B, L, _ = q_raw.shape
    c = min(HG_CHUNK, L)
    assert L % c == 0
    n = L // c
    H = HG_HEADS
    tok = lambda w: pl.BlockSpec((B, c, w), lambda i: (0, i, 0))
    o, st = pl.pallas_call(
        _hg_body,
        grid=(n,),
        in_specs=[tok(H * HG_DK), tok(H * HG_DK), tok(H * HG_DV), tok(H * HG_DV),
                  pl.BlockSpec((1, H * HG_DK), lambda i: (0, 0)),
                  pl.BlockSpec((1, HG_DV), lambda i: (0, 0)),
                  pl.BlockSpec((B, H, HG_DV, HG_DK), lambda i: (0, 0, 0, 0))],
        out_specs=[tok(H * HG_DV), pl.BlockSpec((B, H, HG_DV, HG_DK), lambda i: (0, 0, 0, 0))],
        out_shape=[jax.ShapeDtypeStruct((B, L, H * HG_DV), f32),
                   jax.ShapeDtypeStruct((B, H, HG_DV, HG_DK), f32)],
        compiler_params=pltpu.CompilerParams(dimension_semantics=("arbitrary",),
                                             vmem_limit_bytes=VMEM_LIMIT_BYTES),
        name="hgrn2_scan",
    )(q_raw, f_raw, i_raw, gate, lb.reshape(1, H * HG_DK), norm_w.reshape(1, HG_DV), s0.transpose(0, 1, 3, 2))
    return o, st.transpose(0, 1, 3, 2)


def _split(z, sizes):
    return jnp.split(z, np.cumsum(sizes)[:-1].tolist(), axis=-1)


def _layer_norm(x, g, b):
    mu = x.mean(-1, keepdims=True)
    var = jnp.square(x - mu).mean(-1, keepdims=True)
    return (x - mu) * lax.rsqrt(var + EPS) * g + b


def _masked_softmax(logits, mask):
    z = jnp.where(mask, logits.astype(jnp.float32), NEG)
    e = jnp.exp(z - z.max(-1, keepdims=True)) * mask
    s = e.sum(-1, keepdims=True)
    return e / jnp.where(s > 0, s, 1.0)


def _l2norm(a):
    return a * lax.rsqrt(jnp.sum(a * a, -1, keepdims=True) + EPS)


def _gated_rmsnorm(o, gate, w):
    B, L, H, dv = o.shape
    o = o * lax.rsqrt(jnp.mean(o * o, -1, keepdims=True) + EPS) * w
    return o.reshape(B, L, H * dv) * jax.nn.silu(gate)


def _t5_bucket(dist):
    n = jnp.maximum(dist, 0)
    exact = N_BUCKETS // 2
    rel = jnp.log(jnp.maximum(n, 1).astype(jnp.float32) / exact) / math.log(MAX_DISTANCE / exact)
    large = jnp.minimum(exact + (rel * (N_BUCKETS - exact)).astype(jnp.int32), N_BUCKETS - 1)
    return jnp.where(n < exact, n, large)


def _head_bias(dist, rel_bias):
    Q, K = dist.shape
    bias = rel_bias[_t5_bucket(dist)]
    return bias.reshape(Q, K, NSA_KV, NSA_GROUP).transpose(2, 3, 0, 1).astype(jnp.float32)


def _compress(rows, w1, w2, pe):
    B, T, G, dh = rows.shape
    n_chunk = T // CMP_STRIDE
    r = rows[:, :n_chunk * CMP_STRIDE].reshape(B, n_chunk, CMP_STRIDE, G, dh)
    w1h = w1.reshape(2, CMP_STRIDE, dh, CMP_HID)
    lo = jnp.einsum('bcsgd,sdh->bcgh', r, w1h[0])
    hi = jnp.einsum('bcsgd,sdh->bcgh', r, w1h[1])
    h = lo[:, :-1] + hi[:, 1:] + pe.reshape(-1) @ w1
    return jax.nn.silu(h) @ w2


def _importance_map(n_cmp, n_sb):
    c0 = jnp.arange(n_cmp) * CMP_STRIDE
    c1 = c0 + CMP_BLOCK - 1
    b0 = jnp.arange(n_sb) * SEL_BLOCK
    b1 = b0 + SEL_BLOCK - 1
    return ((c0[:, None] <= b1[None, :]) & (c1[:, None] >= b0[None, :])).astype(jnp.float32)


def _nsa_core(q, gates, t, kc, vc, cmp_end, imp_map, n_sb, fetch_sel, kw, vw, w_pos, rel_bias):
    f32 = jnp.float32
    B, Q = q.shape[:2]
    lg = jnp.einsum('bqgjd,bcgd->bgjqc', q, kc).astype(f32) + _head_bias(t[:, None] - cmp_end[None, :], rel_bias)
    p_cmp = _masked_softmax(lg, cmp_end[None, :] <= t[:, None])
    o_cmp = jnp.einsum('bgjqc,bcgd->bqgjd', p_cmp, vc)
    n_top = min(N_SELECT, n_sb)
    imp = jnp.einsum('bgjqc,cn->bgqn', p_cmp, imp_map)
    blk = jnp.arange(n_sb)[None, :]
    cur = (t // SEL_BLOCK)[:, None]
    forced = (blk == 0) | (blk == cur) | (blk == cur - 1)
    score = jnp.where(blk * SEL_BLOCK <= t[:, None], jnp.where(forced, jnp.inf, imp), -jnp.inf)
    top_s, top_i = lax.top_k(score, n_top)
    n_keys = n_top * SEL_BLOCK
    ks, vs = (a.reshape(B, NSA_KV, Q, n_keys, NSA_DH) for a in fetch_sel(top_i))
    s_pos = (top_i[..., None] * SEL_BLOCK + jnp.arange(SEL_BLOCK)).reshape(B, NSA_KV, Q, n_keys)
    s_ok = jnp.repeat(top_s > -jnp.inf, SEL_BLOCK, axis=-1) & (s_pos <= t[:, None])
    table = rel_bias.reshape(N_BUCKETS, NSA_KV, NSA_GROUP).transpose(1, 2, 0)
    bucket = _t5_bucket(t[:, None] - s_pos)
    onehot = (bucket[..., None] == jnp.arange(N_BUCKETS)).astype(f32)
    s_bias = jnp.einsum('bgqmn,gjn->bgjqm', onehot, table.astype(f32), precision=lax.Precision.HIGHEST)
    lg = jnp.einsum('bqgjd,bgqmd->bgjqm', q, ks).astype(f32) + s_bias
    p_sel = _masked_softmax(lg, s_ok[:, :, None])
    o_sel = jnp.einsum('bgjqm,bgqmd->bqgjd', p_sel, vs)
    w_ok = (w_pos[None, :] <= t[:, None]) & (w_pos[None, :] > t[:, None] - WINDOW) & (w_pos[None, :] >= 0)
    lg = jnp.einsum('bqgjd,bkgd->bgjqk', q, kw).astype(f32) + _head_bias(t[:, None] - w_pos[None, :], rel_bias)
    p_win = _masked_softmax(lg, w_ok)
    o_win = jnp.einsum('bgjqk,bkgd->bqgjd', p_win, vw)
    o = gates[..., 0:1] * o_cmp + gates[..., 1:2] * o_sel + gates[..., 2:3] * o_win
    return o.reshape(B, Q, NSA_HEADS * NSA_DH)


def _nsa_mixer(q_raw, kv_raw, gate_raw, past, w_cmp1, w_cmp2, pe_cmp, rel_bias):
    B, L, _ = q_raw.shape
    q = (q_raw * NSA_DH ** -0.5).reshape(B, L, NSA_KV, NSA_GROUP, NSA_DH)
    gates = jax.nn.sigmoid(gate_raw).reshape(B, L, NSA_KV, NSA_GROUP, 3)
    kv = kv_raw.reshape(B, L, N_KV_STREAMS, NSA_KV, NSA_DH)
    new_rows, new_win = kv[:, :, :N_CACHE_STREAMS], kv[:, :, N_CACHE_STREAMS:]
    if past is None:
        kc = _compress(new_rows[:, :, 0], w_cmp1[0], w_cmp2[0], pe_cmp[0])
        vc = _compress(new_rows[:, :, 1], w_cmp1[1], w_cmp2[1], pe_cmp[1])
        o = _nsa_prompt_attention(q, gates, kc, vc, new_rows[:, :, 2], new_rows[:, :, 3],
                                    new_win[:, :, 0], new_win[:, :, 1], rel_bias)
        return o, new_rows, new_win[:, -min(WINDOW, L):]

    cache_nsa, layer, page_table, win_buf = past
    assert L == 1 and cache_nsa.shape[2] == PAGE_SIZE
    n_pool = cache_nsa.shape[1]
    n_pages = page_table.shape[1]
    pos0 = n_pages * PAGE_SIZE
    pages = jnp.take(cache_nsa[layer].reshape(n_pool, PAGE_SIZE * ROW_WIDTH), page_table.reshape(-1),
                     axis=0).astype(MXU_DTYPE)
    part = _compress_partials(pages, w_cmp1)
    part = part.reshape((B, n_pages * CHUNKS_PER_PAGE) + part.shape[1:])

    def finish(n):
        h = part[:, :-1, 0, n] + part[:, 1:, 1, n] + pe_cmp[n].reshape(-1) @ w_cmp1[n]
        return jax.nn.silu(h) @ w_cmp2[n]

    kc, vc = finish(0), finish(1)
    n_cmp = kc.shape[1]
    cmp_end = jnp.arange(n_cmp) * CMP_STRIDE + CMP_BLOCK - 1
    n_past_blk = pos0 // SEL_BLOCK
    n_sb = n_past_blk + 1
    blk_per_page = PAGE_SIZE // SEL_BLOCK
    first = (jnp.arange(SEL_BLOCK) == 0)[:, None]

    def fetch_sel(top_i):
        n_top = top_i.shape[-1]
        idx = jnp.minimum(top_i, n_past_blk - 1)
        row = jnp.arange(B)[:, None, None, None] * n_pages + idx // blk_per_page
        got = jnp.take(pages, row.reshape(-1), axis=0)
        got = got.reshape(B, NSA_KV, 1, n_top, blk_per_page, SEL_BLOCK, N_CACHE_STREAMS, NSA_KV, NSA_DH)
        half = (idx % blk_per_page)[..., None, None]
        is_new = (top_i == n_past_blk)[..., None, None]
        out = []
        for stream in (2, 3):
            cached = jnp.stack([got[:, g, :, :, :, :, stream, g] for g in range(NSA_KV)], axis=1)
            block = cached[:, :, :, :, 0]
            for h in range(1, blk_per_page):
                block = jnp.where(half == h, cached[:, :, :, :, h], block)
            fresh = (new_rows[:, 0, stream][:, :, None, None, None, :] * first).astype(block.dtype)
            out.append(jnp.where(is_new, fresh, block))
        return out

    wrows = jnp.concatenate([win_buf, new_win], 1)
    n_buf = wrows.shape[1] - L
    o = _nsa_core(q, gates, pos0 + jnp.arange(L), kc, vc, cmp_end, _importance_map(n_cmp, n_sb), n_sb, fetch_sel,
                  wrows[:, :, 0], wrows[:, :, 1], pos0 - n_buf + jnp.arange(n_buf + L), rel_bias)
    return o, new_rows, wrows[:, -min(WINDOW, pos0 + L):]


def _chunk(a, c):
    B, L = a.shape[:2]
    n = -(-L // c)
    a = jnp.pad(a, [(0, 0), (0, n * c - L)] + [(0, 0)] * (a.ndim - 2))
    a = a.reshape((B, n, c) + a.shape[2:])
    return jnp.moveaxis(a, (1, 2), (0, 3))


def _unchunk(o, L):
    n, B, H, c, d = o.shape
    return jnp.moveaxis(o, (0, 3), (1, 2)).reshape(B, n * c, H, d)[:, :L]


def _gated_delta(q, k, v, beta, g, s0):
    f32 = jnp.float32
    L = k.shape[1]
    c = min(DN_CHUNK, L)
    qs, ks, vs = (_chunk(a, c) for a in (q, k, v))
    bs, gs = _chunk(beta, c), _chunk(g, c)
    incl = jnp.tril(jnp.ones((c, c), bool))
    strict = jnp.tril(jnp.ones((c, c), bool), -1)
    eye = jnp.eye(c, dtype=f32)

    def step(S, xs):
        qc, kc, vc, bc, gc = xs
        G = jnp.cumsum(gc, -1)
        dec = jnp.where(incl, jnp.exp(jnp.where(incl, G[..., :, None] - G[..., None, :], 0.0)), 0.0)
        kb = kc * bc[..., None]
        A = jnp.where(strict, jnp.einsum('bhtk,bhsk->bhts', kb, kc) * dec, 0.0)
        Tm = lax.linalg.triangular_solve(eye + A, jnp.broadcast_to(eye, A.shape), left_side=True, lower=True)
        u = Tm @ (vc * bc[..., None])
        w = Tm @ (kb * jnp.exp(G)[..., None])
        v_new = u - w @ S
        attn = jnp.einsum('bhtk,bhsk->bhts', qc, kc) * dec
        o = (qc * jnp.exp(G)[..., None]) @ S + attn @ v_new
        S = S * jnp.exp(G[..., -1])[..., None, None] + jnp.einsum(
            'bhtk,bhtv->bhkv', kc * jnp.exp(G[..., -1:] - G)[..., None], v_new)
        return S, o

    S, o = lax.scan(step, s0, (qs, ks, vs, bs, gs))
    return _unchunk(o, L), S


def _gla_scan(q, k, v, logf, s0):
    L = k.shape[1]
    c = min(HG_CHUNK, L)
    qs, ks, vs, fs = (_chunk(a, c) for a in (q, k, v, logf))
    incl = jnp.tril(jnp.ones((c, c), bool))[:, :, None]

    def step(S, xs):
        qc, kc, vc, fc = xs
        G = jnp.cumsum(fc, axis=-2)
        diff = G[:, :, :, None, :] - G[:, :, None, :, :]
        dec = jnp.where(incl, jnp.exp(jnp.where(incl, diff, 0.0)), 0.0)
        attn = jnp.einsum('bhtk,bhsk,bhtsk->bhts', qc, kc, dec)
        o = jnp.einsum('bhtk,bhkv->bhtv', qc * jnp.exp(G), S) + jnp.einsum('bhts,bhsv->bhtv', attn, vc)
        S = S * jnp.exp(G[:, :, -1])[..., None] + jnp.einsum(
            'bhsk,bhsv->bhkv', kc * jnp.exp(G[:, :, -1:] - G), vc)
        return S, o

    S, o = lax.scan(step, s0, (qs, ks, vs, fs))
    return _unchunk(o, L), S


def _short_conv(u, buf, w):
    L = u.shape[1]
    xc = jnp.concatenate([buf, u], 1)
    y = sum(xc[:, j:j + L] * w[j] for j in range(DN_CONV))
    return jax.nn.silu(y), xc[:, -(DN_CONV - 1):]


def _deltanet_mixer(qkv_raw, a_raw, b_raw, gate_raw, conv_buf, s0, conv_w, a_log, dt_bias, norm_w):
    B, L, _ = qkv_raw.shape
    if conv_buf is None:
        conv_buf = jnp.zeros((B, DN_CONV - 1, DN_CONV_DIM), qkv_raw.dtype)
    if s0 is None:
        s0 = jnp.zeros((B, DN_HEADS, DN_DK, DN_DV), qkv_raw.dtype)
    u, new_buf = _short_conv(qkv_raw, conv_buf, conv_w)
    if L % DN_CHUNK == 0:
        beta = jax.nn.sigmoid(b_raw)
        g = -jnp.exp(a_log) * jax.nn.softplus(a_raw + dt_bias)
        o, s_new = _dn_scan(u, g, beta, gate_raw, norm_w, s0)
        return o, new_buf, s_new
    q, k, v = _split(u, (DN_HEADS * DN_DK, DN_HEADS * DN_DK, DN_HEADS * DN_DV))
    q = _l2norm(q.reshape(B, L, DN_HEADS, DN_DK)) * DN_DK ** -0.5
    k = _l2norm(k.reshape(B, L, DN_HEADS, DN_DK))
    v = v.reshape(B, L, DN_HEADS, DN_DV)
    beta = jax.nn.sigmoid(b_raw)
    g = -jnp.exp(a_log) * jax.nn.softplus(a_raw + dt_bias)
    o, s_new = _gated_delta(q, k, v, beta, g, s0)
    return _gated_rmsnorm(o, gate_raw, norm_w), new_buf, s_new


def _hgrn2_mixer(q_raw, f_raw, i_raw, gate_raw, s0, lb, norm_w):
    B, L, _ = q_raw.shape
    if s0 is None:
        s0 = jnp.zeros((B, HG_HEADS, HG_DK, HG_DV), q_raw.dtype)
    if L % HG_CHUNK == 0:
        return _hg_scan(q_raw, f_raw, i_raw, gate_raw, lb, norm_w, s0)
    q = q_raw.reshape(B, L, HG_HEADS, HG_DK) * HG_DK ** -0.5
    f = lb + (1.0 - lb) * jax.nn.sigmoid(f_raw)
    logf = jnp.log(f).reshape(B, L, HG_HEADS, HG_DK)
    k = (1.0 - f).reshape(B, L, HG_HEADS, HG_DK)
    v = i_raw.reshape(B, L, HG_HEADS, HG_DV)
    o, s_new = _gla_scan(q, k, v, logf, s0)
    return _gated_rmsnorm(o, gate_raw, norm_w), s_new


def _moe(x, w_router, router_bias, wg, wu, wd, precise):
    f32 = jnp.float32
    N, D = x.shape
    eidx, gw = _router(x, w_router, router_bias, precise)
    n_assign = N * TOP_K
    e_flat = eidx.reshape(-1)
    onehot = (e_flat[:, None] == jnp.arange(N_EXPERTS)[None, :]).astype(jnp.int32)
    running = jnp.cumsum(onehot, axis=0)
    counts = running[-1]
    rank = jnp.take_along_axis(running, e_flat[:, None], axis=1)[:, 0] - 1
    padded = (counts + MOE_ROW_BLOCK - 1) // MOE_ROW_BLOCK * MOE_ROW_BLOCK
    pad_end = jnp.cumsum(padded)
    dest = ((pad_end - padded)[e_flat] + rank).astype(jnp.int32)
    n_blocks = -(-n_assign // MOE_ROW_BLOCK) + N_EXPERTS
    n_rows = n_blocks * MOE_ROW_BLOCK
    tok = jnp.full((n_rows,), N, jnp.int32).at[dest].set(jnp.arange(n_assign, dtype=jnp.int32) // TOP_K)
    wt = jnp.zeros((n_rows,), f32).at[dest].set(gw.reshape(-1))
    blk_exp = jnp.minimum(jnp.searchsorted(pad_end, jnp.arange(n_blocks) * MOE_ROW_BLOCK, side='right'),
                          N_EXPERTS - 1).astype(jnp.int32)
    x_pad = jnp.concatenate([x, jnp.zeros((1, D), x.dtype)])
    out = _expert_ffn(x_pad[tok], wt, blk_exp, wg, wu, wd, precise)
    slot = dest.reshape(N, TOP_K)
    return out[slot[:, 0]] + out[slot[:, 1]]


def _trunk_layer(x, ple, lb, past, precise, w_in, w_cmp1, w_cmp2, pe_cmp, dn_conv_w, dn_a_log, dn_dt_bias,
                 dn_norm_w, hg_norm_w, w_branch, w_out, ln1_g, ln1_b, ln2_g, ln2_b, w_router,
                 router_bias, w_exp_gate, w_exp_up, w_exp_down, w_ple, w_ple_gate, rel_bias):
    nsa_past, conv_buf, dn_state, hg_state = past
    B, L, D = x.shape
    x2 = x.reshape(B * L, D)
    mm = functools.partial(_mm, precise=precise)
    z = mm(x2, w_in).reshape(B, L, -1)
    (nsa_q, nsa_kv, nsa_g, dn_qkv, dn_a, dn_b, dn_g,
     hg_q, hg_f, hg_i, hg_g, merge) = _split(z, IN_SPLITS)
    o_a, rows_new, win_new = _nsa_mixer(nsa_q, nsa_kv, nsa_g, nsa_past, w_cmp1, w_cmp2, pe_cmp, rel_bias)
    o_b, conv_new, dn_new = _deltanet_mixer(dn_qkv, dn_a, dn_b, dn_g, conv_buf, dn_state, dn_conv_w,
                                            dn_a_log, dn_dt_bias, dn_norm_w)
    o_c, hg_new = _hgrn2_mixer(hg_q, hg_f, hg_i, hg_g, hg_state, lb, hg_norm_w)
    g = jax.nn.sigmoid(merge).reshape(B * L, 3, D)
    m = (g[:, 0] * mm(o_a.reshape(B * L, -1), w_branch[0]) + g[:, 1] * mm(o_b.reshape(B * L, -1), w_branch[1])
         + g[:, 2] * mm(o_c.reshape(B * L, -1), w_branch[2]))
    x2 = _layer_norm(DEEPNORM_ALPHA * x2 + mm(m, w_out), ln1_g, ln1_b)
    ffn = _moe(x2, w_router, router_bias, w_exp_gate, w_exp_up, w_exp_down, precise)
    ple_term = mm(ple.reshape(B * L, -1), w_ple) * jax.nn.sigmoid(mm(x2, w_ple_gate))
    x2 = _layer_norm(DEEPNORM_ALPHA * x2 + ffn + ple_term, ln2_g, ln2_b)
    return x2.reshape(B, L, D), (rows_new, win_new, conv_new, dn_new, hg_new)


def kernel(x_prompt, x_sample, cache_nsa, state_win, state_dn_conv, state_dn, state_hg, page_table,
           p_prompt, p_sample, w_in, w_cmp1, w_cmp2, pe_cmp, dn_conv_w, dn_a_log, dn_dt_bias, dn_norm_w,
           hg_lb_logits, hg_norm_w, w_branch, w_out, ln1_g, ln1_b, ln2_g, ln2_b, w_router, router_bias,
           w_exp_gate, w_exp_up, w_exp_down, w_ple, w_ple_gate, rel_bias):
    lb = jnp.cumsum(jax.nn.softmax(hg_lb_logits, axis=0), axis=0)
    lb = lb - lb[0:1]
    n_dec = page_table.shape[0]
    y_prompt, y_sample = x_prompt, x_sample
    st_p, st_s = [], []
    for l in range(DEPTH):
        lw = (w_in[l], w_cmp1[l], w_cmp2[l], pe_cmp[l], dn_conv_w[l], dn_a_log[l], dn_dt_bias[l],
              dn_norm_w[l], hg_norm_w[l], w_branch[l], w_out[l], ln1_g[l], ln1_b[l], ln2_g[l], ln2_b[l],
              w_router, router_bias, w_exp_gate[l], w_exp_up[l], w_exp_down[l], w_ple[l], w_ple_gate[l],
              rel_bias)
        y_prompt, s = _trunk_layer(y_prompt, p_prompt[l], lb[l], (None, None, None, None), False, *lw)
        st_p.append(s)
        y_sample, s = _trunk_layer(y_sample, p_sample[l], lb[l],
                                   ((cache_nsa, l, page_table, state_win[l]), state_dn_conv[l], state_dn[l],
                                    state_hg[l]),
                                   False, *lw)
        st_s.append(s)
    outs = [y_prompt, y_sample]
    for i in range(5):
        outs.append(jnp.stack([s[i] for s in st_p]))
        outs.append(jnp.stack([s[i] for s in st_s]))
    return tuple(outs)
```

```python
import functools
import math

import jax
import jax.numpy as jnp
import numpy as np
from jax import lax
from jax.experimental import pallas as pl
from jax.experimental.pallas import tpu as pltpu

D_MODEL = 1024
DEPTH = 4
NSA_HEADS = 8
NSA_KV = 2
NSA_GROUP = NSA_HEADS // NSA_KV
NSA_DH = 64
CMP_STRIDE = 16
CMP_BLOCK = 2 * CMP_STRIDE
CMP_HID = 128
SEL_BLOCK = 64
N_SELECT = 16
WINDOW = 512
Q_BLOCK = 128
N_KV_STREAMS = 6
N_CACHE_STREAMS = 4
N_BUCKETS = 32
MAX_DISTANCE = 128
DN_HEADS = 4
DN_DK = 128
DN_DV = 128
DN_CONV = 4
DN_CHUNK = 64
DN_CONV_DIM = DN_HEADS * (2 * DN_DK + DN_DV)
HG_HEADS = 4
HG_DK = 128
HG_DV = 128
HG_CHUNK = 64
N_EXPERTS = 16
N_EXPERT_GROUPS = 4
EXPERTS_PER_GROUP = N_EXPERTS // N_EXPERT_GROUPS
TOP_K = 2
D_EXPERT = 512
MOE_ROW_BLOCK = 128
DEEPNORM_ALPHA = (2 * DEPTH) ** 0.25
EPS = 1e-6
NEG = -1e30
IN_SPLITS = (NSA_HEADS * NSA_DH, N_KV_STREAMS * NSA_KV * NSA_DH, 3 * NSA_HEADS,
             DN_CONV_DIM, DN_HEADS, DN_HEADS, DN_HEADS * DN_DV,
             HG_HEADS * HG_DK, HG_HEADS * HG_DK, HG_HEADS * HG_DV, HG_HEADS * HG_DV,
             3 * D_MODEL)

VMEM_LIMIT_BYTES = 48 * 1024 * 1024
MM_ROW_TILE = 1024
MM_COL_TILE = 512


def _mm_body(x_ref, w_ref, o_ref, *, precise):
    if precise:
        o_ref[...] = jnp.dot(x_ref[...], w_ref[...], preferred_element_type=jnp.float32,
                             precision=lax.Precision.HIGHEST)
    else:
        o_ref[...] = jnp.dot(x_ref[...].astype(jnp.bfloat16), w_ref[...].astype(jnp.bfloat16),
                             preferred_element_type=jnp.float32)


def _mm(x, w, precise=False):
    M, K = x.shape
    N = w.shape[1]
    if not precise:
        x, w = x.astype(jnp.bfloat16), w.astype(jnp.bfloat16)
    tm = min(MM_ROW_TILE, M)
    tn = min(MM_COL_TILE, N)
    return pl.pallas_call(
        functools.partial(_mm_body, precise=precise),
        grid=(pl.cdiv(M, tm), pl.cdiv(N, tn)),
        in_specs=[pl.BlockSpec((tm, K), lambda i, j: (i, 0)),
                  pl.BlockSpec((K, tn), lambda i, j: (0, j))],
        out_specs=pl.BlockSpec((tm, tn), lambda i, j: (i, j)),
        out_shape=jax.ShapeDtypeStruct((M, N), jnp.float32),
        compiler_params=pltpu.CompilerParams(
            dimension_semantics=("parallel", "parallel"),
            vmem_limit_bytes=VMEM_LIMIT_BYTES),
        name="mm",
    )(x, w)


def _expert_body(blk_exp_ref, h_ref, wt_ref, wg_ref, wu_ref, wd_ref, o_ref, *, precise):
    del blk_exp_ref
    if precise:
        dot = functools.partial(jnp.dot, preferred_element_type=jnp.float32,
                                precision=lax.Precision.HIGHEST)
        h = h_ref[...]
        a = jax.nn.silu(dot(h, wg_ref[0])) * dot(h, wu_ref[0])
        o_ref[...] = dot(a, wd_ref[0]) * wt_ref[...]
    else:
        bf = jnp.bfloat16
        dot = functools.partial(jnp.dot, preferred_element_type=jnp.float32)
        h = h_ref[...].astype(bf)
        a = jax.nn.silu(dot(h, wg_ref[0].astype(bf))) * dot(h, wu_ref[0].astype(bf))
        o_ref[...] = dot(a.astype(bf), wd_ref[0].astype(bf)) * wt_ref[...]


def _expert_ffn(h, wt, blk_exp, wg, wu, wd, precise):
    n_rows, D = h.shape
    n_blocks = n_rows // MOE_ROW_BLOCK
    if not precise:
        h, wg, wu, wd = (a.astype(jnp.bfloat16) for a in (h, wg, wu, wd))
    grid_spec = pltpu.PrefetchScalarGridSpec(
        num_scalar_prefetch=1,
        grid=(n_blocks,),
        in_specs=[pl.BlockSpec((MOE_ROW_BLOCK, D), lambda i, be: (i, 0)),
                  pl.BlockSpec((MOE_ROW_BLOCK, 1), lambda i, be: (i, 0)),
                  pl.BlockSpec((1, D, D_EXPERT), lambda i, be: (be[i], 0, 0)),
                  pl.BlockSpec((1, D, D_EXPERT), lambda i, be: (be[i], 0, 0)),
                  pl.BlockSpec((1, D_EXPERT, D), lambda i, be: (be[i], 0, 0))],
        out_specs=pl.BlockSpec((MOE_ROW_BLOCK, D), lambda i, be: (i, 0)),
    )
    return pl.pallas_call(
        functools.partial(_expert_body, precise=precise),
        grid_spec=grid_spec,
        out_shape=jax.ShapeDtypeStruct((n_rows, D), jnp.float32),
        compiler_params=pltpu.CompilerParams(
            dimension_semantics=("arbitrary",),
            vmem_limit_bytes=VMEM_LIMIT_BYTES),
        name="expert_ffn",
    )(blk_exp, h, wt.reshape(n_rows, 1), wg, wu, wd)


NSA_ROWS = NSA_GROUP * Q_BLOCK
NEAR_CMP = 32
NEAR_CMP_BACK = 16
CMP_PER_QBLOCK = Q_BLOCK // CMP_STRIDE
SEL_PER_TILE = Q_BLOCK // SEL_BLOCK
WIN_TILES = WINDOW // Q_BLOCK
MXU_DTYPE = jnp.bfloat16


MASK_OFF = -1e30
FAR_TILES = 8
SEL_SHIFT = SEL_BLOCK.bit_length() - 1
QB_SHIFT = Q_BLOCK.bit_length() - 1
assert 1 << SEL_SHIFT == SEL_BLOCK and 1 << QB_SHIFT == Q_BLOCK


def _nsa_prompt_body(q_ref, gt_ref, kc_ref, vct_ref, vc_ref, ks_ref, vst_ref, kw_ref, vwt_ref,
                       band_ref, near_ref, far_ref, winb_ref, imapt_ref, o_ref, m_scr, l_scr, acc_scr, z_scr):
    f32 = jnp.float32
    R, QB = NSA_ROWS, Q_BLOCK
    i = pl.program_id(2)
    n_cmp_pad = vct_ref.shape[-1]
    n_sb = imapt_ref.shape[0]
    qt = q_ref[0, 0, 0]
    far_bias = far_ref[0]
    r_lane = lax.broadcasted_iota(jnp.int32, (1, R), 1) & (QB - 1)
    t_lane = i * QB + r_lane

    def mxu(a, b):
        return jnp.dot(a.astype(MXU_DTYPE), b.astype(MXU_DTYPE), preferred_element_type=f32)

    def recip(x):
        r0 = 1.0 / x
        return r0 * (2.0 - x * r0)

    c0 = CMP_PER_QBLOCK * i - NEAR_CMP_BACK
    kc_far = kc_ref[0, 0, NEAR_CMP_BACK:NEAR_CMP_BACK + n_cmp_pad, :]
    lg_far = mxu(kc_far, qt) + far_bias
    mk_far = lax.broadcasted_iota(jnp.int32, (n_cmp_pad, R), 0) < c0
    near_start = pl.multiple_of(CMP_PER_QBLOCK * i, 8)
    k_near = kc_ref[0, 0, pl.ds(near_start, NEAR_CMP), :]
    v_near = vc_ref[0, 0, pl.ds(near_start, NEAR_CMP), :]
    lg_near = mxu(k_near, qt) + near_ref[0]
    c_near = c0 + lax.broadcasted_iota(jnp.int32, (NEAR_CMP, R), 0)
    mk_near = (c_near >= 0) & (c_near * CMP_STRIDE + (CMP_BLOCK - 1) <= t_lane)
    z_far = jnp.where(mk_far, lg_far, NEG)
    z_near = jnp.where(mk_near, lg_near, NEG)
    mx = jnp.maximum(z_far.max(0, keepdims=True), z_near.max(0, keepdims=True))
    e_far = jnp.where(mk_far, jnp.exp(z_far - mx), 0.0)
    e_near = jnp.where(mk_near, jnp.exp(z_near - mx), 0.0)
    s = e_far.sum(0, keepdims=True) + e_near.sum(0, keepdims=True)
    inv = recip(jnp.where(s > 0, s, 1.0))
    p_far = (e_far * inv).astype(MXU_DTYPE)
    p_near = (e_near * inv).astype(MXU_DTYPE)
    o_cmp = mxu(vct_ref[0, 0], p_far) + mxu(v_near.T, p_near)

    nn = lax.broadcasted_iota(jnp.int32, (n_sb, NEAR_CMP), 0)
    cn = c0 + lax.broadcasted_iota(jnp.int32, (n_sb, NEAR_CMP), 1)
    per_sel = SEL_BLOCK // CMP_STRIDE
    imapt_near = ((cn >= per_sel * nn - 1) & (cn <= per_sel * nn + per_sel - 1)).astype(f32)
    imp_rows = mxu(imapt_ref[...], p_far) + mxu(imapt_near, p_near)
    imp = imp_rows[:, 0:QB]
    for j in range(1, NSA_GROUP):
        imp = imp + imp_rows[:, j * QB:(j + 1) * QB]
    n_io = lax.broadcasted_iota(jnp.int32, (n_sb, QB), 0)
    n_f = n_io.astype(f32)
    r_q = lax.broadcasted_iota(jnp.int32, (n_sb, QB), 1)
    t_q = i * QB + r_q
    cur = SEL_PER_TILE * i + (r_q >= SEL_BLOCK).astype(jnp.int32)
    forced = (n_io == 0) | (n_io == cur) | (n_io == cur - 1)
    score = jnp.where(n_io * SEL_BLOCK <= t_q, jnp.where(forced, jnp.inf, imp), -jnp.inf)

    def pick_one(_, carry):
        score, sel = carry
        top = score.max(0, keepdims=True)
        first = jnp.where(score == top, n_f, float(n_sb)).min(0, keepdims=True)
        hit = n_f == first
        sel = jnp.where(hit & (top > -jnp.inf), 1.0, sel)
        return jnp.where(hit, -jnp.inf, score), sel

    _, sel = lax.fori_loop(0, min(N_SELECT, n_sb), pick_one, (score, jnp.zeros((n_sb, QB), f32)))
    sel = sel.astype(MXU_DTYPE)

    c_t = lax.broadcasted_iota(jnp.int32, (QB, R), 0)
    r_t = lax.broadcasted_iota(jnp.int32, (QB, R), 1) & (QB - 1)
    causal_off = jnp.where(c_t <= r_t, 0.0, MASK_OFF)

    def put(start, rows, z):
        z_scr[pl.ds(start, rows), :] = z
        m_scr[...] = jnp.maximum(m_scr[...], z.max(0, keepdims=True))

    def sel_off(kt):
        blk = SEL_PER_TILE * kt + (lax.broadcasted_iota(jnp.int32, (QB, n_sb), 0) >= SEL_BLOCK).astype(jnp.int32)
        expand = (lax.broadcasted_iota(jnp.int32, (QB, n_sb), 1) == blk).astype(MXU_DTYPE)
        off = (mxu(expand, sel) - 1.0) * (-MASK_OFF)
        return jnp.concatenate([off] * NSA_GROUP, axis=1)

    m_scr[...] = jnp.full((1, R), NEG, f32)
    KF = FAR_TILES * QB
    n_chunks = i // FAR_TILES + 1
    key_f = lax.broadcasted_iota(jnp.int32, (KF, n_sb), 0)
    n_f2 = lax.broadcasted_iota(jnp.int32, (KF, n_sb), 1)
    tile_f = lax.broadcasted_iota(jnp.int32, (KF, 1), 0) >> QB_SHIFT

    def sel_far(it, carry):
        start = pl.multiple_of(it * KF, KF)
        lg = mxu(ks_ref[0, 0, pl.ds(start, KF), :], qt)
        expand = (n_f2 == it * (KF // SEL_BLOCK) + (key_f >> SEL_SHIFT)).astype(MXU_DTYPE)
        off = (mxu(expand, sel) - 1.0) * (-MASK_OFF)
        off = off + jnp.where(it * FAR_TILES + tile_f < i - 1, 0.0, MASK_OFF)
        put(start, KF, lg + far_bias + jnp.concatenate([off] * NSA_GROUP, axis=1))
        return carry

    lax.fori_loop(0, n_chunks, sel_far, 0)

    @pl.when(i >= 1)
    def _():
        start = pl.multiple_of((i - 1) * QB, QB)
        put(start, QB, mxu(ks_ref[0, 0, pl.ds(start, QB), :], qt) + band_ref[0, 1] + sel_off(i - 1))

    diag = pl.multiple_of(i * QB, QB)
    put(diag, QB, mxu(ks_ref[0, 0, pl.ds(diag, QB), :], qt) + band_ref[0, 0] + (sel_off(i) + causal_off))
    m_sel = m_scr[...]
    l_scr[...] = jnp.zeros((1, R), f32)

    def sel_exp(it, carry):
        start = pl.multiple_of(it * KF, KF)
        e = jnp.exp(z_scr[pl.ds(start, KF), :] - m_sel)
        z_scr[pl.ds(start, KF), :] = e
        l_scr[...] = l_scr[...] + e.sum(0, keepdims=True)
        return carry

    lax.fori_loop(0, n_chunks, sel_exp, 0)
    inv_sel = recip(l_scr[...])
    acc_scr[...] = jnp.zeros_like(acc_scr)

    def sel_pv(it, carry):
        start = pl.multiple_of(it * KF, KF)
        p = (z_scr[pl.ds(start, KF), :] * inv_sel).astype(MXU_DTYPE)
        acc_scr[...] = acc_scr[...] + mxu(vst_ref[0, 0, :, pl.ds(start, KF)], p)
        return carry

    lax.fori_loop(0, n_chunks, sel_pv, 0)
    o_sel = acc_scr[...]

    start_d = pl.multiple_of((i + WIN_TILES) * QB, QB)
    z_d = mxu(kw_ref[0, 0, pl.ds(start_d, QB), :], qt) + band_ref[0, 0] + causal_off
    start_w = pl.multiple_of(i * QB, QB)
    before_start = (lax.broadcasted_iota(jnp.int32, (WINDOW, 1), 0) >> QB_SHIFT) < WIN_TILES - i
    z_w = mxu(kw_ref[0, 0, pl.ds(start_w, WINDOW), :], qt) + winb_ref[0] + jnp.where(before_start, MASK_OFF, 0.0)
    m_win = jnp.maximum(z_d.max(0, keepdims=True), z_w.max(0, keepdims=True))
    e_d = jnp.exp(z_d - m_win)
    e_w = jnp.exp(z_w - m_win)
    inv_win = recip(e_d.sum(0, keepdims=True) + e_w.sum(0, keepdims=True))
    o_win = (mxu(vwt_ref[0, 0, :, pl.ds(start_d, QB)], (e_d * inv_win).astype(MXU_DTYPE))
             + mxu(vwt_ref[0, 0, :, pl.ds(start_w, WINDOW)], (e_w * inv_win).astype(MXU_DTYPE)))
    gt = gt_ref[0, 0, 0]
    o_ref[0, 0, 0] = gt[0:1] * o_cmp + gt[1:2] * o_sel + gt[2:3] * o_win


def _nsa_prompt_attention(q, gates, kc, vc, ksel, vsel, kwin, vwin, rel_bias):
    f32 = jnp.float32
    B, L = q.shape[:2]
    G, J, dh, QB, R = NSA_KV, NSA_GROUP, NSA_DH, Q_BLOCK, NSA_ROWS
    assert L % QB == 0 and WINDOW % QB == 0 and QB % SEL_BLOCK == 0
    nq = L // QB
    n_cmp = kc.shape[1]
    n_cmp_pad = L // CMP_STRIDE
    n_sb = L // SEL_BLOCK
    assert n_cmp == n_cmp_pad - 1

    def cols(a):
        x = a.shape[-1]
        return a.reshape(B, nq, QB, G, J, x).transpose(0, 3, 1, 5, 4, 2).reshape(B, G, nq, x, R)

    q_c = cols(q).astype(MXU_DTYPE)
    gt_c = cols(gates)
    pad = ((0, 0), (NEAR_CMP_BACK, n_cmp_pad - n_cmp + NEAR_CMP - NEAR_CMP_BACK), (0, 0), (0, 0))
    kc_p = jnp.pad(kc, pad).transpose(0, 2, 1, 3)
    vc_p = jnp.pad(vc, pad).transpose(0, 2, 1, 3)
    vct = vc_p[:, :, NEAR_CMP_BACK:NEAR_CMP_BACK + n_cmp_pad].transpose(0, 1, 3, 2).astype(MXU_DTYPE)
    ks = ksel.transpose(0, 2, 1, 3).astype(MXU_DTYPE)
    vst = vsel.transpose(0, 2, 3, 1).astype(MXU_DTYPE)
    kw = jnp.pad(kwin.transpose(0, 2, 1, 3), ((0, 0), (0, 0), (WINDOW, 0), (0, 0))).astype(MXU_DTYPE)
    vwt = jnp.pad(vwin.transpose(0, 2, 3, 1), ((0, 0), (0, 0), (0, 0), (WINDOW, 0))).astype(MXU_DTYPE)

    n_dist = MAX_DISTANCE
    by_dist = rel_bias[_t5_bucket(jnp.arange(n_dist))].astype(f32).reshape(n_dist, G, J)
    by_dist = by_dist.transpose(1, 2, 0)

    def table(dist):
        d = jnp.clip(dist, 0, n_dist - 1)
        return by_dist[:, :, d].transpose(0, 2, 1, 3).reshape(G, dist.shape[0], R)

    c = jnp.arange(QB)[:, None]
    r = jnp.arange(QB)[None, :]
    band = jnp.stack([table(r - c), table(r - c + QB)], axis=1)
    u = jnp.arange(NEAR_CMP)[:, None]
    near = table(r - CMP_STRIDE * (u - NEAR_CMP_BACK) - (CMP_BLOCK - 1))
    far = jnp.repeat(by_dist[:, :, n_dist - 1], QB, axis=1).reshape(G, 1, R)
    assert nq % FAR_TILES == 0
    edge = jnp.tile(jnp.where(c > r, 0.0, MASK_OFF), (1, J))[None] + far
    winb = jnp.concatenate([edge] + [jnp.broadcast_to(far, (G, QB, R))] * (WIN_TILES - 2) + [band[:, 1]], axis=1)
    imapt = _importance_map(n_cmp_pad, n_sb).T

    kv_spec = lambda shape: pl.BlockSpec((1, 1) + shape, lambda b, g, i: (b, g, 0, 0))
    out = pl.pallas_call(
        _nsa_prompt_body,
        grid=(B, G, nq),
        in_specs=[pl.BlockSpec((1, 1, 1, dh, R), lambda b, g, i: (b, g, i, 0, 0)),
                  pl.BlockSpec((1, 1, 1, 3, R), lambda b, g, i: (b, g, i, 0, 0)),
                  kv_spec((n_cmp_pad + NEAR_CMP, dh)),
                  kv_spec((dh, n_cmp_pad)),
                  kv_spec((n_cmp_pad + NEAR_CMP, dh)),
                  kv_spec((L, dh)), kv_spec((dh, L)), kv_spec((L + WINDOW, dh)), kv_spec((dh, L + WINDOW)),
                  pl.BlockSpec((1, 2, QB, R), lambda b, g, i: (g, 0, 0, 0)),
                  pl.BlockSpec((1, NEAR_CMP, R), lambda b, g, i: (g, 0, 0)),
                  pl.BlockSpec((1, 1, R), lambda b, g, i: (g, 0, 0)),
                  pl.BlockSpec((1, WINDOW, R), lambda b, g, i: (g, 0, 0)),
                  pl.BlockSpec((n_sb, n_cmp_pad), lambda b, g, i: (0, 0))],
        out_specs=pl.BlockSpec((1, 1, 1, dh, R), lambda b, g, i: (b, g, i, 0, 0)),
        out_shape=jax.ShapeDtypeStruct((B, G, nq, dh, R), f32),
        scratch_shapes=[pltpu.VMEM((1, R), f32), pltpu.VMEM((1, R), f32), pltpu.VMEM((dh, R), f32),
                        pltpu.VMEM((L, R), f32)],
        compiler_params=pltpu.CompilerParams(
            dimension_semantics=("parallel", "parallel", "arbitrary"),
            vmem_limit_bytes=VMEM_LIMIT_BYTES),
        name="nsa_prompt",
    )(q_c, gt_c, kc_p, vct, vc_p, ks, vst, kw, vwt, band, near, far, winb, imapt)
    return out.reshape(B, G, nq, dh, J, QB).transpose(0, 2, 5, 1, 4, 3).reshape(B, L, G * J * dh)


PAGE_SIZE = 128
ROW_WIDTH = N_CACHE_STREAMS * NSA_KV * NSA_DH
CMP_WIDTH = 2 * NSA_KV * NSA_DH
CHUNKS_PER_PAGE = PAGE_SIZE // CMP_STRIDE
PAGES_PER_STEP = 16


def _cmp_partial_body(x_ref, w_ref, o_ref):
    pg = x_ref.shape[0]
    chunk_lanes = CMP_STRIDE * ROW_WIDTH
    acc = None
    for s in range(CMP_STRIDE):
        lo = s * ROW_WIDTH
        lhs = jnp.concatenate([x_ref[:, c * chunk_lanes + lo:c * chunk_lanes + lo + CMP_WIDTH]
                               for c in range(CHUNKS_PER_PAGE)], axis=0)
        part = jnp.dot(lhs.astype(MXU_DTYPE), w_ref[s], preferred_element_type=jnp.float32)
        acc = part if acc is None else acc + part
    for c in range(CHUNKS_PER_PAGE):
        o_ref[0, c] = acc[c * pg:(c + 1) * pg]


def _compress_partials(pages, w1):
    f32 = jnp.float32
    n = pages.shape[0]
    G, dh = NSA_KV, NSA_DH
    pg = math.gcd(n, PAGES_PER_STEP)
    w = w1.reshape(2, 2, CMP_STRIDE, dh, CMP_HID)
    wb = jnp.einsum('nzsdh,nm,gk->sngdzmkh', w, jnp.eye(2, dtype=f32), jnp.eye(G, dtype=f32))
    n_out = 2 * 2 * G * CMP_HID
    wb = wb.reshape(CMP_STRIDE, CMP_WIDTH, n_out).astype(MXU_DTYPE)
    out = pl.pallas_call(
        _cmp_partial_body,
        grid=(n // pg,),
        in_specs=[pl.BlockSpec((pg, PAGE_SIZE * ROW_WIDTH), lambda i: (i, 0)),
                  pl.BlockSpec((CMP_STRIDE, CMP_WIDTH, n_out), lambda i: (0, 0, 0))],
        out_specs=pl.BlockSpec((1, CHUNKS_PER_PAGE, pg, n_out), lambda i: (i, 0, 0, 0)),
        out_shape=jax.ShapeDtypeStruct((n // pg, CHUNKS_PER_PAGE, pg, n_out), f32),
        compiler_params=pltpu.CompilerParams(dimension_semantics=("parallel",),
                                             vmem_limit_bytes=VMEM_LIMIT_BYTES),
        name="compress_partials",
    )(pages, wb)
    return out.transpose(0, 2, 1, 3).reshape(n * CHUNKS_PER_PAGE, 2, 2, G, CMP_HID)


def _hdot(a, b):
    return jnp.dot(a, b, preferred_element_type=jnp.float32, precision=lax.Precision.HIGHEST)


def _hdot_nt(a, b):
    return lax.dot_general(a, b, (((1,), (1,)), ((), ())), preferred_element_type=jnp.float32,
                           precision=lax.Precision.HIGHEST)


def _bdot(a, b):
    return jnp.dot(a.astype(MXU_DTYPE), b.astype(MXU_DTYPE), preferred_element_type=jnp.float32)


def _dot3(a, b):
    f32 = jnp.float32
    a_hi = a.astype(MXU_DTYPE)
    b_hi = b.astype(MXU_DTYPE)
    a_lo = (a - a_hi.astype(f32)).astype(MXU_DTYPE)
    b_lo = (b - b_hi.astype(f32)).astype(MXU_DTYPE)
    dot = functools.partial(jnp.dot, preferred_element_type=f32)
    return dot(a_hi, b_hi) + (dot(a_hi, b_lo) + dot(a_lo, b_hi))


def _bdot_nt(a, b):
    return lax.dot_general(a.astype(MXU_DTYPE), b.astype(MXU_DTYPE), (((1,), (1,)), ((), ())),
                           preferred_element_type=jnp.float32)


ROUTER_OUT_LANES = 128


def _router_body(x_ref, w_ref, b_ref, o_ref, *, precise):
    f32 = jnp.float32
    if precise:
        logits = _hdot(x_ref[...], w_ref[...])
    else:
        logits = jnp.dot(x_ref[...].astype(jnp.bfloat16), w_ref[...].astype(jnp.bfloat16),
                         preferred_element_type=f32)
    scores = jax.nn.sigmoid(logits)
    biased = scores + b_ref[...]
    tm = scores.shape[0]
    lane = lax.broadcasted_iota(jnp.int32, (tm, N_EXPERTS), 1).astype(f32)
    e_io = lax.broadcasted_iota(jnp.int32, (tm, N_EXPERTS), 1)
    grp = sum((e_io >= gi * EXPERTS_PER_GROUP).astype(jnp.int32) for gi in range(1, N_EXPERT_GROUPS))

    def top2(vals):
        m1 = vals.max(-1, keepdims=True)
        i1 = jnp.where(vals == m1, lane, float(N_EXPERTS)).min(-1, keepdims=True)
        rest = jnp.where(lane == i1, -jnp.inf, vals)
        m2 = rest.max(-1, keepdims=True)
        i2 = jnp.where(rest == m2, lane, float(N_EXPERTS)).min(-1, keepdims=True)
        return m1, i1, m2, i2

    best = None
    gsel = None
    for gi in range(N_EXPERT_GROUPS):
        m1, _, m2, _ = top2(jnp.where(grp == gi, biased, -jnp.inf))
        gsum = m1 + m2
        if best is None:
            best, gsel = gsum, jnp.zeros_like(gsum, dtype=jnp.int32)
        else:
            better = gsum > best
            best = jnp.where(better, gsum, best)
            gsel = jnp.where(better, gi, gsel)
    _, i1, _, i2 = top2(jnp.where(grp == gsel, biased, -jnp.inf))
    s1 = jnp.where(lane == i1, scores, 0.0).sum(-1, keepdims=True)
    s2 = jnp.where(lane == i2, scores, 0.0).sum(-1, keepdims=True)
    tot = s1 + s2
    col = lax.broadcasted_iota(jnp.int32, (tm, ROUTER_OUT_LANES), 1)
    o_ref[...] = jnp.where(col == 0, i1, jnp.where(col == 1, i2, jnp.where(col == 2, s1 / tot, s2 / tot)))


def _router(x, w_router, router_bias, precise):
    N, D = x.shape
    tm = min(MM_ROW_TILE, N)
    out = pl.pallas_call(
        functools.partial(_router_body, precise=precise),
        grid=(pl.cdiv(N, tm),),
        in_specs=[pl.BlockSpec((tm, D), lambda i: (i, 0)),
                  pl.BlockSpec((D, N_EXPERTS), lambda i: (0, 0)),
                  pl.BlockSpec((1, N_EXPERTS), lambda i: (0, 0))],
        out_specs=pl.BlockSpec((tm, ROUTER_OUT_LANES), lambda i: (i, 0)),
        out_shape=jax.ShapeDtypeStruct((N, ROUTER_OUT_LANES), jnp.float32),
        compiler_params=pltpu.CompilerParams(dimension_semantics=("parallel",),
                                             vmem_limit_bytes=VMEM_LIMIT_BYTES),
        name="router",
    )(x, w_router, router_bias.reshape(1, N_EXPERTS))
    return out[:, 0:2].astype(jnp.int32), out[:, 2:4]


def _tri(c, strict=False, upper=False):
    r = lax.broadcasted_iota(jnp.int32, (c, c), 0)
    s = lax.broadcasted_iota(jnp.int32, (c, c), 1)
    if upper:
        r, s = s, r
    return (r > s) if strict else (r >= s)


def _dn_body(u_ref, gcol_ref, grow_ref, beta_ref, gate_ref, nw_ref, s0_ref, o_ref, s_ref):
    f32 = jnp.float32
    nb, c = u_ref.shape[0], u_ref.shape[1]
    dk, dv = DN_DK, DN_DV

    @pl.when(pl.program_id(0) == 0)
    def _():
        s_ref[...] = s0_ref[...]

    incl = _tri(c)
    strict = _tri(c, strict=True)
    eye = (lax.broadcasted_iota(jnp.int32, (c, c), 0) == lax.broadcasted_iota(jnp.int32, (c, c), 1)).astype(f32)
    nw = nw_ref[...]
    pairs = [(b, h) for b in range(nb) for h in range(DN_HEADS)]
    each = lambda f, *cols: [f(*args) for args in zip(*cols)]
    states = [s_ref[b, h] for b, h in pairs]
    gates = [gate_ref[b, :, h * dv:(h + 1) * dv] for b, h in pairs]
    q = [u_ref[b, :, h * dk:(h + 1) * dk] for b, h in pairs]
    k = [u_ref[b, :, DN_HEADS * dk + h * dk:DN_HEADS * dk + (h + 1) * dk] for b, h in pairs]
    v = [u_ref[b, :, 2 * DN_HEADS * dk + h * dv:2 * DN_HEADS * dk + (h + 1) * dv] for b, h in pairs]
    gc = [gcol_ref[b, 0][:, h:h + 1] for b, h in pairs]
    gr = [grow_ref[b, 0][h:h + 1, :] for b, h in pairs]
    beta = [beta_ref[b, 0][:, h:h + 1] for b, h in pairs]
    q = each(lambda x: x * lax.rsqrt(jnp.sum(x * x, -1, keepdims=True) + EPS) * dk ** -0.5, q)
    k = each(lambda x: x * lax.rsqrt(jnp.sum(x * x, -1, keepdims=True) + EPS), k)
    dec = each(lambda a, b: jnp.where(incl, jnp.exp(jnp.where(incl, a - b, 0.0)), 0.0), gc, gr)
    kb = each(lambda a, b: a * b, k, beta)
    a = each(lambda x, y, d: jnp.where(strict, _bdot_nt(x, y) * d, 0.0), kb, k, dec)
    p = each(lambda x: -x, a)
    tm = each(lambda x: eye + x, p)
    for _ in range(max(1, math.ceil(math.log2(c))) - 1):
        p = each(lambda x: _dot3(x, x), p)
        tm = each(lambda t, x: t + _dot3(t, x), tm, p)
    eg = each(jnp.exp, gc)
    u = each(lambda t, x, bt: _bdot(t, x * bt), tm, v, beta)
    w = each(lambda t, x, e: _bdot(t, x * e), tm, kb, eg)
    v_new = each(lambda x, y, s: x - _bdot(y, s), u, w, states)
    attn = each(lambda x, y, d: _bdot_nt(x, y) * d, q, k, dec)
    o = each(lambda x, e, s, at, vn: _bdot(x * e, s) + _bdot(at, vn), q, eg, states, attn, v_new)
    g_last = each(lambda x: x[c - 1:c, :], gc)
    new_states = each(lambda s, gl, x, g, vn: s * jnp.exp(gl) + _bdot((x * jnp.exp(gl - g)).T, vn),
                      states, g_last, k, gc, v_new)
    o = each(lambda x, g: x * lax.rsqrt(jnp.mean(x * x, -1, keepdims=True) + EPS) * nw * jax.nn.silu(g), o, gates)
    for (b, h), s_new, o_bh in zip(pairs, new_states, o):
        s_ref[b, h] = s_new
        o_ref[b, :, h * dv:(h + 1) * dv] = o_bh


def _dn_scan(u, g, beta, gate, norm_w, s0):
    f32 = jnp.float32
    B, L, _ = u.shape
    c = min(DN_CHUNK, L)
    assert L % c == 0
    n = L // c
    H = DN_HEADS
    g4 = jnp.cumsum(g.reshape(B, n, c, H), axis=2)
    return pl.pallas_call(
        _dn_body,
        grid=(n,),
        in_specs=[pl.BlockSpec((B, c, DN_CONV_DIM), lambda i: (0, i, 0)),
                  pl.BlockSpec((B, 1, c, H), lambda i: (0, i, 0, 0)),
                  pl.BlockSpec((B, 1, H, c), lambda i: (0, i, 0, 0)),
                  pl.BlockSpec((B, 1, c, H), lambda i: (0, i, 0, 0)),
                  pl.BlockSpec((B, c, H * DN_DV), lambda i: (0, i, 0)),
                  pl.BlockSpec((1, DN_DV), lambda i: (0, 0)),
                  pl.BlockSpec((B, H, DN_DK, DN_DV), lambda i: (0, 0, 0, 0))],
        out_specs=[pl.BlockSpec((B, c, H * DN_DV), lambda i: (0, i, 0)),
                   pl.BlockSpec((B, H, DN_DK, DN_DV), lambda i: (0, 0, 0, 0))],
        out_shape=[jax.ShapeDtypeStruct((B, L, H * DN_DV), f32),
                   jax.ShapeDtypeStruct((B, H, DN_DK, DN_DV), f32)],
        compiler_params=pltpu.CompilerParams(dimension_semantics=("arbitrary",),
                                             vmem_limit_bytes=VMEM_LIMIT_BYTES),
        name="deltanet_scan",
    )(u, g4, g4.transpose(0, 1, 3, 2), beta.reshape(B, n, c, H), gate, norm_w.reshape(1, DN_DV), s0)


HG_SUB = 16


def _hg_body(q_ref, f_ref, i_ref, gate_ref, lb_ref, nw_ref, s0_ref, o_ref, s_ref):
    f32 = jnp.float32
    nb, c = q_ref.shape[0], q_ref.shape[1]
    dk, dv = HG_DK, HG_DV
    sub = min(HG_SUB, c)

    @pl.when(pl.program_id(0) == 0)
    def _():
        s_ref[...] = s0_ref[...]

    lower = _tri(c).astype(f32)
    t_sub = lax.broadcasted_iota(jnp.int32, (sub, 1), 0)
    nw = nw_ref[...]
    pairs = [(b, h) for b in range(nb) for h in range(HG_HEADS)]
    each = lambda fn, *cols: [fn(*args) for args in zip(*cols)]
    st = [s_ref[b, h] for b, h in pairs]
    gates = [gate_ref[b, :, h * dv:(h + 1) * dv] for b, h in pairs]
    q = [q_ref[b, :, h * dk:(h + 1) * dk] * dk ** -0.5 for b, h in pairs]
    lbs = [lb_ref[:, h * dk:(h + 1) * dk] for b, h in pairs]
    f = [lb + (1.0 - lb) * jax.nn.sigmoid(f_ref[b, :, h * dk:(h + 1) * dk]) for (b, h), lb in zip(pairs, lbs)]
    k = each(lambda x: 1.0 - x, f)
    v = [i_ref[b, :, h * dv:(h + 1) * dv] for b, h in pairs]
    gcum = each(lambda x: _hdot(lower, jnp.log(x)), f)
    o_inter = each(lambda x, g, s: _bdot_nt(x * jnp.exp(g), s), q, gcum, st)
    v_r = each(lambda x: x.astype(MXU_DTYPE).astype(f32), v)
    out_blocks = []
    for blk in range(c // sub):
        r0 = blk * sub
        q_i = each(lambda x: x[r0:r0 + sub], q)
        g_i = each(lambda x: x[r0:r0 + sub], gcum)
        o_i = each(lambda x: x[r0:r0 + sub], o_inter)
        if blk > 0:
            def earlier(qi, gi, g, kk, vv, oi):
                g_b = g[r0 - 1:r0]
                a = _hdot_nt(qi * jnp.exp(gi - g_b), kk[:r0] * jnp.exp(g_b - g[:r0]))
                return oi + _bdot(a, vv[:r0])
            o_i = each(earlier, q_i, g_i, gcum, k, v, o_i)
        for s in range(sub):
            seen = t_sub >= s

            def diag(qi, gi, g, kk, vr, oi):
                term = qi * kk[r0 + s:r0 + s + 1] * jnp.exp(jnp.where(seen, gi - g[r0 + s:r0 + s + 1], 0.0))
                a_col = jnp.where(seen, term.sum(-1, keepdims=True), 0.0)
                return oi + a_col.astype(MXU_DTYPE).astype(f32) * vr[r0 + s:r0 + s + 1]
            o_i = each(diag, q_i, g_i, gcum, k, v_r, o_i)
        out_blocks.append(each(
            lambda oi, g: oi * lax.rsqrt(jnp.mean(oi * oi, -1, keepdims=True) + EPS) * nw * jax.nn.silu(g[r0:r0 + sub]),
            o_i, gates))
    g_last = each(lambda g: g[c - 1:c], gcum)
    new_st = each(lambda s, gl, vv, kk, g: s * jnp.exp(gl) + _bdot(vv.T, kk * jnp.exp(gl - g)), st, g_last, v, k, gcum)
    for n, (b, h) in enumerate(pairs):
        s_ref[b, h] = new_st[n]
        for blk in range(c // sub):
            o_ref[b, blk * sub:(blk + 1) * sub, h * dv:(h + 1) * dv] = out_blocks[blk][n]


def _hg_scan(q_raw, f_raw, i_raw, gate, lb, norm_w, s0):
    f32 = jnp.float32
    B, L, _ = q_raw.shape
    c = min(HG_CHUNK, L)
    assert L % c == 0
    n = L // c
    H = HG_HEADS
    tok = lambda w: pl.BlockSpec((B, c, w), lambda i: (0, i, 0))
    o, st = pl.pallas_call(
        _hg_body,
        grid=(n,),
        in_specs=[tok(H * HG_DK), tok(H * HG_DK), tok(H * HG_DV), tok(H * HG_DV),
                  pl.BlockSpec((1, H * HG_DK), lambda i: (0, 0)),
                  pl.BlockSpec((1, HG_DV), lambda i: (0, 0)),
                  pl.BlockSpec((B, H, HG_DV, HG_DK), lambda i: (0, 0, 0, 0))],
        out_specs=[tok(H * HG_DV), pl.BlockSpec((B, H, HG_DV, HG_DK), lambda i: (0, 0, 0, 0))],
        out_shape=[jax.ShapeDtypeStruct((B, L, H * HG_DV), f32),
                   jax.ShapeDtypeStruct((B, H, HG_DV, HG_DK), f32)],
        compiler_params=pltpu.CompilerParams(dimension_semantics=("arbitrary",),
                                             vmem_limit_bytes=VMEM_LIMIT_BYTES),
        name="hgrn2_scan",
    )(q_raw, f_raw, i_raw, gate, lb.reshape(1, H * HG_DK), norm_w.reshape(1, HG_DV), s0.transpose(0, 1, 3, 2))
    return o, st.transpose(0, 1, 3, 2)


def _split(z, sizes):
    return jnp.split(z, np.cumsum(sizes)[:-1].tolist(), axis=-1)


def _layer_norm(x, g, b):
    mu = x.mean(-1, keepdims=True)
    var = jnp.square(x - mu).mean(-1, keepdims=True)
    return (x - mu) * lax.rsqrt(var + EPS) * g + b


def _masked_softmax(logits, mask):
    z = jnp.where(mask, logits.astype(jnp.float32), NEG)
    e = jnp.exp(z - z.max(-1, keepdims=True)) * mask
    s = e.sum(-1, keepdims=True)
    return e / jnp.where(s > 0, s, 1.0)


def _l2norm(a):
    return a * lax.rsqrt(jnp.sum(a * a, -1, keepdims=True) + EPS)


def _gated_rmsnorm(o, gate, w):
    B, L, H, dv = o.shape
    o = o * lax.rsqrt(jnp.mean(o * o, -1, keepdims=True) + EPS) * w
    return o.reshape(B, L, H * dv) * jax.nn.silu(gate)


def _t5_bucket(dist):
    n = jnp.maximum(dist, 0)
    exact = N_BUCKETS // 2
    rel = jnp.log(jnp.maximum(n, 1).astype(jnp.float32) / exact) / math.log(MAX_DISTANCE / exact)
    large = jnp.minimum(exact + (rel * (N_BUCKETS - exact)).astype(jnp.int32), N_BUCKETS - 1)
    return jnp.where(n < exact, n, large)


def _head_bias(dist, rel_bias):
    Q, K = dist.shape
    bias = rel_bias[_t5_bucket(dist)]
    return bias.reshape(Q, K, NSA_KV, NSA_GROUP).transpose(2, 3, 0, 1).astype(jnp.float32)


def _compress(rows, w1, w2, pe):
    B, T, G, dh = rows.shape
    n_chunk = T // CMP_STRIDE
    r = rows[:, :n_chunk * CMP_STRIDE].reshape(B, n_chunk, CMP_STRIDE, G, dh)
    w1h = w1.reshape(2, CMP_STRIDE, dh, CMP_HID)
    lo = jnp.einsum('bcsgd,sdh->bcgh', r, w1h[0])
    hi = jnp.einsum('bcsgd,sdh->bcgh', r, w1h[1])
    h = lo[:, :-1] + hi[:, 1:] + pe.reshape(-1) @ w1
    return jax.nn.silu(h) @ w2


def _importance_map(n_cmp, n_sb):
    c0 = jnp.arange(n_cmp) * CMP_STRIDE
    c1 = c0 + CMP_BLOCK - 1
    b0 = jnp.arange(n_sb) * SEL_BLOCK
    b1 = b0 + SEL_BLOCK - 1
    return ((c0[:, None] <= b1[None, :]) & (c1[:, None] >= b0[None, :])).astype(jnp.float32)


def _nsa_core(q, gates, t, kc, vc, cmp_end, imp_map, n_sb, fetch_sel, kw, vw, w_pos, rel_bias):
    f32 = jnp.float32
    B, Q = q.shape[:2]
    lg = jnp.einsum('bqgjd,bcgd->bgjqc', q, kc).astype(f32) + _head_bias(t[:, None] - cmp_end[None, :], rel_bias)
    p_cmp = _masked_softmax(lg, cmp_end[None, :] <= t[:, None])
    o_cmp = jnp.einsum('bgjqc,bcgd->bqgjd', p_cmp, vc)
    n_top = min(N_SELECT, n_sb)
    imp = jnp.einsum('bgjqc,cn->bgqn', p_cmp, imp_map)
    blk = jnp.arange(n_sb)[None, :]
    cur = (t // SEL_BLOCK)[:, None]
    forced = (blk == 0) | (blk == cur) | (blk == cur - 1)
    score = jnp.where(blk * SEL_BLOCK <= t[:, None], jnp.where(forced, jnp.inf, imp), -jnp.inf)
    top_s, top_i = lax.top_k(score, n_top)
    n_keys = n_top * SEL_BLOCK
    ks, vs = (a.reshape(B, NSA_KV, Q, n_keys, NSA_DH) for a in fetch_sel(top_i))
    s_pos = (top_i[..., None] * SEL_BLOCK + jnp.arange(SEL_BLOCK)).reshape(B, NSA_KV, Q, n_keys)
    s_ok = jnp.repeat(top_s > -jnp.inf, SEL_BLOCK, axis=-1) & (s_pos <= t[:, None])
    table = rel_bias.reshape(N_BUCKETS, NSA_KV, NSA_GROUP).transpose(1, 2, 0)
    bucket = _t5_bucket(t[:, None] - s_pos)
    onehot = (bucket[..., None] == jnp.arange(N_BUCKETS)).astype(f32)
    s_bias = jnp.einsum('bgqmn,gjn->bgjqm', onehot, table.astype(f32), precision=lax.Precision.HIGHEST)
    lg = jnp.einsum('bqgjd,bgqmd->bgjqm', q, ks).astype(f32) + s_bias
    p_sel = _masked_softmax(lg, s_ok[:, :, None])
    o_sel = jnp.einsum('bgjqm,bgqmd->bqgjd', p_sel, vs)
    w_ok = (w_pos[None, :] <= t[:, None]) & (w_pos[None, :] > t[:, None] - WINDOW) & (w_pos[None, :] >= 0)
    lg = jnp.einsum('bqgjd,bkgd->bgjqk', q, kw).astype(f32) + _head_bias(t[:, None] - w_pos[None, :], rel_bias)
    p_win = _masked_softmax(lg, w_ok)
    o_win = jnp.einsum('bgjqk,bkgd->bqgjd', p_win, vw)
    o = gates[..., 0:1] * o_cmp + gates[..., 1:2] * o_sel + gates[..., 2:3] * o_win
    return o.reshape(B, Q, NSA_HEADS * NSA_DH)


def _nsa_mixer(q_raw, kv_raw, gate_raw, past, w_cmp1, w_cmp2, pe_cmp, rel_bias):
    B, L, _ = q_raw.shape
    q = (q_raw * NSA_DH ** -0.5).reshape(B, L, NSA_KV, NSA_GROUP, NSA_DH)
    gates = jax.nn.sigmoid(gate_raw).reshape(B, L, NSA_KV, NSA_GROUP, 3)
    kv = kv_raw.reshape(B, L, N_KV_STREAMS, NSA_KV, NSA_DH)
    new_rows, new_win = kv[:, :, :N_CACHE_STREAMS], kv[:, :, N_CACHE_STREAMS:]
    if past is None:
        kc = _compress(new_rows[:, :, 0], w_cmp1[0], w_cmp2[0], pe_cmp[0])
        vc = _compress(new_rows[:, :, 1], w_cmp1[1], w_cmp2[1], pe_cmp[1])
        o = _nsa_prompt_attention(q, gates, kc, vc, new_rows[:, :, 2], new_rows[:, :, 3],
                                    new_win[:, :, 0], new_win[:, :, 1], rel_bias)
        return o, new_rows, new_win[:, -min(WINDOW, L):]

    cache_nsa, layer, page_table, win_buf = past
    assert L == 1 and cache_nsa.shape[2] == PAGE_SIZE
    n_pool = cache_nsa.shape[1]
    n_pages = page_table.shape[1]
    pos0 = n_pages * PAGE_SIZE
    pages = jnp.take(cache_nsa[layer].reshape(n_pool, PAGE_SIZE * ROW_WIDTH), page_table.reshape(-1),
                     axis=0).astype(MXU_DTYPE)
    part = _compress_partials(pages, w_cmp1)
    part = part.reshape((B, n_pages * CHUNKS_PER_PAGE) + part.shape[1:])

    def finish(n):
        h = part[:, :-1, 0, n] + part[:, 1:, 1, n] + pe_cmp[n].reshape(-1) @ w_cmp1[n]
        return jax.nn.silu(h) @ w_cmp2[n]

    kc, vc = finish(0), finish(1)
    n_cmp = kc.shape[1]
    cmp_end = jnp.arange(n_cmp) * CMP_STRIDE + CMP_BLOCK - 1
    n_past_blk = pos0 // SEL_BLOCK
    n_sb = n_past_blk + 1
    blk_per_page = PAGE_SIZE // SEL_BLOCK
    first = (jnp.arange(SEL_BLOCK) == 0)[:, None]

    def fetch_sel(top_i):
        n_top = top_i.shape[-1]
        idx = jnp.minimum(top_i, n_past_blk - 1)
        row = jnp.arange(B)[:, None, None, None] * n_pages + idx // blk_per_page
        got = jnp.take(pages, row.reshape(-1), axis=0)
        got = got.reshape(B, NSA_KV, 1, n_top, blk_per_page, SEL_BLOCK, N_CACHE_STREAMS, NSA_KV, NSA_DH)
        half = (idx % blk_per_page)[..., None, None]
        is_new = (top_i == n_past_blk)[..., None, None]
        out = []
        for stream in (2, 3):
            cached = jnp.stack([got[:, g, :, :, :, :, stream, g] for g in range(NSA_KV)], axis=1)
            block = cached[:, :, :, :, 0]
            for h in range(1, blk_per_page):
                block = jnp.where(half == h, cached[:, :, :, :, h], block)
            fresh = (new_rows[:, 0, stream][:, :, None, None, None, :] * first).astype(block.dtype)
            out.append(jnp.where(is_new, fresh, block))
        return out

    wrows = jnp.concatenate([win_buf, new_win], 1)
    n_buf = wrows.shape[1] - L
    o = _nsa_core(q, gates, pos0 + jnp.arange(L), kc, vc, cmp_end, _importance_map(n_cmp, n_sb), n_sb, fetch_sel,
                  wrows[:, :, 0], wrows[:, :, 1], pos0 - n_buf + jnp.arange(n_buf + L), rel_bias)
    return o, new_rows, wrows[:, -min(WINDOW, pos0 + L):]


def _chunk(a, c):
    B, L = a.shape[:2]
    n = -(-L // c)
    a = jnp.pad(a, [(0, 0), (0, n * c - L)] + [(0, 0)] * (a.ndim - 2))
    a = a.reshape((B, n, c) + a.shape[2:])
    return jnp.moveaxis(a, (1, 2), (0, 3))


def _unchunk(o, L):
    n, B, H, c, d = o.shape
    return jnp.moveaxis(o, (0, 3), (1, 2)).reshape(B, n * c, H, d)[:, :L]


def _gated_delta(q, k, v, beta, g, s0):
    f32 = jnp.float32
    L = k.shape[1]
    c = min(DN_CHUNK, L)
    qs, ks, vs = (_chunk(a, c) for a in (q, k, v))
    bs, gs = _chunk(beta, c), _chunk(g, c)
    incl = jnp.tril(jnp.ones((c, c), bool))
    strict = jnp.tril(jnp.ones((c, c), bool), -1)
    eye = jnp.eye(c, dtype=f32)

    def step(S, xs):
        qc, kc, vc, bc, gc = xs
        G = jnp.cumsum(gc, -1)
        dec = jnp.where(incl, jnp.exp(jnp.where(incl, G[..., :, None] - G[..., None, :], 0.0)), 0.0)
        kb = kc * bc[..., None]
        A = jnp.where(strict, jnp.einsum('bhtk,bhsk->bhts', kb, kc) * dec, 0.0)
        Tm = lax.linalg.triangular_solve(eye + A, jnp.broadcast_to(eye, A.shape), left_side=True, lower=True)
        u = Tm @ (vc * bc[..., None])
        w = Tm @ (kb * jnp.exp(G)[..., None])
        v_new = u - w @ S
        attn = jnp.einsum('bhtk,bhsk->bhts', qc, kc) * dec
        o = (qc * jnp.exp(G)[..., None]) @ S + attn @ v_new
        S = S * jnp.exp(G[..., -1])[..., None, None] + jnp.einsum(
            'bhtk,bhtv->bhkv', kc * jnp.exp(G[..., -1:] - G)[..., None], v_new)
        return S, o

    S, o = lax.scan(step, s0, (qs, ks, vs, bs, gs))
    return _unchunk(o, L), S


def _gla_scan(q, k, v, logf, s0):
    L = k.shape[1]
    c = min(HG_CHUNK, L)
    qs, ks, vs, fs = (_chunk(a, c) for a in (q, k, v, logf))
    incl = jnp.tril(jnp.ones((c, c), bool))[:, :, None]

    def step(S, xs):
        qc, kc, vc, fc = xs
        G = jnp.cumsum(fc, axis=-2)
        diff = G[:, :, :, None, :] - G[:, :, None, :, :]
        dec = jnp.where(incl, jnp.exp(jnp.where(incl, diff, 0.0)), 0.0)
        attn = jnp.einsum('bhtk,bhsk,bhtsk->bhts', qc, kc, dec)
        o = jnp.einsum('bhtk,bhkv->bhtv', qc * jnp.exp(G), S) + jnp.einsum('bhts,bhsv->bhtv', attn, vc)
        S = S * jnp.exp(G[:, :, -1])[..., None] + jnp.einsum(
            'bhsk,bhsv->bhkv', kc * jnp.exp(G[:, :, -1:] - G), vc)
        return S, o

    S, o = lax.scan(step, s0, (qs, ks, vs, fs))
    return _unchunk(o, L), S


def _short_conv(u, buf, w):
    L = u.shape[1]
    xc = jnp.concatenate([buf, u], 1)
    y = sum(xc[:, j:j + L] * w[j] for j in range(DN_CONV))
    return jax.nn.silu(y), xc[:, -(DN_CONV - 1):]


def _deltanet_mixer(qkv_raw, a_raw, b_raw, gate_raw, conv_buf, s0, conv_w, a_log, dt_bias, norm_w):
    B, L, _ = qkv_raw.shape
    if conv_buf is None:
        conv_buf = jnp.zeros((B, DN_CONV - 1, DN_CONV_DIM), qkv_raw.dtype)
    if s0 is None:
        s0 = jnp.zeros((B, DN_HEADS, DN_DK, DN_DV), qkv_raw.dtype)
    u, new_buf = _short_conv(qkv_raw, conv_buf, conv_w)
    if L % DN_CHUNK == 0:
        beta = jax.nn.sigmoid(b_raw)
        g = -jnp.exp(a_log) * jax.nn.softplus(a_raw + dt_bias)
        o, s_new = _dn_scan(u, g, beta, gate_raw, norm_w, s0)
        return o, new_buf, s_new
    q, k, v = _split(u, (DN_HEADS * DN_DK, DN_HEADS * DN_DK, DN_HEADS * DN_DV))
    q = _l2norm(q.reshape(B, L, DN_HEADS, DN_DK)) * DN_DK ** -0.5
    k = _l2norm(k.reshape(B, L, DN_HEADS, DN_DK))
    v = v.reshape(B, L, DN_HEADS, DN_DV)
    beta = jax.nn.sigmoid(b_raw)
    g = -jnp.exp(a_log) * jax.nn.softplus(a_raw + dt_bias)
    o, s_new = _gated_delta(q, k, v, beta, g, s0)
    return _gated_rmsnorm(o, gate_raw, norm_w), new_buf, s_new


def _hgrn2_mixer(q_raw, f_raw, i_raw, gate_raw, s0, lb, norm_w):
    B, L, _ = q_raw.shape
    if s0 is None:
        s0 = jnp.zeros((B, HG_HEADS, HG_DK, HG_DV), q_raw.dtype)
    if L % HG_CHUNK == 0:
        return _hg_scan(q_raw, f_raw, i_raw, gate_raw, lb, norm_w, s0)
    q = q_raw.reshape(B, L, HG_HEADS, HG_DK) * HG_DK ** -0.5
    f = lb + (1.0 - lb) * jax.nn.sigmoid(f_raw)
    logf = jnp.log(f).reshape(B, L, HG_HEADS, HG_DK)
    k = (1.0 - f).reshape(B, L, HG_HEADS, HG_DK)
    v = i_raw.reshape(B, L, HG_HEADS, HG_DV)
    o, s_new = _gla_scan(q, k, v, logf, s0)
    return _gated_rmsnorm(o, gate_raw, norm_w), s_new


def _moe(x, w_router, router_bias, wg, wu, wd, precise):
    f32 = jnp.float32
    N, D = x.shape
    eidx, gw = _router(x, w_router, router_bias, precise)
    n_assign = N * TOP_K
    e_flat = eidx.reshape(-1)
    onehot = (e_flat[:, None] == jnp.arange(N_EXPERTS)[None, :]).astype(jnp.int32)
    running = jnp.cumsum(onehot, axis=0)
    counts = running[-1]
    rank = jnp.take_along_axis(running, e_flat[:, None], axis=1)[:, 0] - 1
    padded = (counts + MOE_ROW_BLOCK - 1) // MOE_ROW_BLOCK * MOE_ROW_BLOCK
    pad_end = jnp.cumsum(padded)
    dest = ((pad_end - padded)[e_flat] + rank).astype(jnp.int32)
    n_blocks = -(-n_assign // MOE_ROW_BLOCK) + N_EXPERTS
    n_rows = n_blocks * MOE_ROW_BLOCK
    tok = jnp.full((n_rows,), N, jnp.int32).at[dest].set(jnp.arange(n_assign, dtype=jnp.int32) // TOP_K)
    wt = jnp.zeros((n_rows,), f32).at[dest].set(gw.reshape(-1))
    blk_exp = jnp.minimum(jnp.searchsorted(pad_end, jnp.arange(n_blocks) * MOE_ROW_BLOCK, side='right'),
                          N_EXPERTS - 1).astype(jnp.int32)
    x_pad = jnp.concatenate([x, jnp.zeros((1, D), x.dtype)])
    out = _expert_ffn(x_pad[tok], wt, blk_exp, wg, wu, wd, precise)
    slot = dest.reshape(N, TOP_K)
    return out[slot[:, 0]] + out[slot[:, 1]]


def _trunk_layer(x, ple, lb, past, precise, w_in, w_cmp1, w_cmp2, pe_cmp, dn_conv_w, dn_a_log, dn_dt_bias,
                 dn_norm_w, hg_norm_w, w_branch, w_out, ln1_g, ln1_b, ln2_g, ln2_b, w_router,
                 router_bias, w_exp_gate, w_exp_up, w_exp_down, w_ple, w_ple_gate, rel_bias):
    nsa_past, conv_buf, dn_state, hg_state = past
    B, L, D = x.shape
    x2 = x.reshape(B * L, D)
    mm = functools.partial(_mm, precise=precise)
    z = mm(x2, w_in).reshape(B, L, -1)
    (nsa_q, nsa_kv, nsa_g, dn_qkv, dn_a, dn_b, dn_g,
     hg_q, hg_f, hg_i, hg_g, merge) = _split(z, IN_SPLITS)
    o_a, rows_new, win_new = _nsa_mixer(nsa_q, nsa_kv, nsa_g, nsa_past, w_cmp1, w_cmp2, pe_cmp, rel_bias)
    o_b, conv_new, dn_new = _deltanet_mixer(dn_qkv, dn_a, dn_b, dn_g, conv_buf, dn_state, dn_conv_w,
                                            dn_a_log, dn_dt_bias, dn_norm_w)
    o_c, hg_new = _hgrn2_mixer(hg_q, hg_f, hg_i, hg_g, hg_state, lb, hg_norm_w)
    g = jax.nn.sigmoid(merge).reshape(B * L, 3, D)
    m = (g[:, 0] * mm(o_a.reshape(B * L, -1), w_branch[0]) + g[:, 1] * mm(o_b.reshape(B * L, -1), w_branch[1])
         + g[:, 2] * mm(o_c.reshape(B * L, -1), w_branch[2]))
    x2 = _layer_norm(DEEPNORM_ALPHA * x2 + mm(m, w_out), ln1_g, ln1_b)
    ffn = _moe(x2, w_router, router_bias, w_exp_gate, w_exp_up, w_exp_down, precise)
    ple_term = mm(ple.reshape(B * L, -1), w_ple) * jax.nn.sigmoid(mm(x2, w_ple_gate))
    x2 = _layer_norm(DEEPNORM_ALPHA * x2 + ffn + ple_term, ln2_g, ln2_b)
    return x2.reshape(B, L, D), (rows_new, win_new, conv_new, dn_new, hg_new)


def kernel(x_prompt, x_sample, cache_nsa, state_win, state_dn_conv, state_dn, state_hg, page_table,
           p_prompt, p_sample, w_in, w_cmp1, w_cmp2, pe_cmp, dn_conv_w, dn_a_log, dn_dt_bias, dn_norm_w,
           hg_lb_logits, hg_norm_w, w_branch, w_out, ln1_g, ln1_b, ln2_g, ln2_b, w_router, router_bias,
           w_exp_gate, w_exp_up, w_exp_down, w_ple, w_ple_gate, rel_bias):
    lb = jnp.cumsum(jax.nn.softmax(hg_lb_logits, axis=0), axis=0)
    lb = lb - lb[0:1]
    n_dec = page_table.shape[0]
    y_prompt, y_sample = x_prompt, x_sample
    st_p, st_s = [], []
    for l in range(DEPTH):
        lw = (w_in[l], w_cmp1[l], w_cmp2[l], pe_cmp[l], dn_conv_w[l], dn_a_log[l], dn_dt_bias[l],
              dn_norm_w[l], hg_norm_w[l], w_branch[l], w_out[l], ln1_g[l], ln1_b[l], ln2_g[l], ln2_b[l],
              w_router, router_bias, w_exp_gate[l], w_exp_up[l], w_exp_down[l], w_ple[l], w_ple_gate[l],
              rel_bias)
        y_prompt, s = _trunk_layer(y_prompt, p_prompt[l], lb[l], (None, None, None, None), False, *lw)
        st_p.append(s)
        y_sample, s = _trunk_layer(y_sample, p_sample[l], lb[l],
                                   ((cache_nsa, l, page_table, state_win[l]), state_dn_conv[l], state_dn[l],
                                    state_hg[l]),
                                   False, *lw)
        st_s.append(s)
    outs = [y_prompt, y_sample]
    for i in range(5):
        outs.append(jnp.stack([s[i] for s in st_p]))
        outs.append(jnp.stack([s[i] for s in st_s]))
    return tuple(outs)
```
